```python
import math
import jax
import jax.numpy as jnp
from jax import lax
import numpy as np

D_MODEL = 1024
BATCH = 8
SEQ = 8192
DEPTH = 2

CTX_LEN = 256
GRID_W = 64
N_HEADS = 8
N_KV_HEADS = 2
KV_GROUP = N_HEADS // N_KV_HEADS
HEAD_DIM = 128
ATTN_WIDTH = N_HEADS * HEAD_DIM
KV_WIDTH = N_KV_HEADS * HEAD_DIM
ROPE_AXIS_DIM = HEAD_DIM // 2
ROPE_THETA = 10000.0
Q_BLOCK = 128
HYENA_WIDTH = D_MODEL // 2
HYENA_PROJ = 3 * HYENA_WIDTH
SHORT_CONV = 3
FILTER_EMB = 33
FILTER_BANDS = (FILTER_EMB - 1) // 2
FILTER_HIDDEN = 64
FILTER_OUT_GAIN = 0.05
DECAY_TARGET = 1e-2
FAST_DECAY_PCT = 0.3
SLOW_DECAY_PCT = 1.5
N_BRANCHES = 2
GATE_WIDTH = N_BRANCHES * D_MODEL
PROJ_WIDTH = ATTN_WIDTH + 2 * KV_WIDTH + HYENA_PROJ + GATE_WIDTH
PROJ_SPLITS = (ATTN_WIDTH, ATTN_WIDTH + KV_WIDTH, ATTN_WIDTH + 2 * KV_WIDTH,
               ATTN_WIDTH + 2 * KV_WIDTH + HYENA_PROJ)
D_FF = (8 * D_MODEL + 3 * 256 - 1) // (3 * 256) * 256
N_MOD = 6
EPS = 1e-6

kernel_name = 'hybrid_gqa_hyena_diffusion_block'


def rms_norm(x, gain):
    xf = x.astype(jnp.float32)
    y = xf * lax.rsqrt(jnp.mean(xf * xf, axis=-1, keepdims=True) + EPS)
    return (y * gain.astype(jnp.float32)).astype(x.dtype)


def modulate(h, shift, scale):
    return h * (1 + scale) + shift


def split_heads_norm(t, gain, n_heads):
    t = t.reshape(t.shape[0], t.shape[1], n_heads, HEAD_DIM)
    return rms_norm(t, gain)


def axial_rope_tables(rows):
    row = jnp.repeat(jnp.arange(rows, dtype=jnp.float32), GRID_W)
    col = jnp.tile(jnp.arange(GRID_W, dtype=jnp.float32), rows)
    inv_freq = ROPE_THETA ** (-jnp.arange(0, ROPE_AXIS_DIM, 2, dtype=jnp.float32) / ROPE_AXIS_DIM)
    ang = jnp.concatenate([row[:, None] * inv_freq, col[:, None] * inv_freq], axis=-1)
    return jnp.cos(ang), jnp.sin(ang)


def apply_rope(t, cos, sin):
    tf = t.astype(jnp.float32).reshape(*t.shape[:-1], HEAD_DIM // 2, 2)
    c = cos[None, :, None, :]
    s = sin[None, :, None, :]
    t0, t1 = tf[..., 0], tf[..., 1]
    out = jnp.stack([t0 * c - t1 * s, t0 * s + t1 * c], axis=-1)
    return out.reshape(t.shape).astype(t.dtype)


def latent_attention(q, k, v, k_ctx, v_ctx):
    b, length = q.shape[0], q.shape[1]
    n_blocks = length // Q_BLOCK
    keys = jnp.concatenate([k_ctx, k], axis=1)
    vals = jnp.concatenate([v_ctx, v], axis=1)
    qb = q.reshape(b, n_blocks, Q_BLOCK, N_KV_HEADS, KV_GROUP, HEAD_DIM)
    qb = jnp.moveaxis(qb, 1, 0)
    scale = HEAD_DIM ** -0.5

    def one_block(q_blk):
        s = jnp.einsum('bqhgd,bkhd->bhgqk', q_blk, keys, preferred_element_type=jnp.float32) * scale
        p = jax.nn.softmax(s, axis=-1).astype(vals.dtype)
        return jnp.einsum('bhgqk,bkhd->bqhgd', p, vals)

    out = lax.map(one_block, qb)
    return jnp.moveaxis(out, 0, 1).reshape(b, length, ATTN_WIDTH)


def context_attention(q, k, v):
    b, length = q.shape[0], q.shape[1]
    qg = q.reshape(b, length, N_KV_HEADS, KV_GROUP, HEAD_DIM)
    s = jnp.einsum('bqhgd,bkhd->bhgqk', qg, k, preferred_element_type=jnp.float32) * HEAD_DIM ** -0.5
    p = jax.nn.softmax(s, axis=-1).astype(v.dtype)
    return jnp.einsum('bhgqk,bkhd->bqhgd', p, v).reshape(b, length, ATTN_WIDTH)


def short_conv(u, w, bias):
    up = jnp.pad(u, ((0, 0), (1, 1), (0, 0)))
    return up[:, :-2] * w[0] + up[:, 1:-1] * w[1] + up[:, 2:] * w[2] + bias


def hyena_filter(length, fw1, fb1, fw2, fb2, fw3, fb3, fw4, freq):
    t = jnp.linspace(0.0, 1.0, length, dtype=jnp.float32)[:, None]
    w = (2.0 * math.pi / length) * jnp.arange(length, dtype=jnp.float32)[:, None]
    f = jnp.linspace(1e-4, FILTER_BANDS - 1, FILTER_BANDS, dtype=jnp.float32)[None, :]
    z = jnp.concatenate([t, jnp.cos(f * w), -jnp.sin(f * w)], axis=-1)
    h = jnp.sin(freq[0] * (z @ fw1 + fb1))
    h = jnp.sin(freq[1] * (h @ fw2 + fb2))
    h = jnp.sin(freq[2] * (h @ fw3 + fb3))
    h = (h @ fw4).reshape(length, 2, HYENA_WIDTH)
    max_decay = math.log(DECAY_TARGET) / FAST_DECAY_PCT
    min_decay = math.log(DECAY_TARGET) / SLOW_DECAY_PCT
    deltas = jnp.abs(jnp.linspace(min_decay, max_decay, HYENA_WIDTH, dtype=jnp.float32))
    decay = jnp.exp(-t * deltas)
    h = h * decay[:, None, :]
    return h[:, 0], h[:, 1]


def bidirectional_long_conv(v, h_fwd, h_bwd):
    length = v.shape[1]
    n_fft = 2 * length
    k = jnp.concatenate([h_fwd, jnp.zeros((1, HYENA_WIDTH), h_fwd.dtype), h_bwd[:0:-1]], axis=0)
    v_f = jnp.fft.rfft(v.astype(jnp.float32), n=n_fft, axis=1)
    k_f = jnp.fft.rfft(k, n=n_fft, axis=0)
    return jnp.fft.irfft(v_f * k_f[None], n=n_fft, axis=1)[:, :length]


def hyena_mix(u, conv_w, conv_b, fw1, fb1, fw2, fb2, fw3, fb3, fw4, freq, bias):
    length = u.shape[1]
    x0, x1, v = jnp.split(short_conv(u, conv_w, conv_b), 3, axis=-1)
    v = v * x1
    h_fwd, h_bwd = hyena_filter(length, fw1, fb1, fw2, fb2, fw3, fb3, fw4, freq)
    y = bidirectional_long_conv(v, h_fwd, h_bwd) + v.astype(jnp.float32) * bias
    return (y * x0).astype(u.dtype)


def merge_branches(attn_out, hyena_out, gate_logits, w_o_attn, w_o_hyena, w_out):
    g_attn, g_hyena = jnp.split(jax.nn.sigmoid(gate_logits), N_BRANCHES, axis=-1)
    merged = g_attn * (attn_out @ w_o_attn) + g_hyena * (hyena_out @ w_o_hyena)
    return merged @ w_out


def swiglu(h, w_gate_up, w_down):
    g, u = jnp.split(h @ w_gate_up, 2, axis=-1)
    return (jax.nn.silu(g) * u) @ w_down


def setup_inputs(seed: int = 0) -> dict:
    key = jax.random.key(seed)
    ks = jax.random.split(key, 28)
    D = D_MODEL

    def nrm(k, shape, scale=1.0):
        return jax.random.normal(k, shape, jnp.float32) * scale

    return {
        'x': nrm(ks[0], (BATCH, SEQ, D)),
        'c': nrm(ks[1], (BATCH, D)),
        'ctx': nrm(ks[2], (BATCH, CTX_LEN, D)),
        'c_ctx': nrm(ks[3], (D,)),
        'w_mod': nrm(ks[4], (DEPTH, D, N_MOD * D), D ** -0.5),
        'b_mod': nrm(ks[5], (DEPTH, N_MOD * D), 0.01),
        'norm_mix': 1.0 + nrm(ks[6], (DEPTH, D), 0.05),
        'w_in': nrm(ks[7], (DEPTH, D, PROJ_WIDTH), D ** -0.5),
        'q_norm': 1.0 + nrm(ks[8], (DEPTH, HEAD_DIM), 0.05),
        'k_norm': 1.0 + nrm(ks[9], (DEPTH, HEAD_DIM), 0.05),
        'conv_w': nrm(ks[10], (DEPTH, SHORT_CONV, HYENA_PROJ), SHORT_CONV ** -0.5),
        'conv_b': nrm(ks[11], (DEPTH, HYENA_PROJ), 0.01),
        'filt_w1': nrm(ks[12], (DEPTH, FILTER_EMB, FILTER_HIDDEN), FILTER_EMB ** -0.5),
        'filt_b1': nrm(ks[13], (DEPTH, FILTER_HIDDEN), 0.1),
        'filt_w2': nrm(ks[14], (DEPTH, FILTER_HIDDEN, FILTER_HIDDEN), FILTER_HIDDEN ** -0.5),
        'filt_b2': nrm(ks[15], (DEPTH, FILTER_HIDDEN), 0.1),
        'filt_w3': nrm(ks[16], (DEPTH, FILTER_HIDDEN, FILTER_HIDDEN), FILTER_HIDDEN ** -0.5),
        'filt_b3': nrm(ks[17], (DEPTH, FILTER_HIDDEN), 0.1),
        'filt_w4': nrm(ks[18], (DEPTH, FILTER_HIDDEN, 2 * HYENA_WIDTH), FILTER_HIDDEN ** -0.5 * FILTER_OUT_GAIN),
        'filt_freq': 1.0 + nrm(ks[19], (DEPTH, 3, FILTER_HIDDEN), 0.1),
        'hyena_bias': nrm(ks[20], (DEPTH, HYENA_WIDTH), 0.5),
        'w_o_attn': nrm(ks[21], (DEPTH, ATTN_WIDTH, D), ATTN_WIDTH ** -0.5),
        'w_o_hyena': nrm(ks[22], (DEPTH, HYENA_WIDTH, D), HYENA_WIDTH ** -0.5),
        'w_out': nrm(ks[23], (DEPTH, D, D), D ** -0.5),
        'norm_ffn': 1.0 + nrm(ks[24], (DEPTH, D), 0.05),
        'w_gate_up': nrm(ks[25], (DEPTH, D, 2 * D_FF), D ** -0.5),
        'w_down': nrm(ks[26], (DEPTH, D_FF, D), D_FF ** -0.5),
        'norm_final': 1.0 + nrm(ks[27], (D,), 0.05),
    }


def reference(x, c, ctx, c_ctx, w_mod, b_mod, norm_mix, w_in, q_norm, k_norm, conv_w, conv_b,
              filt_w1, filt_b1, filt_w2, filt_b2, filt_w3, filt_b3, filt_w4, filt_freq, hyena_bias,
              w_o_attn, w_o_hyena, w_out, norm_ffn, w_gate_up, w_down, norm_final):
    b, length = x.shape[0], x.shape[1]
    rows = length // GRID_W
    cos, sin = axial_rope_tables(rows)
    silu_c = jax.nn.silu(c)
    silu_cc = jax.nn.silu(c_ctx)
    for layer in range(DEPTH):
        last = layer == DEPTH - 1
        mod = (silu_c @ w_mod[layer] + b_mod[layer])[:, None, :]
        mod_c = silu_cc @ w_mod[layer] + b_mod[layer]
        shift1, scale1, gate1, shift2, scale2, gate2 = jnp.split(mod, N_MOD, axis=-1)
        c_shift1, c_scale1, c_gate1, c_shift2, c_scale2, c_gate2 = jnp.split(mod_c, N_MOD, axis=-1)
        filt = (filt_w1[layer], filt_b1[layer], filt_w2[layer], filt_b2[layer],
                filt_w3[layer], filt_b3[layer], filt_w4[layer], filt_freq[layer])

        h_ctx = modulate(rms_norm(ctx, norm_mix[layer]), c_shift1, c_scale1)
        if last:
            kc, vc = jnp.split(h_ctx @ w_in[layer][:, ATTN_WIDTH:ATTN_WIDTH + 2 * KV_WIDTH], 2, axis=-1)
        else:
            qc, kc, vc, hyc, gc = jnp.split(h_ctx @ w_in[layer], PROJ_SPLITS, axis=-1)
        kc = split_heads_norm(kc, k_norm[layer], N_KV_HEADS)
        vc = vc.reshape(b, vc.shape[1], N_KV_HEADS, HEAD_DIM)

        h = modulate(rms_norm(x, norm_mix[layer]), shift1, scale1)
        q, k, v, hy, g = jnp.split(h @ w_in[layer], PROJ_SPLITS, axis=-1)
        q = apply_rope(split_heads_norm(q, q_norm[layer], N_HEADS), cos, sin)
        k = apply_rope(split_heads_norm(k, k_norm[layer], N_KV_HEADS), cos, sin)
        v = v.reshape(b, length, N_KV_HEADS, HEAD_DIM)
        attn = latent_attention(q, k, v, kc, vc)
        hyo = hyena_mix(hy, conv_w[layer], conv_b[layer], *filt, hyena_bias[layer])
        x_mix = merge_branches(attn, hyo, g, w_o_attn[layer], w_o_hyena[layer], w_out[layer])

        if not last:
            attn_c = context_attention(split_heads_norm(qc, q_norm[layer], N_HEADS), kc, vc)
            hyo_c = hyena_mix(hyc, conv_w[layer], conv_b[layer], *filt, hyena_bias[layer])
            ctx_mix = merge_branches(attn_c, hyo_c, gc, w_o_attn[layer], w_o_hyena[layer], w_out[layer])
            ctx = ctx + c_gate1 * ctx_mix
            ctx = ctx + c_gate2 * swiglu(modulate(rms_norm(ctx, norm_ffn[layer]), c_shift2, c_scale2),
                                         w_gate_up[layer], w_down[layer])

        x = x + gate1 * x_mix
        x = x + gate2 * swiglu(modulate(rms_norm(x, norm_ffn[layer]), shift2, scale2),
                               w_gate_up[layer], w_down[layer])
    return rms_norm(x, norm_final)
```

```python
import functools
import math

import numpy as np
import jax
import jax.numpy as jnp
from jax import lax
from jax.experimental import pallas as pl
from jax.experimental.pallas import tpu as pltpu

F32 = jnp.float32
BF16 = jnp.bfloat16

GRID_W = 64
N_HEADS = 8
N_KV_HEADS = 2
KV_GROUP = N_HEADS // N_KV_HEADS
HEAD_DIM = 128
ROPE_THETA = 10000.0
FILTER_EMB = 33
FILTER_BANDS = (FILTER_EMB - 1) // 2
DECAY_TARGET = 1e-2
FAST_DECAY_PCT = 0.3
SLOW_DECAY_PCT = 1.5
N_MOD = 6
EPS = 1e-6

LANES = 128
VMEM_LIMIT = 48 * 1024 * 1024
DFT_N1 = 128


def _params(*sem):
    return pltpu.CompilerParams(dimension_semantics=sem, vmem_limit_bytes=VMEM_LIMIT)


def _resident(shape):
    nd = len(shape)
    return pl.BlockSpec(shape, lambda *_: (0,) * nd, pipeline_mode=pl.Buffered(1))


def _dot(a, b):
    return jnp.dot(a, b, preferred_element_type=F32)


def _split_bf16(a):
    hi = a.astype(BF16)
    lo = (a - hi.astype(F32)).astype(BF16)
    return hi, lo


def _dot3(a, b):
    ah, al = _split_bf16(a)
    bh, bl = _split_bf16(b)
    return _dot(ah, bh) + (_dot(ah, bl) + _dot(al, bh))


def _rms(x):
    return x * lax.rsqrt(jnp.mean(x * x, axis=-1, keepdims=True) + EPS)


def _mod_kernel(c_ref, w_ref, b_ref, o_ref):
    c = c_ref[...]
    s = c * jax.nn.sigmoid(c)
    o_ref[...] = _dot3(s, w_ref[...]) + b_ref[...]


def _modulation(c_rows, w, b):
    m, d = c_rows.shape
    n = w.shape[1]
    tn = 1536
    return pl.pallas_call(
        _mod_kernel,
        grid=(n // tn,),
        in_specs=[pl.BlockSpec((m, d), lambda j: (0, 0)),
                  pl.BlockSpec((d, tn), lambda j: (0, j)),
                  pl.BlockSpec((1, tn), lambda j: (0, j))],
        out_specs=pl.BlockSpec((m, tn), lambda j: (0, j)),
        out_shape=jax.ShapeDtypeStruct((m, n), F32),
        compiler_params=_params("parallel"),
        name="modulation",
    )(c_rows, w, b.reshape(1, n))


def _inproj_kernel(x_ref, shift_ref, scale_ref, gain_ref, w_ref, qg_ref, kg_ref, cos_ref, sin_ref,
                   q_ref, k_ref, v_ref, hy_ref, g_ref, *, d_attn, d_kv, d_hy):
    x = x_ref[0]
    h = (_rms(x) * gain_ref[...]) * (1.0 + scale_ref[0]) + shift_ref[0]
    hb = h.astype(BF16)
    cosf = cos_ref[...]
    sinf = sin_ref[...]

    def head_norm_rope(t, gain):
        t = _rms(t) * gain
        return t * cosf + pltpu.roll(t, HEAD_DIM // 2, axis=1) * sinf

    c0 = 0
    q = _dot(hb, w_ref[:, c0:c0 + d_attn])
    for hd in range(d_attn // HEAD_DIM):
        sl = slice(hd * HEAD_DIM, (hd + 1) * HEAD_DIM)
        q_ref[0, :, sl] = head_norm_rope(q[:, sl], qg_ref[...]).astype(q_ref.dtype)
    c0 += d_attn
    k = _dot(hb, w_ref[:, c0:c0 + d_kv])
    for hd in range(d_kv // HEAD_DIM):
        sl = slice(hd * HEAD_DIM, (hd + 1) * HEAD_DIM)
        k_ref[0, :, sl] = head_norm_rope(k[:, sl], kg_ref[...]).astype(k_ref.dtype)
    c0 += d_kv
    v_ref[0] = _dot(hb, w_ref[:, c0:c0 + d_kv]).astype(v_ref.dtype)
    c0 += d_kv
    hy_ref[0] = _dot(hb, w_ref[:, c0:c0 + d_hy])
    c0 += d_hy
    g_ref[0] = _dot(hb, w_ref[:, c0:])


def _in_projection(x, shift, scale, gain, w_bf, qg, kg, cosf, sinf, *, tm, d_attn, d_kv, d_hy):
    b, length, d = x.shape
    d_gate = w_bf.shape[1] - d_attn - 2 * d_kv - d_hy
    row = lambda bi, i: (bi, i, 0)
    per_b = lambda bi, i: (bi, 0, 0)
    kern = functools.partial(_inproj_kernel, d_attn=d_attn, d_kv=d_kv, d_hy=d_hy)
    return pl.pallas_call(
        kern,
        grid=(b, length // tm),
        in_specs=[pl.BlockSpec((1, tm, d), row),
                  pl.BlockSpec((1, 1, d), per_b),
                  pl.BlockSpec((1, 1, d), per_b),
                  _resident((1, d)),
                  _resident(w_bf.shape),
                  _resident((1, HEAD_DIM)),
                  _resident((1, HEAD_DIM)),
                  pl.BlockSpec((tm, HEAD_DIM), lambda bi, i: (i, 0)),
                  pl.BlockSpec((tm, HEAD_DIM), lambda bi, i: (i, 0))],
        out_specs=[pl.BlockSpec((1, tm, d_attn), row),
                   pl.BlockSpec((1, tm, d_kv), row),
                   pl.BlockSpec((1, tm, d_kv), row),
                   pl.BlockSpec((1, tm, d_hy), row),
                   pl.BlockSpec((1, tm, d_gate), row)],
        out_shape=[jax.ShapeDtypeStruct((b, length, d_attn), BF16),
                   jax.ShapeDtypeStruct((b, length, d_kv), BF16),
                   jax.ShapeDtypeStruct((b, length, d_kv), BF16),
                   jax.ShapeDtypeStruct((b, length, d_hy), F32),
                   jax.ShapeDtypeStruct((b, length, d_gate), F32)],
        compiler_params=_params("parallel", "parallel"),
        name="in_projection",
    )(x, shift, scale, gain, w_bf, qg, kg, cosf, sinf)


def _flash_kernel(q_ref, k_ref, v_ref, o_ref, qs_ref, m_ref, l_ref, acc_ref, *, tq):
    j = pl.program_id(3)

    @pl.when(j == 0)
    def _():
        for g in range(KV_GROUP):
            qs_ref[g * tq:(g + 1) * tq, :] = q_ref[0, :, g * HEAD_DIM:(g + 1) * HEAD_DIM]
        m_ref[...] = jnp.full(m_ref.shape, -jnp.inf, F32)
        l_ref[...] = jnp.zeros(l_ref.shape, F32)
        acc_ref[...] = jnp.zeros(acc_ref.shape, F32)

    s = lax.dot_general(qs_ref[...], k_ref[0], (((1,), (1,)), ((), ())),
                        preferred_element_type=F32)
    m_prev = m_ref[...]
    m_new = jnp.maximum(m_prev, jnp.max(s, axis=-1, keepdims=True))
    alpha = jnp.exp(m_prev - m_new)
    p = jnp.exp(s - m_new)
    l_ref[...] = alpha * l_ref[...] + jnp.sum(p, axis=-1, keepdims=True)
    acc_ref[...] = alpha * acc_ref[...] + _dot(p.astype(BF16), v_ref[0])
    m_ref[...] = m_new

    @pl.when(j == pl.num_programs(3) - 1)
    def _():
        out = acc_ref[...] / l_ref[...]
        for g in range(KV_GROUP):
            o_ref[0, :, g * HEAD_DIM:(g + 1) * HEAD_DIM] = out[g * tq:(g + 1) * tq].astype(o_ref.dtype)


def _attention(q, k, v, *, tq, tk):
    b, lq, _ = q.shape
    lk = k.shape[1]
    gw = KV_GROUP * HEAD_DIM
    kern = functools.partial(_flash_kernel, tq=tq)
    return pl.pallas_call(
        kern,
        grid=(b, N_KV_HEADS, lq // tq, lk // tk),
        in_specs=[pl.BlockSpec((1, tq, gw), lambda bi, h, i, j: (bi, i, h)),
                  pl.BlockSpec((1, tk, HEAD_DIM), lambda bi, h, i, j: (bi, j, h)),
                  pl.BlockSpec((1, tk, HEAD_DIM), lambda bi, h, i, j: (bi, j, h))],
        out_specs=pl.BlockSpec((1, tq, gw), lambda bi, h, i, j: (bi, i, h)),
        out_shape=jax.ShapeDtypeStruct(q.shape, BF16),
        scratch_shapes=[pltpu.VMEM((KV_GROUP * tq, HEAD_DIM), BF16),
                        pltpu.VMEM((KV_GROUP * tq, 1), F32),
                        pltpu.VMEM((KV_GROUP * tq, 1), F32),
                        pltpu.VMEM((KV_GROUP * tq, HEAD_DIM), F32)],
        compiler_params=_params("parallel", "parallel", "parallel", "arbitrary"),
        name="attention",
    )(q, k, v)


def _shortconv_kernel(u_ref, prev_ref, next_ref, w_ref, b_ref, vv_ref, x0_ref, *, c):
    i = pl.program_id(1)
    u = u_ref[0]
    tl = u.shape[0]
    row = lax.broadcasted_iota(jnp.int32, (tl, 1), 0)
    prev_row = jnp.where(i == 0, 0.0, prev_ref[0, 7:8, :])
    next_row = jnp.where(i == pl.num_programs(1) - 1, 0.0, next_ref[0, 0:1, :])
    up = jnp.where(row == 0, prev_row, pltpu.roll(u, 1, axis=0))
    un = jnp.where(row == tl - 1, next_row, pltpu.roll(u, tl - 1, axis=0))
    y = up * w_ref[0:1, :] + u * w_ref[1:2, :] + un * w_ref[2:3, :] + b_ref[...]
    x0_ref[0, 0] = y[:, :c]
    vv_ref[0, 0] = y[:, 2 * c:] * y[:, c:2 * c]


def _short_conv_gate(u, conv_w, conv_b, *, tl):
    b, length, c3 = u.shape
    c = c3 // 3
    p = b // 2
    nblk8 = length // 8
    per = tl // 8
    out_map = lambda bi, i: (bi % p, bi // p, i, 0)
    kern = functools.partial(_shortconv_kernel, c=c)
    return pl.pallas_call(
        kern,
        grid=(b, length // tl),
        in_specs=[pl.BlockSpec((1, tl, c3), lambda bi, i: (bi, i, 0)),
                  pl.BlockSpec((1, 8, c3), lambda bi, i: (bi, jnp.maximum(i * per - 1, 0), 0)),
                  pl.BlockSpec((1, 8, c3), lambda bi, i: (bi, jnp.minimum((i + 1) * per, nblk8 - 1), 0)),
                  _resident((3, c3)),
                  _resident((1, c3))],
        out_specs=[pl.BlockSpec((1, 1, tl, c), out_map),
                   pl.BlockSpec((1, 1, tl, c), out_map)],
        out_shape=[jax.ShapeDtypeStruct((p, 2, length, c), F32),
                   jax.ShapeDtypeStruct((p, 2, length, c), F32)],
        compiler_params=_params("parallel", "parallel"),
        name="short_conv_gate",
    )(u, u, u, conv_w, conv_b.reshape(1, c3))


def _filter_kernel(z_ref, w1_ref, b1_ref, w2_ref, b2_ref, w3_ref, b3_ref, w4_ref, fr_ref, dec_ref,
                   hf_ref, hb_ref, *, c):
    h = jnp.sin(fr_ref[0:1, :] * (_dot3(z_ref[...], w1_ref[...]) + b1_ref[...]))
    h = jnp.sin(fr_ref[1:2, :] * (_dot3(h, w2_ref[...]) + b2_ref[...]))
    h = jnp.sin(fr_ref[2:3, :] * (_dot3(h, w3_ref[...]) + b3_ref[...]))
    out = _dot3(h, w4_ref[...])
    dec = dec_ref[...]
    hf_ref[...] = out[:, :c] * dec
    hb_ref[...] = out[:, c:] * dec


def _pad_to(a, shape):
    return jnp.pad(a, [(0, s - d) for d, s in zip(a.shape, shape)])


def _hyena_filter(length, fw1, fb1, fw2, fb2, fw3, fb3, fw4, freq, c):
    t = jnp.linspace(0.0, 1.0, length, dtype=F32)[:, None]
    w = (2.0 * math.pi / length) * jnp.arange(length, dtype=F32)[:, None]
    f = jnp.linspace(1e-4, FILTER_BANDS - 1, FILTER_BANDS, dtype=F32)[None, :]
    z = jnp.concatenate([t, jnp.cos(f * w), -jnp.sin(f * w)], axis=-1)
    max_decay = math.log(DECAY_TARGET) / FAST_DECAY_PCT
    min_decay = math.log(DECAY_TARGET) / SLOW_DECAY_PCT
    deltas = jnp.abs(jnp.linspace(min_decay, max_decay, c, dtype=F32))
    decay = jnp.exp(-t * deltas)

    hid = LANES
    z = _pad_to(z, (length, hid))
    w1 = _pad_to(fw1, (hid, hid))
    w2 = _pad_to(fw2, (hid, hid))
    w3 = _pad_to(fw3, (hid, hid))
    w4 = _pad_to(fw4, (hid, 2 * c))
    b1 = _pad_to(fb1.reshape(1, -1), (1, hid))
    b2 = _pad_to(fb2.reshape(1, -1), (1, hid))
    b3 = _pad_to(fb3.reshape(1, -1), (1, hid))
    fr = _pad_to(freq, (3, hid))
    tl = min(length, 1024)
    kern = functools.partial(_filter_kernel, c=c)
    return pl.pallas_call(
        kern,
        grid=(length // tl,),
        in_specs=[pl.BlockSpec((tl, hid), lambda i: (i, 0)),
                  _resident((hid, hid)), _resident((1, hid)),
                  _resident((hid, hid)), _resident((1, hid)),
                  _resident((hid, hid)), _resident((1, hid)),
                  _resident((hid, 2 * c)), _resident((3, hid)),
                  pl.BlockSpec((tl, c), lambda i: (i, 0))],
        out_specs=[pl.BlockSpec((tl, c), lambda i: (i, 0)),
                   pl.BlockSpec((tl, c), lambda i: (i, 0))],
        out_shape=[jax.ShapeDtypeStruct((length, c), F32),
                   jax.ShapeDtypeStruct((length, c), F32)],
        compiler_params=_params("parallel"),
        name="hyena_filter",
    )(z, w1, b1, w2, b2, w3, b3, w4, fr, decay)


def _dft_outer_kernel(*refs, nb, c, per_group, hyena):
    if hyena:
        t_ref, z_ref, vv_ref, x0_ref, bias_ref, o_ref = refs
    else:
        t_ref, z_ref, o_ref = refs
    for j in range(nb):
        cols = slice(j * c, (j + 1) * c)
        y = _dot(t_ref[j if per_group else 0], z_ref[0, :, cols].astype(BF16))
        if hyena:
            y = (y + vv_ref[0, :, cols] * bias_ref[...]) * x0_ref[0, :, cols]
        o_ref[0, :, cols] = y.astype(o_ref.dtype)


def _dft_outer(table, z, out_dtype, *, c, hyena_args=None):
    p, k, ncols = z.shape
    nt, m, _ = table.shape
    groups = ncols // c
    nb = min(groups, 8)
    per_group = nt > 1
    t_spec = (pl.BlockSpec((nb, m, k), lambda g, pi: (g, 0, 0)) if per_group
              else _resident((1, m, k)))
    col_spec = lambda rows: pl.BlockSpec((1, rows, nb * c), lambda g, pi: (pi, 0, g))
    in_specs = [t_spec, col_spec(k)]
    args = [table, z]
    if hyena_args is not None:
        vv, x0, bias = hyena_args
        in_specs += [col_spec(m), col_spec(m), _resident((1, c))]
        args += [vv, x0, bias]
    kern = functools.partial(_dft_outer_kernel, nb=nb, c=c, per_group=per_group,
                             hyena=hyena_args is not None)
    return pl.pallas_call(
        kern,
        grid=(groups // nb, p),
        in_specs=in_specs,
        out_specs=col_spec(m),
        out_shape=jax.ShapeDtypeStruct((p, m, ncols), out_dtype),
        compiler_params=_params("parallel", "parallel"),
        name="dft_outer",
    )(*args)


def _dft_inner_kernel(*refs, kc, n2, filtered):
    if filtered:
        m_ref, x_ref, f_ref, o_ref = refs
    else:
        m_ref, x_ref, o_ref = refs
    mat = m_ref[...]
    for kk in range(kc):
        zin = jnp.concatenate([x_ref[0, 0, kk], x_ref[0, 1, kk]], axis=0)
        y = _dot(mat, zin)
        re, im = y[:n2], y[n2:]
        if filtered:
            fr, fi = f_ref[0, 0, kk], f_ref[0, 1, kk]
            re, im = re * fr - im * fi, re * fi + im * fr
        o_ref[0, 0, kk] = re.astype(o_ref.dtype)
        o_ref[0, 1, kk] = im.astype(o_ref.dtype)


def _dft_inner(mat, x, out_dtype, spectrum=None):
    p, _, n1, n2, c = x.shape
    kc = 8
    blk = lambda g, pi: (pi, 0, g, 0, 0)
    in_specs = [_resident(mat.shape), pl.BlockSpec((1, 2, kc, n2, c), blk)]
    args = [mat, x]
    if spectrum is not None:
        in_specs.append(pl.BlockSpec((1, 2, kc, n2, c), lambda g, pi: (0, 0, g, 0, 0)))
        args.append(spectrum)
    kern = functools.partial(_dft_inner_kernel, kc=kc, n2=n2, filtered=spectrum is not None)
    return pl.pallas_call(
        kern,
        grid=(n1 // kc, p),
        in_specs=in_specs,
        out_specs=pl.BlockSpec((1, 2, kc, n2, c), blk),
        out_shape=jax.ShapeDtypeStruct(x.shape, out_dtype),
        compiler_params=_params("parallel", "parallel"),
        name="dft_inner",
    )(*args)


def _cmul_kernel(x_ref, f_ref, o_ref, *, n):
    xr, xi = x_ref[0, :n], x_ref[0, n:]
    fr, fi = f_ref[0, :n], f_ref[0, n:]
    o_ref[0, :n] = (xr * fr - xi * fi).astype(o_ref.dtype)
    o_ref[0, n:] = (xr * fi + xi * fr).astype(o_ref.dtype)


def _cmul(x, f, out_dtype):
    p, n2x, c = x.shape
    kern = functools.partial(_cmul_kernel, n=n2x // 2)
    return pl.pallas_call(
        kern,
        grid=(p,),
        in_specs=[pl.BlockSpec((1, n2x, c), lambda pi: (pi, 0, 0)),
                  pl.BlockSpec((1, n2x, c), lambda pi: (0, 0, 0))],
        out_specs=pl.BlockSpec((1, n2x, c), lambda pi: (pi, 0, 0)),
        out_shape=jax.ShapeDtypeStruct(x.shape, out_dtype),
        compiler_params=_params("parallel"),
        name="spectrum_product",
    )(x, f)


def _phase_tables(rows_n1, n1_total, n2_total):
    n = n1_total * n2_total
    n2 = jnp.arange(n2_total, dtype=jnp.int32)[:, None, None]
    k1 = jnp.arange(n1_total, dtype=jnp.int32)[None, :, None]
    n1 = jnp.arange(rows_n1, dtype=jnp.int32)[None, None, :]
    ph = ((n2_total * n1 + n2) * k1) % n
    ang = ph.astype(F32) * (2.0 * math.pi / n)
    return jnp.cos(ang), jnp.sin(ang)


def _forward_tables(length, n1_total, n2_total):
    rows = length // n2_total
    cs, sn = _phase_tables(rows, n1_total, n2_total)
    paired = jnp.concatenate([jnp.concatenate([cs, sn], axis=2),
                              jnp.concatenate([-sn, cs], axis=2)], axis=1)
    cs, sn = _phase_tables(n1_total, n1_total, n2_total)
    real = jnp.concatenate([cs, -sn], axis=1)
    return paired.astype(BF16), real.astype(BF16)


def _inverse_table(length, n1_total, n2_total):
    rows = length // n2_total
    cs, sn = _phase_tables(rows, n1_total, n2_total)
    cs = jnp.swapaxes(cs, 1, 2) / (n1_total * n2_total)
    sn = jnp.swapaxes(sn, 1, 2) / (n1_total * n2_total)
    tb = jnp.concatenate([jnp.concatenate([cs, -sn], axis=2),
                          jnp.concatenate([sn, cs], axis=2)], axis=1)
    return tb.astype(BF16)


def _inner_matrices(n2_total):
    idx = np.arange(n2_total)
    ang = 2.0 * np.pi * ((idx[:, None] * idx[None, :]) % n2_total) / n2_total
    cs, sn = np.cos(ang), np.sin(ang)
    fwd = np.block([[cs, sn], [-sn, cs]])
    inv = np.block([[cs, -sn], [sn, cs]])
    return jnp.asarray(fwd, dtype=BF16), jnp.asarray(inv, dtype=BF16)


def _long_conv_gate(vv, x0, h_fwd, h_bwd, bias):
    p, _, length, c = vv.shape
    n = 2 * length
    n1_total = DFT_N1 if n % DFT_N1 == 0 and n // DFT_N1 >= 8 else n
    n2_total = n // n1_total
    rows = length // n2_total
    t_pair, t_real = _forward_tables(length, n1_total, n2_total)
    t_inv = _inverse_table(length, n1_total, n2_total)

    filt = jnp.concatenate([h_fwd, jnp.zeros((1, c), F32), h_bwd[:0:-1]], axis=0)
    filt = filt.reshape(1, n1_total, n2_total * c)
    vv2 = vv.reshape(p, 2 * rows, n2_total * c)
    x02 = x0.reshape(p, 2 * rows, n2_total * c)
    bias2 = bias.reshape(1, c)

    if n2_total > 1:
        m_fwd, m_inv = _inner_matrices(n2_total)
        shape5 = lambda a: a.reshape(a.shape[0], 2, n1_total, n2_total, c)
        spec = _dft_outer(t_real, filt, BF16, c=c)
        spec = _dft_inner(m_fwd, shape5(spec), F32)
        y = _dft_outer(t_pair, vv2, BF16, c=c)
        y = _dft_inner(m_fwd, shape5(y), BF16, spectrum=spec)
        y = _dft_inner(m_inv, y, BF16)
        y = y.reshape(p, 2 * n1_total, n2_total * c)
    else:
        spec = _dft_outer(t_real, filt, F32, c=c)
        y = _dft_outer(t_pair, vv2, F32, c=c)
        y = _cmul(y, spec, BF16)
    out = _dft_outer(t_inv, y, BF16, c=c, hyena_args=(vv2, x02, bias2))
    return out.reshape(p, 2, length, c)


def _merge_kernel(x_ref, attn_ref, hy_ref, g_ref, gate_ref, woa_ref, woh_ref, wout_ref, o_ref, *, d):
    g = jax.nn.sigmoid(g_ref[0])
    merged = (g[:, :d] * _dot(attn_ref[0], woa_ref[...])
              + g[:, d:] * _dot(hy_ref[0, 0], woh_ref[...]))
    o_ref[0] = x_ref[0] + gate_ref[0] * _dot(merged.astype(BF16), wout_ref[...])


def _merge(x, attn, hyo, g, gate, woa, woh, wout, *, tm):
    b, length, d = x.shape
    p = b // 2
    c = hyo.shape[-1]
    row = lambda bi, i: (bi, i, 0)
    kern = functools.partial(_merge_kernel, d=d)
    return pl.pallas_call(
        kern,
        grid=(b, length // tm),
        in_specs=[pl.BlockSpec((1, tm, d), row),
                  pl.BlockSpec((1, tm, attn.shape[-1]), row),
                  pl.BlockSpec((1, 1, tm, c), lambda bi, i: (bi % p, bi // p, i, 0)),
                  pl.BlockSpec((1, tm, g.shape[-1]), row),
                  pl.BlockSpec((1, 1, d), lambda bi, i: (bi, 0, 0)),
                  _resident(woa.shape), _resident(woh.shape), _resident(wout.shape)],
        out_specs=pl.BlockSpec((1, tm, d), row),
        out_shape=jax.ShapeDtypeStruct(x.shape, F32),
        compiler_params=_params("parallel", "parallel"),
        name="merge",
    )(x, attn, hyo, g, gate, woa, woh, wout)


def _ffn_kernel(x_ref, shift_ref, scale_ref, gate_ref, gain_ref, wgu_ref, wd_ref, fin_ref, o_ref,
                *, d_ff, chunk, final_norm):
    x = x_ref[0]
    h = ((_rms(x) * gain_ref[...]) * (1.0 + scale_ref[0]) + shift_ref[0]).astype(BF16)
    acc = jnp.zeros(x.shape, F32)
    for c0 in range(0, d_ff, chunk):
        gt = _dot(h, wgu_ref[:, c0:c0 + chunk])
        up = _dot(h, wgu_ref[:, d_ff + c0:d_ff + c0 + chunk])
        act = (gt * jax.nn.sigmoid(gt) * up).astype(BF16)
        acc = acc + _dot(act, wd_ref[c0:c0 + chunk, :])
    y = x + gate_ref[0] * acc
    if final_norm:
        y = _rms(y) * fin_ref[...]
    o_ref[0] = y


def _ffn(x, shift, scale, gate, gain, wgu, wd, fin_gain, *, tm, final_norm):
    b, length, d = x.shape
    d_ff = wd.shape[0]
    row = lambda bi, i: (bi, i, 0)
    per_b = lambda bi, i: (bi, 0, 0)
    kern = functools.partial(_ffn_kernel, d_ff=d_ff, chunk=256, final_norm=final_norm)
    return pl.pallas_call(
        kern,
        grid=(b, length // tm),
        in_specs=[pl.BlockSpec((1, tm, d), row),
                  pl.BlockSpec((1, 1, d), per_b),
                  pl.BlockSpec((1, 1, d), per_b),
                  pl.BlockSpec((1, 1, d), per_b),
                  _resident((1, d)),
                  _resident(wgu.shape), _resident(wd.shape),
                  _resident((1, d))],
        out_specs=pl.BlockSpec((1, tm, d), row),
        out_shape=jax.ShapeDtypeStruct(x.shape, F32),
        compiler_params=_params("parallel", "parallel"),
        name="ffn",
    )(x, shift, scale, gate, gain, wgu, wd, fin_gain)


def _rope_tables(length):
    rows = length // GRID_W
    axis_dim = HEAD_DIM // 2
    row = jnp.repeat(jnp.arange(rows, dtype=F32), GRID_W)
    col = jnp.tile(jnp.arange(GRID_W, dtype=F32), rows)
    inv_freq = ROPE_THETA ** (-jnp.arange(0, axis_dim, 2, dtype=F32) / axis_dim)
    ang = jnp.concatenate([row[:, None] * inv_freq, col[:, None] * inv_freq], axis=-1)
    cs, sn = jnp.cos(ang), jnp.sin(ang)
    return jnp.concatenate([cs, cs], axis=-1), jnp.concatenate([-sn, sn], axis=-1)


def _head_perm(n_heads):
    within = np.concatenate([np.arange(0, HEAD_DIM, 2), np.arange(1, HEAD_DIM, 2)])
    return np.concatenate([h * HEAD_DIM + within for h in range(n_heads)]), within


def kernel(x, c, ctx, c_ctx, w_mod, b_mod, norm_mix, w_in, q_norm, k_norm, conv_w, conv_b, filt_w1, filt_b1, filt_w2, filt_b2, filt_w3, filt_b3, filt_w4, filt_freq, hyena_bias, w_o_attn, w_o_hyena, w_out, norm_ffn, w_gate_up, w_down, norm_final):
    b, length, d = x.shape
    ctx_len = ctx.shape[1]
    depth = w_mod.shape[0]
    d_attn = N_HEADS * HEAD_DIM
    d_kv = N_KV_HEADS * HEAD_DIM
    d_hy = conv_w.shape[-1]
    c_hy = d_hy // 3
    assert b % 2 == 0 and length % 512 == 0 and ctx_len % 256 == 0

    perm, within = _head_perm(N_HEADS + N_KV_HEADS)
    col_order = np.concatenate([perm, np.arange(d_attn + d_kv, w_in.shape[-1])])

    cosf, sinf = _rope_tables(length)
    cos_ctx = jnp.ones((ctx_len, HEAD_DIM), F32)
    sin_ctx = jnp.zeros((ctx_len, HEAD_DIM), F32)

    m_rows = 16
    c_rows = jnp.zeros((m_rows, d), F32).at[:b].set(c).at[b].set(c_ctx)
    tm = 512
    tm_ctx = ctx_len
    fin = norm_final.reshape(1, d)

    for layer in range(depth):
        last = layer == depth - 1
        mod_all = _modulation(c_rows, w_mod[layer], b_mod[layer])
        mods = [m.reshape(b, 1, d) for m in jnp.split(mod_all[:b], N_MOD, axis=-1)]
        mods_c = [jnp.broadcast_to(m.reshape(1, 1, d), (b, 1, d))
                  for m in jnp.split(mod_all[b], N_MOD, axis=-1)]
        shift1, scale1, gate1, shift2, scale2, gate2 = mods
        c_shift1, c_scale1, c_gate1, c_shift2, c_scale2, c_gate2 = mods_c

        w_in_bf = w_in[layer][:, col_order].astype(BF16)
        qg = (q_norm[layer][within] * HEAD_DIM ** -0.5).reshape(1, HEAD_DIM)
        kg = k_norm[layer][within].reshape(1, HEAD_DIM)
        gain_mix = norm_mix[layer].reshape(1, d)
        gain_ffn = norm_ffn[layer].reshape(1, d)
        woa = w_o_attn[layer].astype(BF16)
        woh = w_o_hyena[layer].astype(BF16)
        wout = w_out[layer].astype(BF16)
        wgu = w_gate_up[layer].astype(BF16)
        wd = w_down[layer].astype(BF16)
        filt = (filt_w1[layer], filt_b1[layer], filt_w2[layer], filt_b2[layer],
                filt_w3[layer], filt_b3[layer], filt_w4[layer], filt_freq[layer])
        proj = functools.partial(_in_projection, d_attn=d_attn, d_kv=d_kv, d_hy=d_hy)

        qc, kc, vc, hyc, gc = proj(ctx, c_shift1, c_scale1, gain_mix, w_in_bf, qg, kg,
                                   cos_ctx, sin_ctx, tm=tm_ctx)
        q, k, v, hy, g = proj(x, shift1, scale1, gain_mix, w_in_bf, qg, kg, cosf, sinf, tm=tm)

        attn = _attention(q, jnp.concatenate([kc, k], axis=1), jnp.concatenate([vc, v], axis=1),
                          tq=256, tk=768 if (ctx_len + length) % 768 == 0 else 256)
        vv, x0 = _short_conv_gate(hy, conv_w[layer], conv_b[layer], tl=512)
        h_fwd, h_bwd = _hyena_filter(length, *filt, c_hy)
        hyo = _long_conv_gate(vv, x0, h_fwd, h_bwd, hyena_bias[layer])
        x_mid = _merge(x, attn, hyo, g, gate1, woa, woh, wout, tm=tm)

        if not last:
            attn_c = _attention(qc, kc, vc, tq=ctx_len, tk=ctx_len)
            vv_c, x0_c = _short_conv_gate(hyc, conv_w[layer], conv_b[layer], tl=ctx_len)
            hc_fwd, hc_bwd = _hyena_filter(ctx_len, *filt, c_hy)
            hyo_c = _long_conv_gate(vv_c, x0_c, hc_fwd, hc_bwd, hyena_bias[layer])
            ctx = _merge(ctx, attn_c, hyo_c, gc, c_gate1, woa, woh, wout, tm=tm_ctx)
            ctx = _ffn(ctx, c_shift2, c_scale2, c_gate2, gain_ffn, wgu, wd, fin,
                       tm=tm_ctx, final_norm=False)

        x = _ffn(x_mid, shift2, scale2, gate2, gain_ffn, wgu, wd, fin, tm=tm, final_norm=last)
    return x
```

```python
import functools
import math

import numpy as np
import jax
import jax.numpy as jnp
from jax import lax
from jax.experimental import pallas as pl
from jax.experimental.pallas import tpu as pltpu

F32 = jnp.float32
BF16 = jnp.bfloat16

GRID_W = 64
N_HEADS = 8
N_KV_HEADS = 2
KV_GROUP = N_HEADS // N_KV_HEADS
HEAD_DIM = 128
ROPE_THETA = 10000.0
FILTER_EMB = 33
FILTER_BANDS = (FILTER_EMB - 1) // 2
DECAY_TARGET = 1e-2
FAST_DECAY_PCT = 0.3
SLOW_DECAY_PCT = 1.5
N_MOD = 6
EPS = 1e-6

LANES = 128
VMEM_LIMIT = 48 * 1024 * 1024
DFT_N1 = 128


def _params(*sem):
    return pltpu.CompilerParams(dimension_semantics=sem, vmem_limit_bytes=VMEM_LIMIT)


def _resident(shape):
    nd = len(shape)
    return pl.BlockSpec(shape, lambda *_: (0,) * nd, pipeline_mode=pl.Buffered(1))


def _dot(a, b):
    return jnp.dot(a, b, preferred_element_type=F32)


def _split_bf16(a):
    hi = a.astype(BF16)
    lo = (a - hi.astype(F32)).astype(BF16)
    return hi, lo


def _dot3(a, b):
    ah, al = _split_bf16(a)
    bh, bl = _split_bf16(b)
    return _dot(ah, bh) + (_dot(ah, bl) + _dot(al, bh))


def _rms(x):
    return x * lax.rsqrt(jnp.mean(x * x, axis=-1, keepdims=True) + EPS)


def _mod_kernel(c_ref, w_ref, b_ref, o_ref):
    c = c_ref[...]
    s = c * jax.nn.sigmoid(c)
    o_ref[...] = _dot3(s, w_ref[...]) + b_ref[...]


def _modulation(c_rows, w, b):
    m, d = c_rows.shape
    n = w.shape[1]
    tn = 1536
    return pl.pallas_call(
        _mod_kernel,
        grid=(n // tn,),
        in_specs=[pl.BlockSpec((m, d), lambda j: (0, 0)),
                  pl.BlockSpec((d, tn), lambda j: (0, j)),
                  pl.BlockSpec((1, tn), lambda j: (0, j))],
        out_specs=pl.BlockSpec((m, tn), lambda j: (0, j)),
        out_shape=jax.ShapeDtypeStruct((m, n), F32),
        compiler_params=_params("parallel"),
        name="modulation",
    )(c_rows, w, b.reshape(1, n))


def _inproj_kernel(x_ref, shift_ref, scale_ref, gain_ref, w_ref, qg_ref, kg_ref, cos_ref, sin_ref,
                   q_ref, k_ref, v_ref, hy_ref, g_ref, *, d_attn, d_kv, d_hy):
    x = x_ref[0]
    h = (_rms(x) * gain_ref[...]) * (1.0 + scale_ref[0]) + shift_ref[0]
    hb = h.astype(BF16)
    cosf = cos_ref[...]
    sinf = sin_ref[...]

    def head_norm_rope(t, gain):
        t = _rms(t) * gain
        return t * cosf + pltpu.roll(t, HEAD_DIM // 2, axis=1) * sinf

    c0 = 0
    q = _dot(hb, w_ref[:, c0:c0 + d_attn])
    for hd in range(d_attn // HEAD_DIM):
        sl = slice(hd * HEAD_DIM, (hd + 1) * HEAD_DIM)
        q_ref[0, :, sl] = head_norm_rope(q[:, sl], qg_ref[...]).astype(q_ref.dtype)
    c0 += d_attn
    k = _dot(hb, w_ref[:, c0:c0 + d_kv])
    for hd in range(d_kv // HEAD_DIM):
        sl = slice(hd * HEAD_DIM, (hd + 1) * HEAD_DIM)
        k_ref[0, :, sl] = head_norm_rope(k[:, sl], kg_ref[...]).astype(k_ref.dtype)
    c0 += d_kv
    v_ref[0] = _dot(hb, w_ref[:, c0:c0 + d_kv]).astype(v_ref.dtype)
    c0 += d_kv
    hy_ref[0] = _dot(hb, w_ref[:, c0:c0 + d_hy])
    c0 += d_hy
    g_ref[0] = _dot(hb, w_ref[:, c0:])


def _in_projection(x, shift, scale, gain, w_bf, qg, kg, cosf, sinf, *, tm, d_attn, d_kv, d_hy):
    b, length, d = x.shape
    d_gate = w_bf.shape[1] - d_attn - 2 * d_kv - d_hy
    row = lambda bi, i: (bi, i, 0)
    per_b = lambda bi, i: (bi, 0, 0)
    kern = functools.partial(_inproj_kernel, d_attn=d_attn, d_kv=d_kv, d_hy=d_hy)
    return pl.pallas_call(
        kern,
        grid=(b, length // tm),
        in_specs=[pl.BlockSpec((1, tm, d), row),
                  pl.BlockSpec((1, 1, d), per_b),
                  pl.BlockSpec((1, 1, d), per_b),
                  _resident((1, d)),
                  _resident(w_bf.shape),
                  _resident((1, HEAD_DIM)),
                  _resident((1, HEAD_DIM)),
                  pl.BlockSpec((tm, HEAD_DIM), lambda bi, i: (i, 0)),
                  pl.BlockSpec((tm, HEAD_DIM), lambda bi, i: (i, 0))],
        out_specs=[pl.BlockSpec((1, tm, d_attn), row),
                   pl.BlockSpec((1, tm, d_kv), row),
                   pl.BlockSpec((1, tm, d_kv), row),
                   pl.BlockSpec((1, tm, d_hy), row),
                   pl.BlockSpec((1, tm, d_gate), row)],
        out_shape=[jax.ShapeDtypeStruct((b, length, d_attn), BF16),
                   jax.ShapeDtypeStruct((b, length, d_kv), BF16),
                   jax.ShapeDtypeStruct((b, length, d_kv), BF16),
                   jax.ShapeDtypeStruct((b, length, d_hy), F32),
                   jax.ShapeDtypeStruct((b, length, d_gate), F32)],
        compiler_params=_params("parallel", "parallel"),
        name="in_projection",
    )(x, shift, scale, gain, w_bf, qg, kg, cosf, sinf)


def _flash_kernel(q_ref, k_ref, v_ref, o_ref, qs_ref, s_ref, m_ref, acc_ref, *, tq, tk, nk):
    for g in range(KV_GROUP):
        qs_ref[g * tq:(g + 1) * tq, :] = q_ref[0, :, g * HEAD_DIM:(g + 1) * HEAD_DIM]
    m_ref[...] = jnp.full(m_ref.shape, -1e30, F32)
    acc_ref[...] = jnp.zeros(acc_ref.shape, F32)
    nt = tk // LANES

    def chunk(ref, j):
        return ref[0, pl.ds(pl.multiple_of(j * tk, tk), tk), :]

    def scores(j, slot):
        s_ref[slot] = lax.dot_general(qs_ref[...], chunk(k_ref, j), (((1,), (1,)), ((), ())),
                                      preferred_element_type=F32)

    def consume(j, slot):
        tiles = [s_ref[slot, :, t * LANES:(t + 1) * LANES] for t in range(nt)]
        mt = functools.reduce(jnp.maximum, tiles)
        m_prev = m_ref[...]
        m_new = jnp.maximum(m_prev, jnp.max(mt, axis=-1, keepdims=True))
        alpha = jnp.exp2(m_prev - m_new)
        p = jnp.concatenate([jnp.exp2(t - m_new).astype(BF16) for t in tiles], axis=1)
        pv = _dot(p, chunk(v_ref, j))
        acc_ref[...] = jnp.concatenate([alpha, alpha], axis=1) * acc_ref[...] + pv
        m_ref[...] = m_new

    scores(0, 0)

    def pair(jj, carry):
        j = 2 * jj
        scores(j + 1, 1)
        consume(j, 0)
        scores(j + 2, 0)
        consume(j + 1, 1)
        return carry

    lax.fori_loop(0, nk // 2, pair, 0)
    consume(nk - 1, 0)

    acc = acc_ref[...]
    out = acc[:, :HEAD_DIM] / acc[:, HEAD_DIM:]
    for g in range(KV_GROUP):
        o_ref[0, :, g * HEAD_DIM:(g + 1) * HEAD_DIM] = out[g * tq:(g + 1) * tq].astype(o_ref.dtype)


def _augment_v(v):
    b, length, _ = v.shape
    v4 = v.reshape(b, length, N_KV_HEADS, HEAD_DIM)
    return jnp.concatenate([v4, jnp.ones_like(v4)], axis=-1).reshape(b, length, -1)


def _attention(q, k, v_aug, *, tq, tk):
    b, lq, _ = q.shape
    lk = k.shape[1]
    gw = KV_GROUP * HEAD_DIM
    rows = KV_GROUP * tq
    assert lk % tk == 0 and (lk // tk) % 2 == 1, "the chunk loop handles an odd chunk count"
    kern = functools.partial(_flash_kernel, tq=tq, tk=tk, nk=lk // tk)
    return pl.pallas_call(
        kern,
        grid=(b, N_KV_HEADS, lq // tq),
        in_specs=[pl.BlockSpec((1, tq, gw), lambda bi, h, i: (bi, i, h)),
                  pl.BlockSpec((1, lk, HEAD_DIM), lambda bi, h, i: (bi, 0, h)),
                  pl.BlockSpec((1, lk, 2 * HEAD_DIM), lambda bi, h, i: (bi, 0, h))],
        out_specs=pl.BlockSpec((1, tq, gw), lambda bi, h, i: (bi, i, h)),
        out_shape=jax.ShapeDtypeStruct(q.shape, BF16),
        scratch_shapes=[pltpu.VMEM((rows, HEAD_DIM), BF16),
                        pltpu.VMEM((2, rows, tk), F32),
                        pltpu.VMEM((rows, LANES), F32),
                        pltpu.VMEM((rows, 2 * HEAD_DIM), F32)],
        compiler_params=_params("parallel", "parallel", "parallel"),
        name="attention",
    )(q, k, v_aug)


def _shortconv_kernel(u_ref, prev_ref, next_ref, w_ref, b_ref, vv_ref, x0_ref, *, c):
    i = pl.program_id(1)
    u = u_ref[0]
    tl = u.shape[0]
    row = lax.broadcasted_iota(jnp.int32, (tl, 1), 0)
    prev_row = jnp.where(i == 0, 0.0, prev_ref[0, 7:8, :])
    next_row = jnp.where(i == pl.num_programs(1) - 1, 0.0, next_ref[0, 0:1, :])
    up = jnp.where(row == 0, prev_row, pltpu.roll(u, 1, axis=0))
    un = jnp.where(row == tl - 1, next_row, pltpu.roll(u, tl - 1, axis=0))
    y = up * w_ref[0:1, :] + u * w_ref[1:2, :] + un * w_ref[2:3, :] + b_ref[...]
    x0_ref[0, 0] = y[:, :c]
    vv_ref[0, 0] = y[:, 2 * c:] * y[:, c:2 * c]


def _short_conv_gate(u, conv_w, conv_b, *, tl):
    b, length, c3 = u.shape
    c = c3 // 3
    p = b // 2
    nblk8 = length // 8
    per = tl // 8
    out_map = lambda bi, i: (bi % p, bi // p, i, 0)
    kern = functools.partial(_shortconv_kernel, c=c)
    return pl.pallas_call(
        kern,
        grid=(b, length // tl),
        in_specs=[pl.BlockSpec((1, tl, c3), lambda bi, i: (bi, i, 0)),
                  pl.BlockSpec((1, 8, c3), lambda bi, i: (bi, jnp.maximum(i * per - 1, 0), 0)),
                  pl.BlockSpec((1, 8, c3), lambda bi, i: (bi, jnp.minimum((i + 1) * per, nblk8 - 1), 0)),
                  _resident((3, c3)),
                  _resident((1, c3))],
        out_specs=[pl.BlockSpec((1, 1, tl, c), out_map),
                   pl.BlockSpec((1, 1, tl, c), out_map)],
        out_shape=[jax.ShapeDtypeStruct((p, 2, length, c), F32),
                   jax.ShapeDtypeStruct((p, 2, length, c), F32)],
        compiler_params=_params("parallel", "parallel"),
        name="short_conv_gate",
    )(u, u, u, conv_w, conv_b.reshape(1, c3))


def _filter_kernel(z_ref, w1_ref, b1_ref, w2_ref, b2_ref, w3_ref, b3_ref, w4_ref, fr_ref, dec_ref,
                   o_ref):
    h = jnp.sin(fr_ref[0:1, :] * (_dot3(z_ref[...], w1_ref[...]) + b1_ref[...]))
    h = jnp.sin(fr_ref[1:2, :] * (_dot3(h, w2_ref[...]) + b2_ref[...]))
    h = jnp.sin(fr_ref[2:3, :] * (_dot3(h, w3_ref[...]) + b3_ref[...]))
    o_ref[...] = _dot3(h, w4_ref[0]) * dec_ref[...]


def _pad_to(a, shape):
    return jnp.pad(a, [(0, s - d) for d, s in zip(a.shape, shape)])


def _hyena_filter(length, fw1, fb1, fw2, fb2, fw3, fb3, fw4, freq, c):
    pos = jnp.arange(length, dtype=jnp.int32)
    src = jnp.concatenate([pos, (length - pos) % length])
    live = jnp.concatenate([jnp.ones((length,), F32),
                            (pos > 0).astype(F32)])[:, None]
    t = jnp.linspace(0.0, 1.0, length, dtype=F32)[src][:, None]
    w = ((2.0 * math.pi / length) * jnp.arange(length, dtype=F32))[src][:, None]
    f = jnp.linspace(1e-4, FILTER_BANDS - 1, FILTER_BANDS, dtype=F32)[None, :]
    z = jnp.concatenate([t, jnp.cos(f * w), -jnp.sin(f * w)], axis=-1)
    max_decay = math.log(DECAY_TARGET) / FAST_DECAY_PCT
    min_decay = math.log(DECAY_TARGET) / SLOW_DECAY_PCT
    deltas = jnp.abs(jnp.linspace(min_decay, max_decay, c, dtype=F32))
    decay = jnp.exp(-t * deltas) * live

    hid = LANES
    z = _pad_to(z, (2 * length, hid))
    w1 = _pad_to(fw1, (hid, hid))
    w2 = _pad_to(fw2, (hid, hid))
    w3 = _pad_to(fw3, (hid, hid))
    w4 = jnp.stack([_pad_to(fw4[:, :c], (hid, c)), _pad_to(fw4[:, c:], (hid, c))])
    b1 = _pad_to(fb1.reshape(1, -1), (1, hid))
    b2 = _pad_to(fb2.reshape(1, -1), (1, hid))
    b3 = _pad_to(fb3.reshape(1, -1), (1, hid))
    fr = _pad_to(freq, (3, hid))
    tl = min(length, 1024)
    per_half = length // tl
    return pl.pallas_call(
        _filter_kernel,
        grid=(2 * per_half,),
        in_specs=[pl.BlockSpec((tl, hid), lambda i: (i, 0)),
                  _resident((hid, hid)), _resident((1, hid)),
                  _resident((hid, hid)), _resident((1, hid)),
                  _resident((hid, hid)), _resident((1, hid)),
                  pl.BlockSpec((1, hid, c), lambda i: (i // per_half, 0, 0)),
                  _resident((3, hid)),
                  pl.BlockSpec((tl, c), lambda i: (i, 0))],
        out_specs=pl.BlockSpec((tl, c), lambda i: (i, 0)),
        out_shape=jax.ShapeDtypeStruct((2 * length, c), F32),
        compiler_params=_params("parallel"),
        name="hyena_filter",
    )(z, w1, b1, w2, b2, w3, b3, w4, fr, decay)


def _dft_outer_kernel(*refs, nb, c, per_group, hyena):
    if hyena:
        t_ref, z_ref, vv_ref, x0_ref, bias_ref, o_ref = refs
    else:
        t_ref, z_ref, o_ref = refs
    for j in range(nb):
        cols = slice(j * c, (j + 1) * c)
        y = _dot(t_ref[j if per_group else 0], z_ref[0, :, cols].astype(BF16))
        if hyena:
            y = (y + vv_ref[0, :, cols] * bias_ref[...]) * x0_ref[0, :, cols]
        o_ref[0, :, cols] = y.astype(o_ref.dtype)


def _dft_outer(table, z, out_dtype, *, c, hyena_args=None):
    p, k, ncols = z.shape
    nt, m, _ = table.shape
    groups = ncols // c
    nb = min(groups, 8)
    per_group = nt > 1
    t_spec = (pl.BlockSpec((nb, m, k), lambda g, pi: (g, 0, 0)) if per_group
              else _resident((1, m, k)))
    col_spec = lambda rows: pl.BlockSpec((1, rows, nb * c), lambda g, pi: (pi, 0, g))
    in_specs = [t_spec, col_spec(k)]
    args = [table, z]
    if hyena_args is not None:
        vv, x0, bias = hyena_args
        in_specs += [col_spec(m), col_spec(m), _resident((1, c))]
        args += [vv, x0, bias]
    kern = functools.partial(_dft_outer_kernel, nb=nb, c=c, per_group=per_group,
                             hyena=hyena_args is not None)
    return pl.pallas_call(
        kern,
        grid=(groups // nb, p),
        in_specs=in_specs,
        out_specs=col_spec(m),
        out_shape=jax.ShapeDtypeStruct((p, m, ncols), out_dtype),
        compiler_params=_params("parallel", "parallel"),
        name="dft_outer",
    )(*args)


def _dft_inner_kernel(*refs, kc, n2, filtered):
    if filtered:
        m_ref, x_ref, f_ref, o_ref = refs
    else:
        m_ref, x_ref, o_ref = refs
    mat = m_ref[...]
    for kk in range(kc):
        zin = jnp.concatenate([x_ref[0, 0, kk], x_ref[0, 1, kk]], axis=0)
        y = _dot(mat, zin)
        re, im = y[:n2], y[n2:]
        if filtered:
            fr, fi = f_ref[0, 0, kk], f_ref[0, 1, kk]
            re, im = re * fr - im * fi, re * fi + im * fr
        o_ref[0, 0, kk] = re.astype(o_ref.dtype)
        o_ref[0, 1, kk] = im.astype(o_ref.dtype)


def _dft_inner(mat, x, out_dtype, spectrum=None):
    p, _, n1, n2, c = x.shape
    kc = 8
    blk = lambda g, pi: (pi, 0, g, 0, 0)
    in_specs = [_resident(mat.shape), pl.BlockSpec((1, 2, kc, n2, c), blk)]
    args = [mat, x]
    if spectrum is not None:
        in_specs.append(pl.BlockSpec((1, 2, kc, n2, c), lambda g, pi: (0, 0, g, 0, 0)))
        args.append(spectrum)
    kern = functools.partial(_dft_inner_kernel, kc=kc, n2=n2, filtered=spectrum is not None)
    return pl.pallas_call(
        kern,
        grid=(n1 // kc, p),
        in_specs=in_specs,
        out_specs=pl.BlockSpec((1, 2, kc, n2, c), blk),
        out_shape=jax.ShapeDtypeStruct(x.shape, out_dtype),
        compiler_params=_params("parallel", "parallel"),
        name="dft_inner",
    )(*args)


def _cmul_kernel(x_ref, f_ref, o_ref, *, n):
    xr, xi = x_ref[0, :n], x_ref[0, n:]
    fr, fi = f_ref[0, :n], f_ref[0, n:]
    o_ref[0, :n] = (xr * fr - xi * fi).astype(o_ref.dtype)
    o_ref[0, n:] = (xr * fi + xi * fr).astype(o_ref.dtype)


def _cmul(x, f, out_dtype):
    p, n2x, c = x.shape
    kern = functools.partial(_cmul_kernel, n=n2x // 2)
    return pl.pallas_call(
        kern,
        grid=(p,),
        in_specs=[pl.BlockSpec((1, n2x, c), lambda pi: (pi, 0, 0)),
                  pl.BlockSpec((1, n2x, c), lambda pi: (0, 0, 0))],
        out_specs=pl.BlockSpec((1, n2x, c), lambda pi: (pi, 0, 0)),
        out_shape=jax.ShapeDtypeStruct(x.shape, out_dtype),
        compiler_params=_params("parallel"),
        name="spectrum_product",
    )(x, f)


def _phase_tables(rows_n1, n1_total, n2_total):
    n = n1_total * n2_total
    n2 = jnp.arange(n2_total, dtype=jnp.int32)[:, None, None]
    k1 = jnp.arange(n1_total, dtype=jnp.int32)[None, :, None]
    n1 = jnp.arange(rows_n1, dtype=jnp.int32)[None, None, :]
    ph = ((n2_total * n1 + n2) * k1) % n
    ang = ph.astype(F32) * (2.0 * math.pi / n)
    return jnp.cos(ang), jnp.sin(ang)


def _forward_tables(length, n1_total, n2_total):
    rows = length // n2_total
    cs, sn = _phase_tables(rows, n1_total, n2_total)
    paired = jnp.concatenate([jnp.concatenate([cs, sn], axis=2),
                              jnp.concatenate([-sn, cs], axis=2)], axis=1)
    cs, sn = _phase_tables(n1_total, n1_total, n2_total)
    real = jnp.concatenate([cs, -sn], axis=1)
    return paired.astype(BF16), real.astype(BF16)


def _inverse_table(length, n1_total, n2_total):
    rows = length // n2_total
    cs, sn = _phase_tables(rows, n1_total, n2_total)
    cs = jnp.swapaxes(cs, 1, 2) / (n1_total * n2_total)
    sn = jnp.swapaxes(sn, 1, 2) / (n1_total * n2_total)
    tb = jnp.concatenate([jnp.concatenate([cs, -sn], axis=2),
                          jnp.concatenate([sn, cs], axis=2)], axis=1)
    return tb.astype(BF16)


def _inner_matrices(n2_total):
    idx = np.arange(n2_total)
    ang = 2.0 * np.pi * ((idx[:, None] * idx[None, :]) % n2_total) / n2_total
    cs, sn = np.cos(ang), np.sin(ang)
    fwd = np.block([[cs, sn], [-sn, cs]])
    inv = np.block([[cs, -sn], [sn, cs]])
    return jnp.asarray(fwd, dtype=BF16), jnp.asarray(inv, dtype=BF16)


def _long_conv_gate(vv, x0, filt, bias):
    p, _, length, c = vv.shape
    n = 2 * length
    n1_total = DFT_N1 if n % DFT_N1 == 0 and n // DFT_N1 >= 8 else n
    n2_total = n // n1_total
    rows = length // n2_total
    t_pair, t_real = _forward_tables(length, n1_total, n2_total)
    t_inv = _inverse_table(length, n1_total, n2_total)

    filt = filt.reshape(1, n1_total, n2_total * c)
    vv2 = vv.reshape(p, 2 * rows, n2_total * c)
    x02 = x0.reshape(p, 2 * rows, n2_total * c)
    bias2 = bias.reshape(1, c)

    if n2_total > 1:
        m_fwd, m_inv = _inner_matrices(n2_total)
        shape5 = lambda a: a.reshape(a.shape[0], 2, n1_total, n2_total, c)
        spec = _dft_outer(t_real, filt, BF16, c=c)
        spec = _dft_inner(m_fwd, shape5(spec), F32)
        y = _dft_outer(t_pair, vv2, BF16, c=c)
        y = _dft_inner(m_fwd, shape5(y), BF16, spectrum=spec)
        y = _dft_inner(m_inv, y, BF16)
        y = y.reshape(p, 2 * n1_total, n2_total * c)
    else:
        spec = _dft_outer(t_real, filt, F32, c=c)
        y = _dft_outer(t_pair, vv2, F32, c=c)
        y = _cmul(y, spec, BF16)
    out = _dft_outer(t_inv, y, BF16, c=c, hyena_args=(vv2, x02, bias2))
    return out.reshape(p, 2, length, c)


def _merge_kernel(x_ref, attn_ref, hy_ref, g_ref, gate_ref, woa_ref, woh_ref, wout_ref, o_ref, *, d):
    g = jax.nn.sigmoid(g_ref[0])
    merged = (g[:, :d] * _dot(attn_ref[0], woa_ref[...])
              + g[:, d:] * _dot(hy_ref[0, 0], woh_ref[...]))
    o_ref[0] = x_ref[0] + gate_ref[0] * _dot(merged.astype(BF16), wout_ref[...])


def _merge(x, attn, hyo, g, gate, woa, woh, wout, *, tm):
    b, length, d = x.shape
    p = b // 2
    c = hyo.shape[-1]
    row = lambda bi, i: (bi, i, 0)
    kern = functools.partial(_merge_kernel, d=d)
    return pl.pallas_call(
        kern,
        grid=(b, length // tm),
        in_specs=[pl.BlockSpec((1, tm, d), row),
                  pl.BlockSpec((1, tm, attn.shape[-1]), row),
                  pl.BlockSpec((1, 1, tm, c), lambda bi, i: (bi % p, bi // p, i, 0)),
                  pl.BlockSpec((1, tm, g.shape[-1]), row),
                  pl.BlockSpec((1, 1, d), lambda bi, i: (bi, 0, 0)),
                  _resident(woa.shape), _resident(woh.shape), _resident(wout.shape)],
        out_specs=pl.BlockSpec((1, tm, d), row),
        out_shape=jax.ShapeDtypeStruct(x.shape, F32),
        compiler_params=_params("parallel", "parallel"),
        name="merge",
    )(x, attn, hyo, g, gate, woa, woh, wout)


def _ffn_kernel(x_ref, shift_ref, scale_ref, gate_ref, gain_ref, wgu_ref, wd_ref, fin_ref, o_ref,
                *, d_ff, chunk, final_norm):
    x = x_ref[0]
    h = ((_rms(x) * gain_ref[...]) * (1.0 + scale_ref[0]) + shift_ref[0]).astype(BF16)
    acc = jnp.zeros(x.shape, F32)
    for c0 in range(0, d_ff, chunk):
        gt = _dot(h, wgu_ref[:, c0:c0 + chunk])
        up = _dot(h, wgu_ref[:, d_ff + c0:d_ff + c0 + chunk])
        act = (gt * jax.nn.sigmoid(gt) * up).astype(BF16)
        acc = acc + _dot(act, wd_ref[c0:c0 + chunk, :])
    y = x + gate_ref[0] * acc
    if final_norm:
        y = _rms(y) * fin_ref[...]
    o_ref[0] = y


def _ffn(x, shift, scale, gate, gain, wgu, wd, fin_gain, *, tm, final_norm):
    b, length, d = x.shape
    d_ff = wd.shape[0]
    row = lambda bi, i: (bi, i, 0)
    per_b = lambda bi, i: (bi, 0, 0)
    kern = functools.partial(_ffn_kernel, d_ff=d_ff, chunk=256, final_norm=final_norm)
    return pl.pallas_call(
        kern,
        grid=(b, length // tm),
        in_specs=[pl.BlockSpec((1, tm, d), row),
                  pl.BlockSpec((1, 1, d), per_b),
                  pl.BlockSpec((1, 1, d), per_b),
                  pl.BlockSpec((1, 1, d), per_b),
                  _resident((1, d)),
                  _resident(wgu.shape), _resident(wd.shape),
                  _resident((1, d))],
        out_specs=pl.BlockSpec((1, tm, d), row),
        out_shape=jax.ShapeDtypeStruct(x.shape, F32),
        compiler_params=_params("parallel", "parallel"),
        name="ffn",
    )(x, shift, scale, gate, gain, wgu, wd, fin_gain)


def _rope_tables(length):
    rows = length // GRID_W
    axis_dim = HEAD_DIM // 2
    row = jnp.repeat(jnp.arange(rows, dtype=F32), GRID_W)
    col = jnp.tile(jnp.arange(GRID_W, dtype=F32), rows)
    inv_freq = ROPE_THETA ** (-jnp.arange(0, axis_dim, 2, dtype=F32) / axis_dim)
    ang = jnp.concatenate([row[:, None] * inv_freq, col[:, None] * inv_freq], axis=-1)
    cs, sn = jnp.cos(ang), jnp.sin(ang)
    return jnp.concatenate([cs, cs], axis=-1), jnp.concatenate([-sn, sn], axis=-1)


def _head_perm(n_heads):
    within = np.concatenate([np.arange(0, HEAD_DIM, 2), np.arange(1, HEAD_DIM, 2)])
    return np.concatenate([h * HEAD_DIM + within for h in range(n_heads)]), within


def kernel(x, c, ctx, c_ctx, w_mod, b_mod, norm_mix, w_in, q_norm, k_norm, conv_w, conv_b, filt_w1, filt_b1, filt_w2, filt_b2, filt_w3, filt_b3, filt_w4, filt_freq, hyena_bias, w_o_attn, w_o_hyena, w_out, norm_ffn, w_gate_up, w_down, norm_final):
    b, length, d = x.shape
    ctx_len = ctx.shape[1]
    depth = w_mod.shape[0]
    d_attn = N_HEADS * HEAD_DIM
    d_kv = N_KV_HEADS * HEAD_DIM
    d_hy = conv_w.shape[-1]
    c_hy = d_hy // 3
    assert b % 2 == 0 and length % 512 == 0 and ctx_len % 256 == 0

    perm, within = _head_perm(N_HEADS + N_KV_HEADS)
    col_order = np.concatenate([perm, np.arange(d_attn + d_kv, w_in.shape[-1])])

    cosf, sinf = _rope_tables(length)
    cos_ctx = jnp.ones((ctx_len, HEAD_DIM), F32)
    sin_ctx = jnp.zeros((ctx_len, HEAD_DIM), F32)

    m_rows = 16
    c_rows = jnp.zeros((m_rows, d), F32).at[:b].set(c).at[b].set(c_ctx)
    tm = 512
    tm_ctx = ctx_len
    fin = norm_final.reshape(1, d)

    for layer in range(depth):
        last = layer == depth - 1
        mod_all = _modulation(c_rows, w_mod[layer], b_mod[layer])
        mods = [m.reshape(b, 1, d) for m in jnp.split(mod_all[:b], N_MOD, axis=-1)]
        mods_c = [jnp.broadcast_to(m.reshape(1, 1, d), (b, 1, d))
                  for m in jnp.split(mod_all[b], N_MOD, axis=-1)]
        shift1, scale1, gate1, shift2, scale2, gate2 = mods
        c_shift1, c_scale1, c_gate1, c_shift2, c_scale2, c_gate2 = mods_c

        w_in_bf = w_in[layer][:, col_order].astype(BF16)
        qg = (q_norm[layer][within] * (HEAD_DIM ** -0.5 * math.log2(math.e))).reshape(1, HEAD_DIM)
        kg = k_norm[layer][within].reshape(1, HEAD_DIM)
        gain_mix = norm_mix[layer].reshape(1, d)
        gain_ffn = norm_ffn[layer].reshape(1, d)
        woa = w_o_attn[layer].astype(BF16)
        woh = w_o_hyena[layer].astype(BF16)
        wout = w_out[layer].astype(BF16)
        wgu = w_gate_up[layer].astype(BF16)
        wd = w_down[layer].astype(BF16)
        filt = (filt_w1[layer], filt_b1[layer], filt_w2[layer], filt_b2[layer],
                filt_w3[layer], filt_b3[layer], filt_w4[layer], filt_freq[layer])
        proj = functools.partial(_in_projection, d_attn=d_attn, d_kv=d_kv, d_hy=d_hy)

        qc, kc, vc, hyc, gc = proj(ctx, c_shift1, c_scale1, gain_mix, w_in_bf, qg, kg,
                                   cos_ctx, sin_ctx, tm=tm_ctx)
        q, k, v, hy, g = proj(x, shift1, scale1, gain_mix, w_in_bf, qg, kg, cosf, sinf, tm=tm)

        vc_aug = _augment_v(vc)
        attn = _attention(q, jnp.concatenate([kc, k], axis=1),
                          jnp.concatenate([vc_aug, _augment_v(v)], axis=1),
                          tq=256, tk=768 if (ctx_len + length) % 768 == 0 else 256)
        vv, x0 = _short_conv_gate(hy, conv_w[layer], conv_b[layer], tl=512)
        hyo = _long_conv_gate(vv, x0, _hyena_filter(length, *filt, c_hy), hyena_bias[layer])
        x_mid = _merge(x, attn, hyo, g, gate1, woa, woh, wout, tm=tm)

        if not last:
            attn_c = _attention(qc, kc, vc_aug, tq=ctx_len, tk=ctx_len)
            vv_c, x0_c = _short_conv_gate(hyc, conv_w[layer], conv_b[layer], tl=ctx_len)
            hyo_c = _long_conv_gate(vv_c, x0_c, _hyena_filter(ctx_len, *filt, c_hy),
                                    hyena_bias[layer])
            ctx = _merge(ctx, attn_c, hyo_c, gc, c_gate1, woa, woh, wout, tm=tm_ctx)
            ctx = _ffn(ctx, c_shift2, c_scale2, c_gate2, gain_ffn, wgu, wd, fin,
                       tm=tm_ctx, final_norm=False)

        x = _ffn(x_mid, shift2, scale2, gate2, gain_ffn, wgu, wd, fin, tm=tm, final_norm=last)
    return x
```

```python
import functools
import math

import numpy as np
import jax
import jax.numpy as jnp
from jax import lax
from jax.experimental import pallas as pl
from jax.experimental.pallas import tpu as pltpu

F32 = jnp.float32
BF16 = jnp.bfloat16

GRID_W = 64
N_HEADS = 8
N_KV_HEADS = 2
KV_GROUP = N_HEADS // N_KV_HEADS
HEAD_DIM = 128
ROPE_THETA = 10000.0
FILTER_EMB = 33
FILTER_BANDS = (FILTER_EMB - 1) // 2
DECAY_TARGET = 1e-2
FAST_DECAY_PCT = 0.3
SLOW_DECAY_PCT = 1.5
N_MOD = 6
EPS = 1e-6

LANES = 128
VMEM_LIMIT = 48 * 1024 * 1024
DFT_N1 = 128


def _params(*sem):
    return pltpu.CompilerParams(dimension_semantics=sem, vmem_limit_bytes=VMEM_LIMIT)


def _resident(shape):
    nd = len(shape)
    return pl.BlockSpec(shape, lambda *_: (0,) * nd, pipeline_mode=pl.Buffered(1))


def _dot(a, b):
    return jnp.dot(a, b, preferred_element_type=F32)


def _split_bf16(a):
    hi = a.astype(BF16)
    lo = (a - hi.astype(F32)).astype(BF16)
    return hi, lo


def _dot3(a, b):
    ah, al = _split_bf16(a)
    bh, bl = _split_bf16(b)
    return _dot(ah, bh) + (_dot(ah, bl) + _dot(al, bh))


def _rms(x):
    return x * lax.rsqrt(jnp.mean(x * x, axis=-1, keepdims=True) + EPS)


def _mod_kernel(c_ref, w_ref, b_ref, o_ref):
    c = c_ref[...]
    s = c * jax.nn.sigmoid(c)
    o_ref[...] = _dot3(s, w_ref[...]) + b_ref[...]


def _modulation(c_rows, w, b):
    m, d = c_rows.shape
    n = w.shape[1]
    tn = 1536
    return pl.pallas_call(
        _mod_kernel,
        grid=(n // tn,),
        in_specs=[pl.BlockSpec((m, d), lambda j: (0, 0)),
                  pl.BlockSpec((d, tn), lambda j: (0, j)),
                  pl.BlockSpec((1, tn), lambda j: (0, j))],
        out_specs=pl.BlockSpec((m, tn), lambda j: (0, j)),
        out_shape=jax.ShapeDtypeStruct((m, n), F32),
        compiler_params=_params("parallel"),
        name="modulation",
    )(c_rows, w, b.reshape(1, n))


def _inproj_kernel(x_ref, shift_ref, scale_ref, gain_ref, w_ref, qg_ref, kg_ref, cos_ref, sin_ref,
                   q_ref, k_ref, v_ref, hy_ref, g_ref, *, d_attn, d_kv, d_hy):
    x = x_ref[0]
    h = (_rms(x) * gain_ref[...]) * (1.0 + scale_ref[0]) + shift_ref[0]
    hb = h.astype(BF16)
    cosf = cos_ref[...]
    sinf = sin_ref[...]

    def head_norm_rope(t, gain):
        t = _rms(t) * gain
        return t * cosf + pltpu.roll(t, HEAD_DIM // 2, axis=1) * sinf

    c0 = 0
    q = _dot(hb, w_ref[:, c0:c0 + d_attn])
    for hd in range(d_attn // HEAD_DIM):
        sl = slice(hd * HEAD_DIM, (hd + 1) * HEAD_DIM)
        q_ref[0, :, sl] = head_norm_rope(q[:, sl], qg_ref[...]).astype(q_ref.dtype)
    c0 += d_attn
    k = _dot(hb, w_ref[:, c0:c0 + d_kv])
    for hd in range(d_kv // HEAD_DIM):
        sl = slice(hd * HEAD_DIM, (hd + 1) * HEAD_DIM)
        k_ref[0, :, sl] = head_norm_rope(k[:, sl], kg_ref[...]).astype(k_ref.dtype)
    c0 += d_kv
    v_ref[0] = _dot(hb, w_ref[:, c0:c0 + d_kv]).astype(v_ref.dtype)
    c0 += d_kv
    hy_ref[0] = _dot(hb, w_ref[:, c0:c0 + d_hy])
    c0 += d_hy
    g_ref[0] = _dot(hb, w_ref[:, c0:])


def _in_projection(x, shift, scale, gain, w_bf, qg, kg, cosf, sinf, *, tm, d_attn, d_kv, d_hy):
    b, length, d = x.shape
    d_gate = w_bf.shape[1] - d_attn - 2 * d_kv - d_hy
    row = lambda bi, i: (bi, i, 0)
    per_b = lambda bi, i: (bi, 0, 0)
    kern = functools.partial(_inproj_kernel, d_attn=d_attn, d_kv=d_kv, d_hy=d_hy)
    return pl.pallas_call(
        kern,
        grid=(b, length // tm),
        in_specs=[pl.BlockSpec((1, tm, d), row),
                  pl.BlockSpec((1, 1, d), per_b),
                  pl.BlockSpec((1, 1, d), per_b),
                  _resident((1, d)),
                  _resident(w_bf.shape),
                  _resident((1, HEAD_DIM)),
                  _resident((1, HEAD_DIM)),
                  pl.BlockSpec((tm, HEAD_DIM), lambda bi, i: (i, 0)),
                  pl.BlockSpec((tm, HEAD_DIM), lambda bi, i: (i, 0))],
        out_specs=[pl.BlockSpec((1, tm, d_attn), row),
                   pl.BlockSpec((1, tm, d_kv), row),
                   pl.BlockSpec((1, tm, d_kv), row),
                   pl.BlockSpec((1, tm, d_hy), row),
                   pl.BlockSpec((1, tm, d_gate), row)],
        out_shape=[jax.ShapeDtypeStruct((b, length, d_attn), BF16),
                   jax.ShapeDtypeStruct((b, length, d_kv), BF16),
                   jax.ShapeDtypeStruct((b, length, d_kv), BF16),
                   jax.ShapeDtypeStruct((b, length, d_hy), F32),
                   jax.ShapeDtypeStruct((b, length, d_gate), F32)],
        compiler_params=_params("parallel", "parallel"),
        name="in_projection",
    )(x, shift, scale, gain, w_bf, qg, kg, cosf, sinf)


def _flash_kernel(*refs, tq, tk, nk, extra):
    if extra:
        q_ref, k_ref, v_ref, ke_ref, ve_ref, o_ref, kall_ref, vall_ref, qs_ref, s_ref, m_ref, acc_ref = refs
    else:
        q_ref, k_ref, v_ref, o_ref, kall_ref, vall_ref, qs_ref, s_ref, m_ref, acc_ref = refs

    @pl.when(pl.program_id(2) == 0)
    def _():
        n_main = k_ref.shape[1]
        kall_ref[0:n_main, :] = k_ref[0]
        vall_ref[0:n_main, 0:HEAD_DIM] = v_ref[0]
        if extra:
            kall_ref[n_main:, :] = ke_ref[0]
            vall_ref[n_main:, 0:HEAD_DIM] = ve_ref[0]
        vall_ref[:, HEAD_DIM:] = jnp.ones((vall_ref.shape[0], HEAD_DIM), vall_ref.dtype)

    for g in range(KV_GROUP):
        qs_ref[g * tq:(g + 1) * tq, :] = q_ref[0, :, g * HEAD_DIM:(g + 1) * HEAD_DIM]
    m_ref[...] = jnp.full(m_ref.shape, -1e30, F32)
    acc_ref[...] = jnp.zeros(acc_ref.shape, F32)
    nt = tk // LANES

    def chunk(ref, j):
        return ref[pl.ds(pl.multiple_of(j * tk, tk), tk), :]

    def scores(j, slot):
        s_ref[slot] = lax.dot_general(qs_ref[...], chunk(kall_ref, j), (((1,), (1,)), ((), ())),
                                      preferred_element_type=F32)

    def consume(j, slot):
        tiles = [s_ref[slot, :, t * LANES:(t + 1) * LANES] for t in range(nt)]
        mt = functools.reduce(jnp.maximum, tiles)
        m_prev = m_ref[...]
        m_new = jnp.maximum(m_prev, jnp.max(mt, axis=-1, keepdims=True))
        alpha = jnp.exp2(m_prev - m_new)
        p = jnp.concatenate([jnp.exp2(t - m_new).astype(BF16) for t in tiles], axis=1)
        pv = _dot(p, chunk(vall_ref, j))
        acc_ref[...] = jnp.concatenate([alpha, alpha], axis=1) * acc_ref[...] + pv
        m_ref[...] = m_new

    scores(0, 0)

    def pair(jj, carry):
        j = 2 * jj
        scores(j + 1, 1)
        consume(j, 0)
        scores(j + 2, 0)
        consume(j + 1, 1)
        return carry

    lax.fori_loop(0, nk // 2, pair, 0)
    consume(nk - 1, 0)

    acc = acc_ref[...]
    out = acc[:, :HEAD_DIM] / acc[:, HEAD_DIM:]
    for g in range(KV_GROUP):
        o_ref[0, :, g * HEAD_DIM:(g + 1) * HEAD_DIM] = out[g * tq:(g + 1) * tq].astype(o_ref.dtype)


def _attention(q, k, v, k_extra=None, v_extra=None, *, tq, tk):
    b, lq, _ = q.shape
    extra = k_extra is not None
    lk = k.shape[1] + (k_extra.shape[1] if extra else 0)
    gw = KV_GROUP * HEAD_DIM
    rows = KV_GROUP * tq
    assert lk % tk == 0 and (lk // tk) % 2 == 1, "the chunk loop handles an odd chunk count"
    kv_spec = lambda a: pl.BlockSpec((1, a.shape[1], HEAD_DIM), lambda bi, h, i: (bi, 0, h))
    kv_args = [k, v] + ([k_extra, v_extra] if extra else [])
    kern = functools.partial(_flash_kernel, tq=tq, tk=tk, nk=lk // tk, extra=extra)
    return pl.pallas_call(
        kern,
        grid=(b, N_KV_HEADS, lq // tq),
        in_specs=[pl.BlockSpec((1, tq, gw), lambda bi, h, i: (bi, i, h))]
                 + [kv_spec(a) for a in kv_args],
        out_specs=pl.BlockSpec((1, tq, gw), lambda bi, h, i: (bi, i, h)),
        out_shape=jax.ShapeDtypeStruct(q.shape, BF16),
        scratch_shapes=[pltpu.VMEM((lk, HEAD_DIM), BF16),
                        pltpu.VMEM((lk, 2 * HEAD_DIM), BF16),
                        pltpu.VMEM((rows, HEAD_DIM), BF16),
                        pltpu.VMEM((2, rows, tk), F32),
                        pltpu.VMEM((rows, LANES), F32),
                        pltpu.VMEM((rows, 2 * HEAD_DIM), F32)],
        compiler_params=_params("parallel", "parallel", "arbitrary"),
        name="attention",
    )(q, *kv_args)


def _shortconv_kernel(u_ref, prev_ref, next_ref, w_ref, b_ref, vv_ref, x0_ref, *, c):
    i = pl.program_id(1)
    u = u_ref[0]
    tl = u.shape[0]
    row = lax.broadcasted_iota(jnp.int32, (tl, 1), 0)
    prev_row = jnp.where(i == 0, 0.0, prev_ref[0, 7:8, :])
    next_row = jnp.where(i == pl.num_programs(1) - 1, 0.0, next_ref[0, 0:1, :])
    up = jnp.where(row == 0, prev_row, pltpu.roll(u, 1, axis=0))
    un = jnp.where(row == tl - 1, next_row, pltpu.roll(u, tl - 1, axis=0))
    y = up * w_ref[0:1, :] + u * w_ref[1:2, :] + un * w_ref[2:3, :] + b_ref[...]
    x0_ref[0, 0] = y[:, :c]
    vv_ref[0, 0] = y[:, 2 * c:] * y[:, c:2 * c]


def _short_conv_gate(u, conv_w, conv_b, *, tl):
    b, length, c3 = u.shape
    c = c3 // 3
    p = b // 2
    nblk8 = length // 8
    per = tl // 8
    out_map = lambda bi, i: (bi % p, bi // p, i, 0)
    kern = functools.partial(_shortconv_kernel, c=c)
    return pl.pallas_call(
        kern,
        grid=(b, length // tl),
        in_specs=[pl.BlockSpec((1, tl, c3), lambda bi, i: (bi, i, 0)),
                  pl.BlockSpec((1, 8, c3), lambda bi, i: (bi, jnp.maximum(i * per - 1, 0), 0)),
                  pl.BlockSpec((1, 8, c3), lambda bi, i: (bi, jnp.minimum((i + 1) * per, nblk8 - 1), 0)),
                  _resident((3, c3)),
                  _resident((1, c3))],
        out_specs=[pl.BlockSpec((1, 1, tl, c), out_map),
                   pl.BlockSpec((1, 1, tl, c), out_map)],
        out_shape=[jax.ShapeDtypeStruct((p, 2, length, c), F32),
                   jax.ShapeDtypeStruct((p, 2, length, c), F32)],
        compiler_params=_params("parallel", "parallel"),
        name="short_conv_gate",
    )(u, u, u, conv_w, conv_b.reshape(1, c3))


def _filter_kernel(z_ref, w1_ref, b1_ref, w2_ref, b2_ref, w3_ref, b3_ref, w4_ref, fr_ref, dec_ref,
                   o_ref):
    h = jnp.sin(fr_ref[0:1, :] * (_dot3(z_ref[...], w1_ref[...]) + b1_ref[...]))
    h = jnp.sin(fr_ref[1:2, :] * (_dot3(h, w2_ref[...]) + b2_ref[...]))
    h = jnp.sin(fr_ref[2:3, :] * (_dot3(h, w3_ref[...]) + b3_ref[...]))
    o_ref[...] = _dot3(h, w4_ref[0]) * dec_ref[...]


def _pad_to(a, shape):
    return jnp.pad(a, [(0, s - d) for d, s in zip(a.shape, shape)])


def _hyena_filter(length, fw1, fb1, fw2, fb2, fw3, fb3, fw4, freq, c):
    mirror = lambda a: jnp.concatenate([a, a[:1], a[:0:-1]])[:, None]
    live = (jnp.arange(2 * length) != length).astype(F32)[:, None]
    t = mirror(jnp.linspace(0.0, 1.0, length, dtype=F32))
    w = mirror((2.0 * math.pi / length) * jnp.arange(length, dtype=F32))
    f = jnp.linspace(1e-4, FILTER_BANDS - 1, FILTER_BANDS, dtype=F32)[None, :]
    z = jnp.concatenate([t, jnp.cos(f * w), -jnp.sin(f * w)], axis=-1)
    max_decay = math.log(DECAY_TARGET) / FAST_DECAY_PCT
    min_decay = math.log(DECAY_TARGET) / SLOW_DECAY_PCT
    deltas = jnp.abs(jnp.linspace(min_decay, max_decay, c, dtype=F32))
    decay = jnp.exp(-t * deltas) * live

    hid = LANES
    z = _pad_to(z, (2 * length, hid))
    w1 = _pad_to(fw1, (hid, hid))
    w2 = _pad_to(fw2, (hid, hid))
    w3 = _pad_to(fw3, (hid, hid))
    w4 = jnp.stack([_pad_to(fw4[:, :c], (hid, c)), _pad_to(fw4[:, c:], (hid, c))])
    b1 = _pad_to(fb1.reshape(1, -1), (1, hid))
    b2 = _pad_to(fb2.reshape(1, -1), (1, hid))
    b3 = _pad_to(fb3.reshape(1, -1), (1, hid))
    fr = _pad_to(freq, (3, hid))
    tl = min(length, 1024)
    per_half = length // tl
    return pl.pallas_call(
        _filter_kernel,
        grid=(2 * per_half,),
        in_specs=[pl.BlockSpec((tl, hid), lambda i: (i, 0)),
                  _resident((hid, hid)), _resident((1, hid)),
                  _resident((hid, hid)), _resident((1, hid)),
                  _resident((hid, hid)), _resident((1, hid)),
                  pl.BlockSpec((1, hid, c), lambda i: (i // per_half, 0, 0)),
                  _resident((3, hid)),
                  pl.BlockSpec((tl, c), lambda i: (i, 0))],
        out_specs=pl.BlockSpec((tl, c), lambda i: (i, 0)),
        out_shape=jax.ShapeDtypeStruct((2 * length, c), F32),
        compiler_params=_params("parallel"),
        name="hyena_filter",
    )(z, w1, b1, w2, b2, w3, b3, w4, fr, decay)


def _dft_outer_kernel(*refs, nb, c, per_group, hyena):
    if hyena:
        t_ref, z_ref, vv_ref, x0_ref, bias_ref, o_ref = refs
    else:
        t_ref, z_ref, o_ref = refs
    for j in range(nb):
        cols = slice(j * c, (j + 1) * c)
        y = _dot(t_ref[j if per_group else 0], z_ref[0, :, cols].astype(BF16))
        if hyena:
            y = (y + vv_ref[0, :, cols] * bias_ref[...]) * x0_ref[0, :, cols]
        o_ref[0, :, cols] = y.astype(o_ref.dtype)


def _dft_outer(table, z, out_dtype, *, c, hyena_args=None):
    p, k, ncols = z.shape
    nt, m, _ = table.shape
    groups = ncols // c
    nb = min(groups, 8)
    per_group = nt > 1
    t_spec = (pl.BlockSpec((nb, m, k), lambda g, pi: (g, 0, 0)) if per_group
              else _resident((1, m, k)))
    col_spec = lambda rows: pl.BlockSpec((1, rows, nb * c), lambda g, pi: (pi, 0, g))
    in_specs = [t_spec, col_spec(k)]
    args = [table, z]
    if hyena_args is not None:
        vv, x0, bias = hyena_args
        in_specs += [col_spec(m), col_spec(m), _resident((1, c))]
        args += [vv, x0, bias]
    kern = functools.partial(_dft_outer_kernel, nb=nb, c=c, per_group=per_group,
                             hyena=hyena_args is not None)
    return pl.pallas_call(
        kern,
        grid=(groups // nb, p),
        in_specs=in_specs,
        out_specs=col_spec(m),
        out_shape=jax.ShapeDtypeStruct((p, m, ncols), out_dtype),
        compiler_params=_params("parallel", "parallel"),
        name="dft_outer",
    )(*args)


def _dft_strided_kernel(*refs, nb, hyena):
    if hyena:
        t_ref, z_ref, vv_ref, x0_ref, bias_ref, o_ref, zs_ref, os_ref, vs_ref, xs_ref = refs
    else:
        t_ref, z_ref, o_ref, zs_ref, os_ref = refs
    k, cb = z_ref.shape[1], z_ref.shape[3]
    m = o_ref.shape[1]
    nl = cb // LANES
    lane = lambda t: slice(t * LANES, (t + 1) * LANES)

    def stage(dst_ref, val):
        for t in range(nl):
            dst_ref[t] = val[:, :, lane(t)].reshape(val.shape[0] * nb, LANES)

    def rows(src_ref, j, n):
        return jnp.concatenate([src_ref[t, pl.ds(j, n, stride=nb), :] for t in range(nl)], axis=1)

    stage(zs_ref, z_ref[0].astype(F32))
    if hyena:
        stage(vs_ref, vv_ref[0])
        stage(xs_ref, x0_ref[0])
    for j in range(nb):
        y = _dot(t_ref[j], rows(zs_ref, j, k).astype(BF16))
        if hyena:
            y = (y + rows(vs_ref, j, m) * bias_ref[...]) * rows(xs_ref, j, m)
        for t in range(nl):
            os_ref[t, pl.ds(j, m, stride=nb), :] = y[:, lane(t)]
    o_ref[0] = jnp.concatenate([os_ref[t].reshape(m, nb, LANES) for t in range(nl)],
                               axis=2).astype(o_ref.dtype)


def _dft_outer_strided(table, z, out_dtype, *, cb, hyena_args=None):
    p, k, n2, c = z.shape
    _, m, _ = table.shape
    nb = 16
    blk = lambda rows: pl.BlockSpec((1, rows, nb, cb), lambda g, ci, pi: (pi, 0, g, ci))
    in_specs = [pl.BlockSpec((nb, m, k), lambda g, ci, pi: (g, 0, 0)), blk(k)]
    args = [table, z]
    staging = lambda rows: pltpu.VMEM((cb // LANES, rows * nb, LANES), F32)
    scratch = [staging(k), staging(m)]
    if hyena_args is not None:
        vv, x0, bias = hyena_args
        in_specs += [blk(m), blk(m), pl.BlockSpec((1, cb), lambda g, ci, pi: (0, ci))]
        args += [vv, x0, bias]
        scratch += [staging(m), staging(m)]
    kern = functools.partial(_dft_strided_kernel, nb=nb, hyena=hyena_args is not None)
    return pl.pallas_call(
        kern,
        grid=(n2 // nb, c // cb, p),
        in_specs=in_specs,
        out_specs=blk(m),
        out_shape=jax.ShapeDtypeStruct((p, m, n2, c), out_dtype),
        scratch_shapes=scratch,
        compiler_params=_params("parallel", "parallel", "parallel"),
        name="dft_outer_strided",
    )(*args)


def _dft_inner_kernel(*refs, kc, n2, filtered):
    if filtered:
        mf_ref, mi_ref, x_ref, f_ref, o_ref = refs
    else:
        mf_ref, x_ref, o_ref = refs
    for kk in range(kc):
        zin = jnp.concatenate([x_ref[0, 0, kk], x_ref[0, 1, kk]], axis=0)
        y = _dot(mf_ref[...], zin)
        re, im = y[:n2], y[n2:]
        if filtered:
            fr, fi = f_ref[0, 0, kk], f_ref[0, 1, kk]
            prod = jnp.concatenate([re * fr - im * fi, re * fi + im * fr], axis=0)
            y = _dot(mi_ref[...], prod.astype(BF16))
            re, im = y[:n2], y[n2:]
        o_ref[0, 0, kk] = re.astype(o_ref.dtype)
        o_ref[0, 1, kk] = im.astype(o_ref.dtype)


def _dft_inner(m_fwd, x, out_dtype, m_inv=None, spectrum=None):
    p, _, n1, n2, c = x.shape
    kc = 8
    blk = lambda g, pi: (pi, 0, g, 0, 0)
    filtered = spectrum is not None
    in_specs = [_resident(m_fwd.shape)]
    args = [m_fwd]
    if filtered:
        in_specs.append(_resident(m_inv.shape))
        args.append(m_inv)
    in_specs.append(pl.BlockSpec((1, 2, kc, n2, c), blk))
    args.append(x)
    if filtered:
        in_specs.append(pl.BlockSpec((1, 2, kc, n2, c), lambda g, pi: (0, 0, g, 0, 0)))
        args.append(spectrum)
    kern = functools.partial(_dft_inner_kernel, kc=kc, n2=n2, filtered=filtered)
    return pl.pallas_call(
        kern,
        grid=(n1 // kc, p),
        in_specs=in_specs,
        out_specs=pl.BlockSpec((1, 2, kc, n2, c), blk),
        out_shape=jax.ShapeDtypeStruct(x.shape, out_dtype),
        compiler_params=_params("parallel", "parallel"),
        name="dft_inner",
    )(*args)


def _cmul_kernel(x_ref, f_ref, o_ref, *, n):
    xr, xi = x_ref[0, :n], x_ref[0, n:]
    fr, fi = f_ref[0, :n], f_ref[0, n:]
    o_ref[0, :n] = (xr * fr - xi * fi).astype(o_ref.dtype)
    o_ref[0, n:] = (xr * fi + xi * fr).astype(o_ref.dtype)


def _cmul(x, f, out_dtype):
    p, n2x, c = x.shape
    kern = functools.partial(_cmul_kernel, n=n2x // 2)
    return pl.pallas_call(
        kern,
        grid=(p,),
        in_specs=[pl.BlockSpec((1, n2x, c), lambda pi: (pi, 0, 0)),
                  pl.BlockSpec((1, n2x, c), lambda pi: (0, 0, 0))],
        out_specs=pl.BlockSpec((1, n2x, c), lambda pi: (pi, 0, 0)),
        out_shape=jax.ShapeDtypeStruct(x.shape, out_dtype),
        compiler_params=_params("parallel"),
        name="spectrum_product",
    )(x, f)


def _phase_tables(rows_n1, n1_total, n2_total):
    n = n1_total * n2_total
    n2 = jnp.arange(n2_total, dtype=jnp.int32)[:, None, None]
    k1 = jnp.arange(n1_total, dtype=jnp.int32)[None, :, None]
    n1 = jnp.arange(rows_n1, dtype=jnp.int32)[None, None, :]
    ph = ((n2_total * n1 + n2) * k1) % n
    ang = ph.astype(F32) * (2.0 * math.pi / n)
    return jnp.cos(ang), jnp.sin(ang)


def _forward_tables(length, n1_total, n2_total):
    rows = length // n2_total
    cs, sn = _phase_tables(rows, n1_total, n2_total)
    paired = jnp.concatenate([jnp.concatenate([cs, sn], axis=2),
                              jnp.concatenate([-sn, cs], axis=2)], axis=1)
    cs, sn = _phase_tables(n1_total, n1_total, n2_total)
    real = jnp.concatenate([cs, -sn], axis=1)
    return paired.astype(BF16), real.astype(BF16)


def _inverse_table(length, n1_total, n2_total):
    rows = length // n2_total
    cs, sn = _phase_tables(rows, n1_total, n2_total)
    cs = jnp.swapaxes(cs, 1, 2) / (n1_total * n2_total)
    sn = jnp.swapaxes(sn, 1, 2) / (n1_total * n2_total)
    tb = jnp.concatenate([jnp.concatenate([cs, -sn], axis=2),
                          jnp.concatenate([sn, cs], axis=2)], axis=1)
    return tb.astype(BF16)


def _inner_matrices(n2_total):
    idx = np.arange(n2_total)
    ang = 2.0 * np.pi * ((idx[:, None] * idx[None, :]) % n2_total) / n2_total
    cs, sn = np.cos(ang), np.sin(ang)
    fwd = np.block([[cs, sn], [-sn, cs]])
    inv = np.block([[cs, -sn], [sn, cs]])
    return jnp.asarray(fwd, dtype=BF16), jnp.asarray(inv, dtype=BF16)


def _long_conv_gate(vv, x0, filt, bias):
    p, _, length, c = vv.shape
    n = 2 * length
    n1_total = DFT_N1 if n % DFT_N1 == 0 and n // DFT_N1 >= 8 else n
    n2_total = n // n1_total
    rows = length // n2_total
    t_pair, t_real = _forward_tables(length, n1_total, n2_total)
    t_inv = _inverse_table(length, n1_total, n2_total)

    bias2 = bias.reshape(1, c)
    if n2_total > 1:
        m_fwd, m_inv = _inner_matrices(n2_total)
        shape5 = lambda a: a.reshape(a.shape[0], 2, n1_total, n2_total, c)
        vv4 = vv.reshape(p, 2 * rows, n2_total, c)
        x04 = x0.reshape(p, 2 * rows, n2_total, c)
        spec = _dft_outer_strided(t_real, filt.reshape(1, n1_total, n2_total, c), BF16, cb=c)
        spec = _dft_inner(m_fwd, shape5(spec), F32)
        y = _dft_outer_strided(t_pair, vv4, BF16, cb=c)
        y = _dft_inner(m_fwd, shape5(y), BF16, m_inv=m_inv, spectrum=spec)
        y = y.reshape(p, 2 * n1_total, n2_total, c)
        out = _dft_outer_strided(t_inv, y, BF16, cb=c // 2, hyena_args=(vv4, x04, bias2))
    else:
        vv2 = vv.reshape(p, 2 * rows, c)
        x02 = x0.reshape(p, 2 * rows, c)
        spec = _dft_outer(t_real, filt.reshape(1, n1_total, c), F32, c=c)
        y = _dft_outer(t_pair, vv2, F32, c=c)
        y = _cmul(y, spec, BF16)
        out = _dft_outer(t_inv, y, BF16, c=c, hyena_args=(vv2, x02, bias2))
    return out.reshape(p, 2, length, c)


def _merge_kernel(x_ref, attn_ref, hy_ref, g_ref, gate_ref, woa_ref, woh_ref, wout_ref, o_ref, *, d):
    g = jax.nn.sigmoid(g_ref[0])
    merged = (g[:, :d] * _dot(attn_ref[0], woa_ref[...])
              + g[:, d:] * _dot(hy_ref[0, 0], woh_ref[...]))
    o_ref[0] = x_ref[0] + gate_ref[0] * _dot(merged.astype(BF16), wout_ref[...])


def _merge(x, attn, hyo, g, gate, woa, woh, wout, *, tm):
    b, length, d = x.shape
    p = b // 2
    c = hyo.shape[-1]
    row = lambda bi, i: (bi, i, 0)
    kern = functools.partial(_merge_kernel, d=d)
    return pl.pallas_call(
        kern,
        grid=(b, length // tm),
        in_specs=[pl.BlockSpec((1, tm, d), row),
                  pl.BlockSpec((1, tm, attn.shape[-1]), row),
                  pl.BlockSpec((1, 1, tm, c), lambda bi, i: (bi % p, bi // p, i, 0)),
                  pl.BlockSpec((1, tm, g.shape[-1]), row),
                  pl.BlockSpec((1, 1, d), lambda bi, i: (bi, 0, 0)),
                  _resident(woa.shape), _resident(woh.shape), _resident(wout.shape)],
        out_specs=pl.BlockSpec((1, tm, d), row),
        out_shape=jax.ShapeDtypeStruct(x.shape, F32),
        compiler_params=_params("parallel", "parallel"),
        name="merge",
    )(x, attn, hyo, g, gate, woa, woh, wout)


def _ffn_kernel(x_ref, shift_ref, scale_ref, gate_ref, gain_ref, wgu_ref, wd_ref, fin_ref, o_ref,
                *, d_ff, chunk, final_norm):
    x = x_ref[0]
    h = ((_rms(x) * gain_ref[...]) * (1.0 + scale_ref[0]) + shift_ref[0]).astype(BF16)
    acc = jnp.zeros(x.shape, F32)
    for c0 in range(0, d_ff, chunk):
        gt = _dot(h, wgu_ref[:, c0:c0 + chunk])
        up = _dot(h, wgu_ref[:, d_ff + c0:d_ff + c0 + chunk])
        act = (gt * jax.nn.sigmoid(gt) * up).astype(BF16)
        acc = acc + _dot(act, wd_ref[c0:c0 + chunk, :])
    y = x + gate_ref[0] * acc
    if final_norm:
        y = _rms(y) * fin_ref[...]
    o_ref[0] = y


def _ffn(x, shift, scale, gate, gain, wgu, wd, fin_gain, *, tm, final_norm):
    b, length, d = x.shape
    d_ff = wd.shape[0]
    row = lambda bi, i: (bi, i, 0)
    per_b = lambda bi, i: (bi, 0, 0)
    kern = functools.partial(_ffn_kernel, d_ff=d_ff, chunk=256, final_norm=final_norm)
    return pl.pallas_call(
        kern,
        grid=(b, length // tm),
        in_specs=[pl.BlockSpec((1, tm, d), row),
                  pl.BlockSpec((1, 1, d), per_b),
                  pl.BlockSpec((1, 1, d), per_b),
                  pl.BlockSpec((1, 1, d), per_b),
                  _resident((1, d)),
                  _resident(wgu.shape), _resident(wd.shape),
                  _resident((1, d))],
        out_specs=pl.BlockSpec((1, tm, d), row),
        out_shape=jax.ShapeDtypeStruct(x.shape, F32),
        compiler_params=_params("parallel", "parallel"),
        name="ffn",
    )(x, shift, scale, gate, gain, wgu, wd, fin_gain)


def _rope_tables(length):
    rows = length // GRID_W
    axis_dim = HEAD_DIM // 2
    row = jnp.repeat(jnp.arange(rows, dtype=F32), GRID_W)
    col = jnp.tile(jnp.arange(GRID_W, dtype=F32), rows)
    inv_freq = ROPE_THETA ** (-jnp.arange(0, axis_dim, 2, dtype=F32) / axis_dim)
    ang = jnp.concatenate([row[:, None] * inv_freq, col[:, None] * inv_freq], axis=-1)
    cs, sn = jnp.cos(ang), jnp.sin(ang)
    return jnp.concatenate([cs, cs], axis=-1), jnp.concatenate([-sn, sn], axis=-1)


def _head_perm(n_heads):
    within = np.concatenate([np.arange(0, HEAD_DIM, 2), np.arange(1, HEAD_DIM, 2)])
    return np.concatenate([h * HEAD_DIM + within for h in range(n_heads)]), within


def kernel(x, c, ctx, c_ctx, w_mod, b_mod, norm_mix, w_in, q_norm, k_norm, conv_w, conv_b, filt_w1, filt_b1, filt_w2, filt_b2, filt_w3, filt_b3, filt_w4, filt_freq, hyena_bias, w_o_attn, w_o_hyena, w_out, norm_ffn, w_gate_up, w_down, norm_final):
    b, length, d = x.shape
    ctx_len = ctx.shape[1]
    depth = w_mod.shape[0]
    d_attn = N_HEADS * HEAD_DIM
    d_kv = N_KV_HEADS * HEAD_DIM
    d_hy = conv_w.shape[-1]
    c_hy = d_hy // 3
    assert b % 2 == 0 and length % 512 == 0 and ctx_len % 256 == 0

    perm, within = _head_perm(N_HEADS + N_KV_HEADS)
    col_order = np.concatenate([perm, np.arange(d_attn + d_kv, w_in.shape[-1])])

    cosf, sinf = _rope_tables(length)
    cos_ctx = jnp.ones((ctx_len, HEAD_DIM), F32)
    sin_ctx = jnp.zeros((ctx_len, HEAD_DIM), F32)

    m_rows = 16
    c_rows = jnp.zeros((m_rows, d), F32).at[:b].set(c).at[b].set(c_ctx)
    tm = 512
    tm_ctx = ctx_len
    fin = norm_final.reshape(1, d)

    for layer in range(depth):
        last = layer == depth - 1
        mod_all = _modulation(c_rows, w_mod[layer], b_mod[layer])
        mods = [m.reshape(b, 1, d) for m in jnp.split(mod_all[:b], N_MOD, axis=-1)]
        mods_c = [jnp.broadcast_to(m.reshape(1, 1, d), (b, 1, d))
                  for m in jnp.split(mod_all[b], N_MOD, axis=-1)]
        shift1, scale1, gate1, shift2, scale2, gate2 = mods
        c_shift1, c_scale1, c_gate1, c_shift2, c_scale2, c_gate2 = mods_c

        w_in_bf = w_in[layer][:, col_order].astype(BF16)
        qg = (q_norm[layer][within] * (HEAD_DIM ** -0.5 * math.log2(math.e))).reshape(1, HEAD_DIM)
        kg = k_norm[layer][within].reshape(1, HEAD_DIM)
        gain_mix = norm_mix[layer].reshape(1, d)
        gain_ffn = norm_ffn[layer].reshape(1, d)
        woa = w_o_attn[layer].astype(BF16)
        woh = w_o_hyena[layer].astype(BF16)
        wout = w_out[layer].astype(BF16)
        wgu = w_gate_up[layer].astype(BF16)
        wd = w_down[layer].astype(BF16)
        filt = (filt_w1[layer], filt_b1[layer], filt_w2[layer], filt_b2[layer],
                filt_w3[layer], filt_b3[layer], filt_w4[layer], filt_freq[layer])
        proj = functools.partial(_in_projection, d_attn=d_attn, d_kv=d_kv, d_hy=d_hy)

        qc, kc, vc, hyc, gc = proj(ctx, c_shift1, c_scale1, gain_mix, w_in_bf, qg, kg,
                                   cos_ctx, sin_ctx, tm=tm_ctx)
        q, k, v, hy, g = proj(x, shift1, scale1, gain_mix, w_in_bf, qg, kg, cosf, sinf, tm=tm)

        attn = _attention(q, k, v, kc, vc,
                          tq=256, tk=768 if (ctx_len + length) % 768 == 0 else 256)
        vv, x0 = _short_conv_gate(hy, conv_w[layer], conv_b[layer], tl=512)
        hyo = _long_conv_gate(vv, x0, _hyena_filter(length, *filt, c_hy), hyena_bias[layer])
        x_mid = _merge(x, attn, hyo, g, gate1, woa, woh, wout, tm=tm)

        if not last:
            attn_c = _attention(qc, kc, vc, tq=ctx_len, tk=ctx_len)
            vv_c, x0_c = _short_conv_gate(hyc, conv_w[layer], conv_b[layer], tl=ctx_len)
            hyo_c = _long_conv_gate(vv_c, x0_c, _hyena_filter(ctx_len, *filt, c_hy),
                                    hyena_bias[layer])
            ctx = _merge(ctx, attn_c, hyo_c, gc, c_gate1, woa, woh, wout, tm=tm_ctx)
            ctx = _ffn(ctx, c_shift2, c_scale2, c_gate2, gain_ffn, wgu, wd, fin,
                       tm=tm_ctx, final_norm=False)

        x = _ffn(x_mid, shift2, scale2, gate2, gain_ffn, wgu, wd, fin, tm=tm, final_norm=last)
    return x
```

```python
import functools
import math

import numpy as np
import jax
import jax.numpy as jnp
from jax import lax
from jax.experimental import pallas as pl
from jax.experimental.pallas import tpu as pltpu

F32 = jnp.float32
BF16 = jnp.bfloat16

GRID_W = 64
N_HEADS = 8
N_KV_HEADS = 2
KV_GROUP = N_HEADS // N_KV_HEADS
HEAD_DIM = 128
ROPE_THETA = 10000.0
FILTER_EMB = 33
FILTER_BANDS = (FILTER_EMB - 1) // 2
DECAY_TARGET = 1e-2
FAST_DECAY_PCT = 0.3
SLOW_DECAY_PCT = 1.5
N_MOD = 6
EPS = 1e-6

LANES = 128
SUBLANES = 8
VMEM_LIMIT = 48 * 1024 * 1024
DFT_N1 = 128


def _params(*sem):
    return pltpu.CompilerParams(dimension_semantics=sem, vmem_limit_bytes=VMEM_LIMIT)


def _resident(shape):
    nd = len(shape)
    return pl.BlockSpec(shape, lambda *_: (0,) * nd, pipeline_mode=pl.Buffered(1))


def _dot(a, b):
    return jnp.dot(a, b, preferred_element_type=F32)


def _split_bf16(a):
    hi = a.astype(BF16)
    lo = (a - hi.astype(F32)).astype(BF16)
    return hi, lo


def _dot3(a, b):
    ah, al = _split_bf16(a)
    bh, bl = _split_bf16(b)
    return _dot(ah, bh) + (_dot(ah, bl) + _dot(al, bh))


def _rms(x):
    return x * lax.rsqrt(jnp.mean(x * x, axis=-1, keepdims=True) + EPS)


def _mod_kernel(c_ref, w_ref, b_ref, o_ref):
    c = c_ref[...]
    s = c * jax.nn.sigmoid(c)
    o_ref[...] = _dot3(s, w_ref[...]) + b_ref[...]


def _modulation(c_rows, w, b):
    m, d = c_rows.shape
    n = w.shape[1]
    tn = 1536
    return pl.pallas_call(
        _mod_kernel,
        grid=(n // tn,),
        in_specs=[pl.BlockSpec((m, d), lambda j: (0, 0)),
                  pl.BlockSpec((d, tn), lambda j: (0, j)),
                  pl.BlockSpec((1, tn), lambda j: (0, j))],
        out_specs=pl.BlockSpec((m, tn), lambda j: (0, j)),
        out_shape=jax.ShapeDtypeStruct((m, n), F32),
        compiler_params=_params("parallel"),
        name="modulation",
    )(c_rows, w, b.reshape(1, n))


def _inproj_kernel(x_ref, shift_ref, scale_ref, gain_ref, w_ref, qg_ref, kg_ref, cos_ref, sin_ref,
                   q_ref, k_ref, v_ref, hy_ref, g_ref, *, d_attn, d_kv, d_hy):
    x = x_ref[0]
    h = (_rms(x) * gain_ref[...]) * (1.0 + scale_ref[0]) + shift_ref[0]
    hb = h.astype(BF16)
    cosf = cos_ref[...]
    sinf = sin_ref[...]

    def head_norm_rope(t, gain):
        t = _rms(t) * gain
        return t * cosf + pltpu.roll(t, HEAD_DIM // 2, axis=1) * sinf

    c0 = 0
    q = _dot(hb, w_ref[:, c0:c0 + d_attn])
    for hd in range(d_attn // HEAD_DIM):
        sl = slice(hd * HEAD_DIM, (hd + 1) * HEAD_DIM)
        q_ref[0, :, sl] = head_norm_rope(q[:, sl], qg_ref[...]).astype(q_ref.dtype)
    c0 += d_attn
    k = _dot(hb, w_ref[:, c0:c0 + d_kv])
    for hd in range(d_kv // HEAD_DIM):
        sl = slice(hd * HEAD_DIM, (hd + 1) * HEAD_DIM)
        k_ref[0, :, sl] = head_norm_rope(k[:, sl], kg_ref[...]).astype(k_ref.dtype)
    c0 += d_kv
    v_ref[0] = _dot(hb, w_ref[:, c0:c0 + d_kv]).astype(v_ref.dtype)
    c0 += d_kv
    hy_ref[0] = _dot(hb, w_ref[:, c0:c0 + d_hy]).astype(hy_ref.dtype)
    c0 += d_hy
    g_ref[0] = _dot(hb, w_ref[:, c0:]).astype(g_ref.dtype)


def _in_projection(x, shift, scale, gain, w_bf, qg, kg, cosf, sinf, *, tm, d_attn, d_kv, d_hy):
    b, length, d = x.shape
    d_gate = w_bf.shape[1] - d_attn - 2 * d_kv - d_hy
    row = lambda bi, i: (bi, i, 0)
    per_b = lambda bi, i: (bi, 0, 0)
    kern = functools.partial(_inproj_kernel, d_attn=d_attn, d_kv=d_kv, d_hy=d_hy)
    return pl.pallas_call(
        kern,
        grid=(b, length // tm),
        in_specs=[pl.BlockSpec((1, tm, d), row),
                  pl.BlockSpec((1, 1, d), per_b),
                  pl.BlockSpec((1, 1, d), per_b),
                  _resident((1, d)),
                  _resident(w_bf.shape),
                  _resident((1, HEAD_DIM)),
                  _resident((1, HEAD_DIM)),
                  pl.BlockSpec((tm, HEAD_DIM), lambda bi, i: (i, 0)),
                  pl.BlockSpec((tm, HEAD_DIM), lambda bi, i: (i, 0))],
        out_specs=[pl.BlockSpec((1, tm, d_attn), row),
                   pl.BlockSpec((1, tm, d_kv), row),
                   pl.BlockSpec((1, tm, d_kv), row),
                   pl.BlockSpec((1, tm, d_hy), row),
                   pl.BlockSpec((1, tm, d_gate), row)],
        out_shape=[jax.ShapeDtypeStruct((b, length, d_attn), BF16),
                   jax.ShapeDtypeStruct((b, length, d_kv), BF16),
                   jax.ShapeDtypeStruct((b, length, d_kv), BF16),
                   jax.ShapeDtypeStruct((b, length, d_hy), BF16),
                   jax.ShapeDtypeStruct((b, length, d_gate), BF16)],
        compiler_params=_params("parallel", "parallel"),
        name="in_projection",
    )(x, shift, scale, gain, w_bf, qg, kg, cosf, sinf)


def _flash_kernel(*refs, tq, tk, nk, extra):
    if extra:
        q_ref, k_ref, v_ref, ke_ref, ve_ref, o_ref, kall_ref, vall_ref, qs_ref, s_ref, m_ref, acc_ref = refs
    else:
        q_ref, k_ref, v_ref, o_ref, kall_ref, vall_ref, qs_ref, s_ref, m_ref, acc_ref = refs

    @pl.when(pl.program_id(2) == 0)
    def _():
        n_main = k_ref.shape[1]
        kall_ref[0:n_main, :] = k_ref[0]
        vall_ref[0:n_main, 0:HEAD_DIM] = v_ref[0]
        if extra:
            kall_ref[n_main:, :] = ke_ref[0]
            vall_ref[n_main:, 0:HEAD_DIM] = ve_ref[0]
        vall_ref[:, HEAD_DIM:] = jnp.ones((vall_ref.shape[0], HEAD_DIM), vall_ref.dtype)

    for g in range(KV_GROUP):
        qs_ref[g * tq:(g + 1) * tq, :] = q_ref[0, :, g * HEAD_DIM:(g + 1) * HEAD_DIM]
    m_ref[...] = jnp.full(m_ref.shape, -1e30, F32)
    acc_ref[...] = jnp.zeros(acc_ref.shape, F32)
    nt = tk // LANES

    def chunk(ref, j):
        return ref[pl.ds(pl.multiple_of(j * tk, tk), tk), :]

    def scores(j, slot):
        s_ref[slot] = lax.dot_general(qs_ref[...], chunk(kall_ref, j), (((1,), (1,)), ((), ())),
                                      preferred_element_type=F32)

    def consume(j, slot):
        tiles = [s_ref[slot, :, t * LANES:(t + 1) * LANES] for t in range(nt)]
        mt = functools.reduce(jnp.maximum, tiles)
        m_prev = m_ref[...]
        m_new = jnp.maximum(m_prev, jnp.max(mt, axis=-1, keepdims=True))
        alpha = jnp.exp2(m_prev - m_new)
        p = jnp.concatenate([jnp.exp2(t - m_new).astype(BF16) for t in tiles], axis=1)
        pv = _dot(p, chunk(vall_ref, j))
        acc_ref[...] = jnp.concatenate([alpha, alpha], axis=1) * acc_ref[...] + pv
        m_ref[...] = m_new

    scores(0, 0)
    for j in range(nk - 1):
        scores(j + 1, (j + 1) % 2)
        consume(j, j % 2)
    consume(nk - 1, (nk - 1) % 2)

    acc = acc_ref[...]
    out = acc[:, :HEAD_DIM] / acc[:, HEAD_DIM:]
    for g in range(KV_GROUP):
        o_ref[0, :, g * HEAD_DIM:(g + 1) * HEAD_DIM] = out[g * tq:(g + 1) * tq].astype(o_ref.dtype)


def _attention(q, k, v, k_extra=None, v_extra=None, *, tq, tk):
    b, lq, _ = q.shape
    extra = k_extra is not None
    lk = k.shape[1] + (k_extra.shape[1] if extra else 0)
    gw = KV_GROUP * HEAD_DIM
    rows = KV_GROUP * tq
    assert lk % tk == 0 and (lk // tk) % 2 == 1, "the chunk loop handles an odd chunk count"
    kv_spec = lambda a: pl.BlockSpec((1, a.shape[1], HEAD_DIM), lambda bi, h, i: (bi, 0, h))
    kv_args = [k, v] + ([k_extra, v_extra] if extra else [])
    kern = functools.partial(_flash_kernel, tq=tq, tk=tk, nk=lk // tk, extra=extra)
    return pl.pallas_call(
        kern,
        grid=(b, N_KV_HEADS, lq // tq),
        in_specs=[pl.BlockSpec((1, tq, gw), lambda bi, h, i: (bi, i, h))]
                 + [kv_spec(a) for a in kv_args],
        out_specs=pl.BlockSpec((1, tq, gw), lambda bi, h, i: (bi, i, h)),
        out_shape=jax.ShapeDtypeStruct(q.shape, BF16),
        scratch_shapes=[pltpu.VMEM((lk, HEAD_DIM), BF16),
                        pltpu.VMEM((lk, 2 * HEAD_DIM), BF16),
                        pltpu.VMEM((rows, HEAD_DIM), BF16),
                        pltpu.VMEM((2, rows, tk), F32),
                        pltpu.VMEM((rows, LANES), F32),
                        pltpu.VMEM((rows, 2 * HEAD_DIM), F32)],
        compiler_params=_params("parallel", "parallel", "arbitrary"),
        name="attention",
    )(q, *kv_args)


def _shortconv_kernel(u_ref, prev_ref, next_ref, w_ref, b_ref, vv_ref, x0_ref, *, c):
    i = pl.program_id(1)
    u = u_ref[0].astype(F32)
    tl = u.shape[0]
    row = lax.broadcasted_iota(jnp.int32, (tl, 1), 0)
    halo = prev_ref.shape[1]
    prev_row = jnp.where(i == 0, 0.0, prev_ref[0, halo - 1:halo, :].astype(F32))
    next_row = jnp.where(i == pl.num_programs(1) - 1, 0.0, next_ref[0, 0:1, :].astype(F32))
    up = jnp.where(row == 0, prev_row, pltpu.roll(u, 1, axis=0))
    un = jnp.where(row == tl - 1, next_row, pltpu.roll(u, tl - 1, axis=0))
    y = up * w_ref[0:1, :] + u * w_ref[1:2, :] + un * w_ref[2:3, :] + b_ref[...]
    x0_ref[0, 0] = y[:, :c]
    vv_ref[0, 0] = y[:, 2 * c:] * y[:, c:2 * c]


def _short_conv_gate(u, conv_w, conv_b, *, tl):
    b, length, c3 = u.shape
    c = c3 // 3
    p = b // 2
    halo = 2 * SUBLANES if u.dtype == BF16 else SUBLANES
    n_halo = length // halo
    per = tl // halo
    out_map = lambda bi, i: (bi % p, bi // p, i, 0)
    kern = functools.partial(_shortconv_kernel, c=c)
    return pl.pallas_call(
        kern,
        grid=(b, length // tl),
        in_specs=[pl.BlockSpec((1, tl, c3), lambda bi, i: (bi, i, 0)),
                  pl.BlockSpec((1, halo, c3), lambda bi, i: (bi, jnp.maximum(i * per - 1, 0), 0)),
                  pl.BlockSpec((1, halo, c3),
                               lambda bi, i: (bi, jnp.minimum((i + 1) * per, n_halo - 1), 0)),
                  _resident((3, c3)),
                  _resident((1, c3))],
        out_specs=[pl.BlockSpec((1, 1, tl, c), out_map),
                   pl.BlockSpec((1, 1, tl, c), out_map)],
        out_shape=[jax.ShapeDtypeStruct((p, 2, length, c), F32),
                   jax.ShapeDtypeStruct((p, 2, length, c), F32)],
        compiler_params=_params("parallel", "parallel"),
        name="short_conv_gate",
    )(u, u, u, conv_w, conv_b.reshape(1, c3))


def _filter_kernel(z_ref, w1_ref, b1_ref, w2_ref, b2_ref, w3_ref, b3_ref, w4_ref, fr_ref, dec_ref,
                   o_ref):
    h = jnp.sin(fr_ref[0:1, :] * (_dot3(z_ref[...], w1_ref[...]) + b1_ref[...]))
    h = jnp.sin(fr_ref[1:2, :] * (_dot3(h, w2_ref[...]) + b2_ref[...]))
    h = jnp.sin(fr_ref[2:3, :] * (_dot3(h, w3_ref[...]) + b3_ref[...]))
    o_ref[...] = _dot3(h, w4_ref[0]) * dec_ref[...]


def _pad_to(a, shape):
    return jnp.pad(a, [(0, s - d) for d, s in zip(a.shape, shape)])


def _hyena_filter(length, fw1, fb1, fw2, fb2, fw3, fb3, fw4, freq, c):
    mirror = lambda a: jnp.concatenate([a, a[:1], a[:0:-1]])[:, None]
    live = (jnp.arange(2 * length) != length).astype(F32)[:, None]
    t = mirror(jnp.linspace(0.0, 1.0, length, dtype=F32))
    w = mirror((2.0 * math.pi / length) * jnp.arange(length, dtype=F32))
    f = jnp.linspace(1e-4, FILTER_BANDS - 1, FILTER_BANDS, dtype=F32)[None, :]
    z = jnp.concatenate([t, jnp.cos(f * w), -jnp.sin(f * w)], axis=-1)
    max_decay = math.log(DECAY_TARGET) / FAST_DECAY_PCT
    min_decay = math.log(DECAY_TARGET) / SLOW_DECAY_PCT
    deltas = jnp.abs(jnp.linspace(min_decay, max_decay, c, dtype=F32))
    decay = jnp.exp(-t * deltas) * live

    hid = LANES
    z = _pad_to(z, (2 * length, hid))
    w1 = _pad_to(fw1, (hid, hid))
    w2 = _pad_to(fw2, (hid, hid))
    w3 = _pad_to(fw3, (hid, hid))
    w4 = jnp.stack([_pad_to(fw4[:, :c], (hid, c)), _pad_to(fw4[:, c:], (hid, c))])
    b1 = _pad_to(fb1.reshape(1, -1), (1, hid))
    b2 = _pad_to(fb2.reshape(1, -1), (1, hid))
    b3 = _pad_to(fb3.reshape(1, -1), (1, hid))
    fr = _pad_to(freq, (3, hid))
    tl = min(length, 1024)
    per_half = length // tl
    return pl.pallas_call(
        _filter_kernel,
        grid=(2 * per_half,),
        in_specs=[pl.BlockSpec((tl, hid), lambda i: (i, 0)),
                  _resident((hid, hid)), _resident((1, hid)),
                  _resident((hid, hid)), _resident((1, hid)),
                  _resident((hid, hid)), _resident((1, hid)),
                  pl.BlockSpec((1, hid, c), lambda i: (i // per_half, 0, 0)),
                  _resident((3, hid)),
                  pl.BlockSpec((tl, c), lambda i: (i, 0))],
        out_specs=pl.BlockSpec((tl, c), lambda i: (i, 0)),
        out_shape=jax.ShapeDtypeStruct((2 * length, c), F32),
        compiler_params=_params("parallel"),
        name="hyena_filter",
    )(z, w1, b1, w2, b2, w3, b3, w4, fr, decay)


def _dft_outer_kernel(*refs, nb, c, per_group, hyena):
    if hyena:
        t_ref, z_ref, vv_ref, x0_ref, bias_ref, o_ref = refs
    else:
        t_ref, z_ref, o_ref = refs
    for j in range(nb):
        cols = slice(j * c, (j + 1) * c)
        y = _dot(t_ref[j if per_group else 0], z_ref[0, :, cols].astype(BF16))
        if hyena:
            y = (y + vv_ref[0, :, cols] * bias_ref[...]) * x0_ref[0, :, cols]
        o_ref[0, :, cols] = y.astype(o_ref.dtype)


def _dft_outer(table, z, out_dtype, *, c, hyena_args=None):
    p, k, ncols = z.shape
    nt, m, _ = table.shape
    groups = ncols // c
    nb = min(groups, 8)
    per_group = nt > 1
    t_spec = (pl.BlockSpec((nb, m, k), lambda g, pi: (g, 0, 0)) if per_group
              else _resident((1, m, k)))
    col_spec = lambda rows: pl.BlockSpec((1, rows, nb * c), lambda g, pi: (pi, 0, g))
    in_specs = [t_spec, col_spec(k)]
    args = [table, z]
    if hyena_args is not None:
        vv, x0, bias = hyena_args
        in_specs += [col_spec(m), col_spec(m), _resident((1, c))]
        args += [vv, x0, bias]
    kern = functools.partial(_dft_outer_kernel, nb=nb, c=c, per_group=per_group,
                             hyena=hyena_args is not None)
    return pl.pallas_call(
        kern,
        grid=(groups // nb, p),
        in_specs=in_specs,
        out_specs=col_spec(m),
        out_shape=jax.ShapeDtypeStruct((p, m, ncols), out_dtype),
        compiler_params=_params("parallel", "parallel"),
        name="dft_outer",
    )(*args)


def _dft_strided_kernel(*refs, nb, hyena):
    if hyena:
        t_ref, z_ref, vv_ref, x0_ref, bias_ref, o_ref, zs_ref, os_ref, vs_ref, xs_ref = refs
    else:
        t_ref, z_ref, o_ref, zs_ref, os_ref = refs
    k, cb = z_ref.shape[1], z_ref.shape[3]
    m = o_ref.shape[1]
    nl = cb // LANES
    ng = nb // SUBLANES
    lane = lambda t: slice(t * LANES, (t + 1) * LANES)

    def stage(dst_ref, val):
        for t in range(nl):
            for g in range(ng):
                piece = val[:, g * SUBLANES:(g + 1) * SUBLANES, lane(t)]
                dst_ref[t, g] = piece.reshape(val.shape[0] * SUBLANES, LANES)

    def rows(src_ref, j, n):
        g, r = divmod(j, SUBLANES)
        return jnp.concatenate([src_ref[t, g, pl.ds(r, n, stride=SUBLANES), :]
                                for t in range(nl)], axis=1)

    stage(zs_ref, z_ref[0].astype(F32))
    if hyena:
        stage(vs_ref, vv_ref[0])
        stage(xs_ref, x0_ref[0])
    for j in range(nb):
        y = _dot(t_ref[j], rows(zs_ref, j, k).astype(BF16))
        if hyena:
            y = (y + rows(vs_ref, j, m) * bias_ref[...]) * rows(xs_ref, j, m)
        g, r = divmod(j, SUBLANES)
        for t in range(nl):
            os_ref[t, g, pl.ds(r, m, stride=SUBLANES), :] = y[:, lane(t)]
    o_ref[0] = jnp.concatenate(
        [jnp.concatenate([os_ref[t, g].reshape(m, SUBLANES, LANES) for g in range(ng)], axis=1)
         for t in range(nl)], axis=2).astype(o_ref.dtype)


def _dft_outer_strided(table, z, out_dtype, *, cb, hyena_args=None):
    p, k, n2, c = z.shape
    _, m, _ = table.shape
    nb = 16
    blk = lambda rows: pl.BlockSpec((1, rows, nb, cb), lambda g, ci, pi: (pi, 0, g, ci))
    in_specs = [pl.BlockSpec((nb, m, k), lambda g, ci, pi: (g, 0, 0)), blk(k)]
    args = [table, z]
    staging = lambda rows: pltpu.VMEM((cb // LANES, nb // SUBLANES, rows * SUBLANES, LANES), F32)
    scratch = [staging(k), staging(m)]
    if hyena_args is not None:
        vv, x0, bias = hyena_args
        in_specs += [blk(m), blk(m), pl.BlockSpec((1, cb), lambda g, ci, pi: (0, ci))]
        args += [vv, x0, bias]
        scratch += [staging(m), staging(m)]
    kern = functools.partial(_dft_strided_kernel, nb=nb, hyena=hyena_args is not None)
    return pl.pallas_call(
        kern,
        grid=(n2 // nb, c // cb, p),
        in_specs=in_specs,
        out_specs=blk(m),
        out_shape=jax.ShapeDtypeStruct((p, m, n2, c), out_dtype),
        scratch_shapes=scratch,
        compiler_params=_params("parallel", "parallel", "parallel"),
        name="dft_outer_strided",
    )(*args)


def _dft_inner_kernel(*refs, kc, n2, filtered):
    if filtered:
        mf_ref, mi_ref, x_ref, f_ref, o_ref = refs
    else:
        mf_ref, x_ref, o_ref = refs
    for kk in range(kc):
        zin = jnp.concatenate([x_ref[0, 0, kk], x_ref[0, 1, kk]], axis=0)
        y = _dot(mf_ref[...], zin)
        re, im = y[:n2], y[n2:]
        if filtered:
            fr, fi = f_ref[0, 0, kk], f_ref[0, 1, kk]
            prod = jnp.concatenate([re * fr - im * fi, re * fi + im * fr], axis=0)
            y = _dot(mi_ref[...], prod.astype(BF16))
            re, im = y[:n2], y[n2:]
        o_ref[0, 0, kk] = re.astype(o_ref.dtype)
        o_ref[0, 1, kk] = im.astype(o_ref.dtype)


def _dft_inner(m_fwd, x, out_dtype, m_inv=None, spectrum=None):
    p, _, n1, n2, c = x.shape
    kc = 8
    blk = lambda g, pi: (pi, 0, g, 0, 0)
    filtered = spectrum is not None
    in_specs = [_resident(m_fwd.shape)]
    args = [m_fwd]
    if filtered:
        in_specs.append(_resident(m_inv.shape))
        args.append(m_inv)
    in_specs.append(pl.BlockSpec((1, 2, kc, n2, c), blk))
    args.append(x)
    if filtered:
        in_specs.append(pl.BlockSpec((1, 2, kc, n2, c), lambda g, pi: (0, 0, g, 0, 0)))
        args.append(spectrum)
    kern = functools.partial(_dft_inner_kernel, kc=kc, n2=n2, filtered=filtered)
    return pl.pallas_call(
        kern,
        grid=(n1 // kc, p),
        in_specs=in_specs,
        out_specs=pl.BlockSpec((1, 2, kc, n2, c), blk),
        out_shape=jax.ShapeDtypeStruct(x.shape, out_dtype),
        compiler_params=_params("parallel", "parallel"),
        name="dft_inner",
    )(*args)


def _cmul_kernel(x_ref, f_ref, o_ref, *, n):
    xr, xi = x_ref[0, :n], x_ref[0, n:]
    fr, fi = f_ref[0, :n], f_ref[0, n:]
    o_ref[0, :n] = (xr * fr - xi * fi).astype(o_ref.dtype)
    o_ref[0, n:] = (xr * fi + xi * fr).astype(o_ref.dtype)


def _cmul(x, f, out_dtype):
    p, n2x, c = x.shape
    kern = functools.partial(_cmul_kernel, n=n2x // 2)
    return pl.pallas_call(
        kern,
        grid=(p,),
        in_specs=[pl.BlockSpec((1, n2x, c), lambda pi: (pi, 0, 0)),
                  pl.BlockSpec((1, n2x, c), lambda pi: (0, 0, 0))],
        out_specs=pl.BlockSpec((1, n2x, c), lambda pi: (pi, 0, 0)),
        out_shape=jax.ShapeDtypeStruct(x.shape, out_dtype),
        compiler_params=_params("parallel"),
        name="spectrum_product",
    )(x, f)


def _phase_tables(rows_n1, n1_total, n2_total):
    n = n1_total * n2_total
    n2 = jnp.arange(n2_total, dtype=jnp.int32)[:, None, None]
    k1 = jnp.arange(n1_total, dtype=jnp.int32)[None, :, None]
    n1 = jnp.arange(rows_n1, dtype=jnp.int32)[None, None, :]
    ph = ((n2_total * n1 + n2) * k1) % n
    ang = ph.astype(F32) * (2.0 * math.pi / n)
    return jnp.cos(ang), jnp.sin(ang)


def _forward_tables(length, n1_total, n2_total):
    rows = length // n2_total
    cs, sn = _phase_tables(rows, n1_total, n2_total)
    paired = jnp.concatenate([jnp.concatenate([cs, sn], axis=2),
                              jnp.concatenate([-sn, cs], axis=2)], axis=1)
    cs, sn = _phase_tables(n1_total, n1_total, n2_total)
    real = jnp.concatenate([cs, -sn], axis=1)
    return paired.astype(BF16), real.astype(BF16)


def _inverse_table(length, n1_total, n2_total):
    rows = length // n2_total
    cs, sn = _phase_tables(rows, n1_total, n2_total)
    cs = jnp.swapaxes(cs, 1, 2) / (n1_total * n2_total)
    sn = jnp.swapaxes(sn, 1, 2) / (n1_total * n2_total)
    tb = jnp.concatenate([jnp.concatenate([cs, -sn], axis=2),
                          jnp.concatenate([sn, cs], axis=2)], axis=1)
    return tb.astype(BF16)


def _inner_matrices(n2_total):
    idx = np.arange(n2_total)
    ang = 2.0 * np.pi * ((idx[:, None] * idx[None, :]) % n2_total) / n2_total
    cs, sn = np.cos(ang), np.sin(ang)
    fwd = np.block([[cs, sn], [-sn, cs]])
    inv = np.block([[cs, -sn], [sn, cs]])
    return jnp.asarray(fwd, dtype=BF16), jnp.asarray(inv, dtype=BF16)


def _long_conv_gate(vv, x0, filt, bias):
    p, _, length, c = vv.shape
    n = 2 * length
    n1_total = DFT_N1 if n % DFT_N1 == 0 and n // DFT_N1 >= 8 else n
    n2_total = n // n1_total
    rows = length // n2_total
    t_pair, t_real = _forward_tables(length, n1_total, n2_total)
    t_inv = _inverse_table(length, n1_total, n2_total)

    bias2 = bias.reshape(1, c)
    if n2_total > 1:
        m_fwd, m_inv = _inner_matrices(n2_total)
        shape5 = lambda a: a.reshape(a.shape[0], 2, n1_total, n2_total, c)
        vv4 = vv.reshape(p, 2 * rows, n2_total, c)
        x04 = x0.reshape(p, 2 * rows, n2_total, c)
        spec = _dft_outer_strided(t_real, filt.reshape(1, n1_total, n2_total, c), BF16, cb=c)
        spec = _dft_inner(m_fwd, shape5(spec), F32)
        y = _dft_outer_strided(t_pair, vv4, BF16, cb=c)
        y = _dft_inner(m_fwd, shape5(y), BF16, m_inv=m_inv, spectrum=spec)
        y = y.reshape(p, 2 * n1_total, n2_total, c)
        out = _dft_outer_strided(t_inv, y, BF16, cb=c // 2, hyena_args=(vv4, x04, bias2))
    else:
        vv2 = vv.reshape(p, 2 * rows, c)
        x02 = x0.reshape(p, 2 * rows, c)
        spec = _dft_outer(t_real, filt.reshape(1, n1_total, c), F32, c=c)
        y = _dft_outer(t_pair, vv2, F32, c=c)
        y = _cmul(y, spec, BF16)
        out = _dft_outer(t_inv, y, BF16, c=c, hyena_args=(vv2, x02, bias2))
    return out.reshape(p, 2, length, c)


def _merge_kernel(x_ref, attn_ref, hy_ref, g_ref, gate_ref, woa_ref, woh_ref, wout_ref, o_ref, *, d):
    g = jax.nn.sigmoid(g_ref[0].astype(F32))
    merged = (g[:, :d] * _dot(attn_ref[0], woa_ref[...])
              + g[:, d:] * _dot(hy_ref[0, 0], woh_ref[...]))
    o_ref[0] = x_ref[0] + gate_ref[0] * _dot(merged.astype(BF16), wout_ref[...])


def _merge(x, attn, hyo, g, gate, woa, woh, wout, *, tm):
    b, length, d = x.shape
    p = b // 2
    c = hyo.shape[-1]
    row = lambda bi, i: (bi, i, 0)
    kern = functools.partial(_merge_kernel, d=d)
    return pl.pallas_call(
        kern,
        grid=(b, length // tm),
        in_specs=[pl.BlockSpec((1, tm, d), row),
                  pl.BlockSpec((1, tm, attn.shape[-1]), row),
                  pl.BlockSpec((1, 1, tm, c), lambda bi, i: (bi % p, bi // p, i, 0)),
                  pl.BlockSpec((1, tm, g.shape[-1]), row),
                  pl.BlockSpec((1, 1, d), lambda bi, i: (bi, 0, 0)),
                  _resident(woa.shape), _resident(woh.shape), _resident(wout.shape)],
        out_specs=pl.BlockSpec((1, tm, d), row),
        out_shape=jax.ShapeDtypeStruct(x.shape, F32),
        compiler_params=_params("parallel", "parallel"),
        name="merge",
    )(x, attn, hyo, g, gate, woa, woh, wout)


def _ffn_kernel(x_ref, shift_ref, scale_ref, gate_ref, gain_ref, wgu_ref, wd_ref, fin_ref, o_ref,
                *, d_ff, chunk, final_norm):
    x = x_ref[0]
    h = ((_rms(x) * gain_ref[...]) * (1.0 + scale_ref[0]) + shift_ref[0]).astype(BF16)
    acc = jnp.zeros(x.shape, F32)
    for c0 in range(0, d_ff, chunk):
        gt = _dot(h, wgu_ref[:, c0:c0 + chunk])
        up = _dot(h, wgu_ref[:, d_ff + c0:d_ff + c0 + chunk])
        act = (gt * jax.nn.sigmoid(gt) * up).astype(BF16)
        acc = acc + _dot(act, wd_ref[c0:c0 + chunk, :])
    y = x + gate_ref[0] * acc
    if final_norm:
        y = _rms(y) * fin_ref[...]
    o_ref[0] = y


def _ffn(x, shift, scale, gate, gain, wgu, wd, fin_gain, *, tm, final_norm):
    b, length, d = x.shape
    d_ff = wd.shape[0]
    row = lambda bi, i: (bi, i, 0)
    per_b = lambda bi, i: (bi, 0, 0)
    kern = functools.partial(_ffn_kernel, d_ff=d_ff, chunk=256, final_norm=final_norm)
    return pl.pallas_call(
        kern,
        grid=(b, length // tm),
        in_specs=[pl.BlockSpec((1, tm, d), row),
                  pl.BlockSpec((1, 1, d), per_b),
                  pl.BlockSpec((1, 1, d), per_b),
                  pl.BlockSpec((1, 1, d), per_b),
                  _resident((1, d)),
                  _resident(wgu.shape), _resident(wd.shape),
                  _resident((1, d))],
        out_specs=pl.BlockSpec((1, tm, d), row),
        out_shape=jax.ShapeDtypeStruct(x.shape, F32),
        compiler_params=_params("parallel", "parallel"),
        name="ffn",
    )(x, shift, scale, gate, gain, wgu, wd, fin_gain)


def _rope_tables(length):
    rows = length // GRID_W
    axis_dim = HEAD_DIM // 2
    row = jnp.repeat(jnp.arange(rows, dtype=F32), GRID_W)
    col = jnp.tile(jnp.arange(GRID_W, dtype=F32), rows)
    inv_freq = ROPE_THETA ** (-jnp.arange(0, axis_dim, 2, dtype=F32) / axis_dim)
    ang = jnp.concatenate([row[:, None] * inv_freq, col[:, None] * inv_freq], axis=-1)
    cs, sn = jnp.cos(ang), jnp.sin(ang)
    return jnp.concatenate([cs, cs], axis=-1), jnp.concatenate([-sn, sn], axis=-1)


def _head_perm(n_heads):
    within = np.concatenate([np.arange(0, HEAD_DIM, 2), np.arange(1, HEAD_DIM, 2)])
    return np.concatenate([h * HEAD_DIM + within for h in range(n_heads)]), within


def kernel(x, c, ctx, c_ctx, w_mod, b_mod, norm_mix, w_in, q_norm, k_norm, conv_w, conv_b, filt_w1, filt_b1, filt_w2, filt_b2, filt_w3, filt_b3, filt_w4, filt_freq, hyena_bias, w_o_attn, w_o_hyena, w_out, norm_ffn, w_gate_up, w_down, norm_final):
    b, length, d = x.shape
    ctx_len = ctx.shape[1]
    depth = w_mod.shape[0]
    d_attn = N_HEADS * HEAD_DIM
    d_kv = N_KV_HEADS * HEAD_DIM
    d_hy = conv_w.shape[-1]
    c_hy = d_hy // 3
    assert b % 2 == 0 and length % 512 == 0 and ctx_len % 256 == 0

    perm, within = _head_perm(N_HEADS + N_KV_HEADS)
    col_order = np.concatenate([perm, np.arange(d_attn + d_kv, w_in.shape[-1])])

    cosf, sinf = _rope_tables(length)
    cos_ctx = jnp.ones((ctx_len, HEAD_DIM), F32)
    sin_ctx = jnp.zeros((ctx_len, HEAD_DIM), F32)

    m_rows = 16
    c_rows = jnp.zeros((m_rows, d), F32).at[:b].set(c).at[b].set(c_ctx)
    tm = 512
    tm_ctx = ctx_len
    fin = norm_final.reshape(1, d)

    for layer in range(depth):
        last = layer == depth - 1
        mod_all = _modulation(c_rows, w_mod[layer], b_mod[layer])
        mods = [m.reshape(b, 1, d) for m in jnp.split(mod_all[:b], N_MOD, axis=-1)]
        mods_c = [jnp.broadcast_to(m.reshape(1, 1, d), (b, 1, d))
                  for m in jnp.split(mod_all[b], N_MOD, axis=-1)]
        shift1, scale1, gate1, shift2, scale2, gate2 = mods
        c_shift1, c_scale1, c_gate1, c_shift2, c_scale2, c_gate2 = mods_c

        w_in_bf = w_in[layer][:, col_order].astype(BF16)
        qg = (q_norm[layer][within] * (HEAD_DIM ** -0.5 * math.log2(math.e))).reshape(1, HEAD_DIM)
        kg = k_norm[layer][within].reshape(1, HEAD_DIM)
        gain_mix = norm_mix[layer].reshape(1, d)
        gain_ffn = norm_ffn[layer].reshape(1, d)
        woa = w_o_attn[layer].astype(BF16)
        woh = w_o_hyena[layer].astype(BF16)
        wout = w_out[layer].astype(BF16)
        wgu = w_gate_up[layer].astype(BF16)
        wd = w_down[layer].astype(BF16)
        filt = (filt_w1[layer], filt_b1[layer], filt_w2[layer], filt_b2[layer],
                filt_w3[layer], filt_b3[layer], filt_w4[layer], filt_freq[layer])
        proj = functools.partial(_in_projection, d_attn=d_attn, d_kv=d_kv, d_hy=d_hy)

        qc, kc, vc, hyc, gc = proj(ctx, c_shift1, c_scale1, gain_mix, w_in_bf, qg, kg,
                                   cos_ctx, sin_ctx, tm=tm_ctx)
        q, k, v, hy, g = proj(x, shift1, scale1, gain_mix, w_in_bf, qg, kg, cosf, sinf, tm=tm)

        attn = _attention(q, k, v, kc, vc,
                          tq=256, tk=768 if (ctx_len + length) % 768 == 0 else 256)
        vv, x0 = _short_conv_gate(hy, conv_w[layer], conv_b[layer], tl=512)
        hyo = _long_conv_gate(vv, x0, _hyena_filter(length, *filt, c_hy), hyena_bias[layer])
        x_mid = _merge(x, attn, hyo, g, gate1, woa, woh, wout, tm=tm)

        if not last:
            attn_c = _attention(qc, kc, vc, tq=ctx_len, tk=ctx_len)
            vv_c, x0_c = _short_conv_gate(hyc, conv_w[layer], conv_b[layer], tl=ctx_len)
            hyo_c = _long_conv_gate(vv_c, x0_c, _hyena_filter(ctx_len, *filt, c_hy),
                                    hyena_bias[layer])
            ctx = _merge(ctx, attn_c, hyo_c, gc, c_gate1, woa, woh, wout, tm=tm_ctx)
            ctx = _ffn(ctx, c_shift2, c_scale2, c_gate2, gain_ffn, wgu, wd, fin,
                       tm=tm_ctx, final_norm=False)

        x = _ffn(x_mid, shift2, scale2, gate2, gain_ffn, wgu, wd, fin, tm=tm, final_norm=last)
    return x
```

```python
import functools
import math

import numpy as np
import jax
import jax.numpy as jnp
from jax import lax
from jax.experimental import pallas as pl
from jax.experimental.pallas import tpu as pltpu

F32 = jnp.float32
BF16 = jnp.bfloat16

GRID_W = 64
N_HEADS = 8
N_KV_HEADS = 2
KV_GROUP = N_HEADS // N_KV_HEADS
HEAD_DIM = 128
ROPE_THETA = 10000.0
FILTER_EMB = 33
FILTER_BANDS = (FILTER_EMB - 1) // 2
DECAY_TARGET = 1e-2
FAST_DECAY_PCT = 0.3
SLOW_DECAY_PCT = 1.5
N_MOD = 6
EPS = 1e-6

LANES = 128
SUBLANES = 8
VMEM_LIMIT = 48 * 1024 * 1024
DFT_N1 = 128


def _params(*sem):
    return pltpu.CompilerParams(dimension_semantics=sem, vmem_limit_bytes=VMEM_LIMIT)


def _resident(shape):
    nd = len(shape)
    return pl.BlockSpec(shape, lambda *_: (0,) * nd, pipeline_mode=pl.Buffered(1))


def _dot(a, b):
    return jnp.dot(a, b, preferred_element_type=F32)


def _split_bf16(a):
    hi = a.astype(BF16)
    lo = (a - hi.astype(F32)).astype(BF16)
    return hi, lo


def _dot3(a, b):
    ah, al = _split_bf16(a)
    bh, bl = _split_bf16(b)
    return _dot(ah, bh) + (_dot(ah, bl) + _dot(al, bh))


def _rms(x):
    return x * lax.rsqrt(jnp.mean(x * x, axis=-1, keepdims=True) + EPS)


def _mod_kernel(c_ref, w_ref, b_ref, o_ref):
    c = c_ref[...]
    s = c * jax.nn.sigmoid(c)
    o_ref[...] = _dot3(s, w_ref[...]) + b_ref[...]


def _modulation(c_rows, w, b):
    m, d = c_rows.shape
    n = w.shape[1]
    tn = 1536
    return pl.pallas_call(
        _mod_kernel,
        grid=(n // tn,),
        in_specs=[pl.BlockSpec((m, d), lambda j: (0, 0)),
                  pl.BlockSpec((d, tn), lambda j: (0, j)),
                  pl.BlockSpec((1, tn), lambda j: (0, j))],
        out_specs=pl.BlockSpec((m, tn), lambda j: (0, j)),
        out_shape=jax.ShapeDtypeStruct((m, n), F32),
        compiler_params=_params("parallel"),
        name="modulation",
    )(c_rows, w, b.reshape(1, n))


def _inproj_kernel(x_ref, shift_ref, scale_ref, gain_ref, w_ref, qg_ref, kg_ref, cos_ref, sin_ref,
                   q_ref, k_ref, v_ref, hy_ref, g_ref, *, d_attn, d_kv, d_hy):
    x = x_ref[0]
    h = (_rms(x) * gain_ref[...]) * (1.0 + scale_ref[0]) + shift_ref[0]
    hb = h.astype(BF16)
    cosf = cos_ref[...]
    sinf = sin_ref[...]

    def head_norm_rope(t, gain):
        t = _rms(t) * gain
        return t * cosf + pltpu.roll(t, HEAD_DIM // 2, axis=1) * sinf

    c0 = 0
    q = _dot(hb, w_ref[:, c0:c0 + d_attn])
    for hd in range(d_attn // HEAD_DIM):
        sl = slice(hd * HEAD_DIM, (hd + 1) * HEAD_DIM)
        q_ref[0, :, sl] = head_norm_rope(q[:, sl], qg_ref[...]).astype(q_ref.dtype)
    c0 += d_attn
    k = _dot(hb, w_ref[:, c0:c0 + d_kv])
    for hd in range(d_kv // HEAD_DIM):
        sl = slice(hd * HEAD_DIM, (hd + 1) * HEAD_DIM)
        k_ref[0, :, sl] = head_norm_rope(k[:, sl], kg_ref[...]).astype(k_ref.dtype)
    c0 += d_kv
    v_ref[0] = _dot(hb, w_ref[:, c0:c0 + d_kv]).astype(v_ref.dtype)
    c0 += d_kv
    hy_ref[0] = _dot(hb, w_ref[:, c0:c0 + d_hy]).astype(hy_ref.dtype)
    c0 += d_hy
    g_ref[0] = _dot(hb, w_ref[:, c0:]).astype(g_ref.dtype)


def _in_projection(x, shift, scale, gain, w_bf, qg, kg, cosf, sinf, *, tm, d_attn, d_kv, d_hy):
    b, length, d = x.shape
    d_gate = w_bf.shape[1] - d_attn - 2 * d_kv - d_hy
    row = lambda bi, i: (bi, i, 0)
    per_b = lambda bi, i: (bi, 0, 0)
    kern = functools.partial(_inproj_kernel, d_attn=d_attn, d_kv=d_kv, d_hy=d_hy)
    return pl.pallas_call(
        kern,
        grid=(b, length // tm),
        in_specs=[pl.BlockSpec((1, tm, d), row),
                  pl.BlockSpec((1, 1, d), per_b),
                  pl.BlockSpec((1, 1, d), per_b),
                  _resident((1, d)),
                  _resident(w_bf.shape),
                  _resident((1, HEAD_DIM)),
                  _resident((1, HEAD_DIM)),
                  pl.BlockSpec((tm, HEAD_DIM), lambda bi, i: (i, 0)),
                  pl.BlockSpec((tm, HEAD_DIM), lambda bi, i: (i, 0))],
        out_specs=[pl.BlockSpec((1, tm, d_attn), row),
                   pl.BlockSpec((1, tm, d_kv), row),
                   pl.BlockSpec((1, tm, d_kv), row),
                   pl.BlockSpec((1, tm, d_hy), row),
                   pl.BlockSpec((1, tm, d_gate), row)],
        out_shape=[jax.ShapeDtypeStruct((b, length, d_attn), BF16),
                   jax.ShapeDtypeStruct((b, length, d_kv), BF16),
                   jax.ShapeDtypeStruct((b, length, d_kv), BF16),
                   jax.ShapeDtypeStruct((b, length, d_hy), BF16),
                   jax.ShapeDtypeStruct((b, length, d_gate), BF16)],
        compiler_params=_params("parallel", "parallel"),
        name="in_projection",
    )(x, shift, scale, gain, w_bf, qg, kg, cosf, sinf)


def _flash_kernel(*refs, tq, tk, nk, extra):
    if extra:
        (q_ref, qn_ref, k_ref, v_ref, ke_ref, ve_ref, o_ref,
         kall_ref, vall_ref, qs_ref, s_ref, m_ref, acc_ref) = refs
    else:
        q_ref, qn_ref, k_ref, v_ref, o_ref, kall_ref, vall_ref, qs_ref, s_ref, m_ref, acc_ref = refs
    first_slot = 2
    slot_of = lambda j: first_slot if j == 0 else (j - 1) % 2

    def stack_heads(src_ref, which):
        for g in range(KV_GROUP):
            qs_ref[which, g * tq:(g + 1) * tq, :] = src_ref[0, :, g * HEAD_DIM:(g + 1) * HEAD_DIM]

    def chunk(ref, j):
        return ref[j * tk:(j + 1) * tk, :]

    def scores(j, slot, which=0):
        s_ref[slot] = lax.dot_general(qs_ref[which], chunk(kall_ref, j), (((1,), (1,)), ((), ())),
                                      preferred_element_type=F32)

    stack_heads(q_ref, 0)
    stack_heads(qn_ref, 1)

    @pl.when(pl.program_id(2) == 0)
    def _():
        n_main = k_ref.shape[1]
        kall_ref[0:n_main, :] = k_ref[0]
        vall_ref[0:n_main, 0:HEAD_DIM] = v_ref[0]
        if extra:
            kall_ref[n_main:, :] = ke_ref[0]
            vall_ref[n_main:, 0:HEAD_DIM] = ve_ref[0]
        vall_ref[:, HEAD_DIM:] = jnp.ones((vall_ref.shape[0], HEAD_DIM), vall_ref.dtype)
        scores(0, first_slot)

    m_ref[...] = jnp.full(m_ref.shape, -1e30, F32)
    acc_ref[...] = jnp.zeros(acc_ref.shape, F32)
    nt = tk // LANES

    def consume(j, slot):
        tiles = [s_ref[slot, :, t * LANES:(t + 1) * LANES] for t in range(nt)]
        mt = functools.reduce(jnp.maximum, tiles)
        m_prev = m_ref[...]
        m_new = jnp.maximum(m_prev, jnp.max(mt, axis=-1, keepdims=True))
        alpha = jnp.exp2(m_prev - m_new)
        p = jnp.concatenate([jnp.exp2(t - m_new).astype(BF16) for t in tiles], axis=1)
        pv = _dot(p, chunk(vall_ref, j))
        acc_ref[...] = jnp.concatenate([alpha, alpha], axis=1) * acc_ref[...] + pv
        m_ref[...] = m_new

    for j in range(nk - 1):
        scores(j + 1, slot_of(j + 1))
        consume(j, slot_of(j))
    if nk == 1:
        consume(0, first_slot)
        scores(0, first_slot, which=1)
    else:
        scores(0, first_slot, which=1)
        consume(nk - 1, slot_of(nk - 1))

    acc = acc_ref[...]
    out = acc[:, :HEAD_DIM] / acc[:, HEAD_DIM:]
    for g in range(KV_GROUP):
        o_ref[0, :, g * HEAD_DIM:(g + 1) * HEAD_DIM] = out[g * tq:(g + 1) * tq].astype(o_ref.dtype)


def _attention(q, k, v, k_extra=None, v_extra=None, *, tq, tk):
    b, lq, _ = q.shape
    extra = k_extra is not None
    lk = k.shape[1] + (k_extra.shape[1] if extra else 0)
    gw = KV_GROUP * HEAD_DIM
    rows = KV_GROUP * tq
    assert lk % tk == 0
    nq = lq // tq
    kv_spec = lambda a: pl.BlockSpec((1, a.shape[1], HEAD_DIM), lambda bi, h, i: (bi, 0, h))
    kv_args = [k, v] + ([k_extra, v_extra] if extra else [])
    kern = functools.partial(_flash_kernel, tq=tq, tk=tk, nk=lk // tk, extra=extra)
    return pl.pallas_call(
        kern,
        grid=(b, N_KV_HEADS, nq),
        in_specs=[pl.BlockSpec((1, tq, gw), lambda bi, h, i: (bi, i, h)),
                  pl.BlockSpec((1, tq, gw), lambda bi, h, i: (bi, jnp.minimum(i + 1, nq - 1), h))]
                 + [kv_spec(a) for a in kv_args],
        out_specs=pl.BlockSpec((1, tq, gw), lambda bi, h, i: (bi, i, h)),
        out_shape=jax.ShapeDtypeStruct(q.shape, BF16),
        scratch_shapes=[pltpu.VMEM((lk, HEAD_DIM), BF16),
                        pltpu.VMEM((lk, 2 * HEAD_DIM), BF16),
                        pltpu.VMEM((2, rows, HEAD_DIM), BF16),
                        pltpu.VMEM((3, rows, tk), F32),
                        pltpu.VMEM((rows, LANES), F32),
                        pltpu.VMEM((rows, 2 * HEAD_DIM), F32)],
        compiler_params=_params("arbitrary", "arbitrary", "arbitrary"),
        name="attention",
    )(q, q, *kv_args)


def _shortconv_kernel(u_ref, prev_ref, next_ref, w_ref, b_ref, vv_ref, x0_ref, *, c):
    i = pl.program_id(1)
    u = u_ref[0].astype(F32)
    tl = u.shape[0]
    row = lax.broadcasted_iota(jnp.int32, (tl, 1), 0)
    halo = prev_ref.shape[1]
    prev_row = jnp.where(i == 0, 0.0, prev_ref[0, halo - 1:halo, :].astype(F32))
    next_row = jnp.where(i == pl.num_programs(1) - 1, 0.0, next_ref[0, 0:1, :].astype(F32))
    up = jnp.where(row == 0, prev_row, pltpu.roll(u, 1, axis=0))
    un = jnp.where(row == tl - 1, next_row, pltpu.roll(u, tl - 1, axis=0))
    y = up * w_ref[0:1, :] + u * w_ref[1:2, :] + un * w_ref[2:3, :] + b_ref[...]
    x0_ref[0, 0] = y[:, :c]
    vv_ref[0, 0] = y[:, 2 * c:] * y[:, c:2 * c]


def _short_conv_gate(u, conv_w, conv_b, *, tl):
    b, length, c3 = u.shape
    c = c3 // 3
    p = b // 2
    halo = 2 * SUBLANES if u.dtype == BF16 else SUBLANES
    n_halo = length // halo
    per = tl // halo
    out_map = lambda bi, i: (bi % p, bi // p, i, 0)
    kern = functools.partial(_shortconv_kernel, c=c)
    return pl.pallas_call(
        kern,
        grid=(b, length // tl),
        in_specs=[pl.BlockSpec((1, tl, c3), lambda bi, i: (bi, i, 0)),
                  pl.BlockSpec((1, halo, c3), lambda bi, i: (bi, jnp.maximum(i * per - 1, 0), 0)),
                  pl.BlockSpec((1, halo, c3),
                               lambda bi, i: (bi, jnp.minimum((i + 1) * per, n_halo - 1), 0)),
                  _resident((3, c3)),
                  _resident((1, c3))],
        out_specs=[pl.BlockSpec((1, 1, tl, c), out_map),
                   pl.BlockSpec((1, 1, tl, c), out_map)],
        out_shape=[jax.ShapeDtypeStruct((p, 2, length, c), F32),
                   jax.ShapeDtypeStruct((p, 2, length, c), F32)],
        compiler_params=_params("parallel", "parallel"),
        name="short_conv_gate",
    )(u, u, u, conv_w, conv_b.reshape(1, c3))


def _filter_kernel(z_ref, w1_ref, b1_ref, w2_ref, b2_ref, w3_ref, b3_ref, w4_ref, fr_ref, dec_ref,
                   o_ref):
    h = jnp.sin(fr_ref[0:1, :] * (_dot3(z_ref[...], w1_ref[...]) + b1_ref[...]))
    h = jnp.sin(fr_ref[1:2, :] * (_dot3(h, w2_ref[...]) + b2_ref[...]))
    h = jnp.sin(fr_ref[2:3, :] * (_dot3(h, w3_ref[...]) + b3_ref[...]))
    o_ref[...] = _dot3(h, w4_ref[0]) * dec_ref[...]


def _pad_to(a, shape):
    return jnp.pad(a, [(0, s - d) for d, s in zip(a.shape, shape)])


def _hyena_filter(length, fw1, fb1, fw2, fb2, fw3, fb3, fw4, freq, c):
    mirror = lambda a: jnp.concatenate([a, a[:1], a[:0:-1]])[:, None]
    live = (jnp.arange(2 * length) != length).astype(F32)[:, None]
    t = mirror(jnp.linspace(0.0, 1.0, length, dtype=F32))
    w = mirror((2.0 * math.pi / length) * jnp.arange(length, dtype=F32))
    f = jnp.linspace(1e-4, FILTER_BANDS - 1, FILTER_BANDS, dtype=F32)[None, :]
    z = jnp.concatenate([t, jnp.cos(f * w), -jnp.sin(f * w)], axis=-1)
    max_decay = math.log(DECAY_TARGET) / FAST_DECAY_PCT
    min_decay = math.log(DECAY_TARGET) / SLOW_DECAY_PCT
    deltas = jnp.abs(jnp.linspace(min_decay, max_decay, c, dtype=F32))
    decay = jnp.exp(-t * deltas) * live

    hid = LANES
    z = _pad_to(z, (2 * length, hid))
    w1 = _pad_to(fw1, (hid, hid))
    w2 = _pad_to(fw2, (hid, hid))
    w3 = _pad_to(fw3, (hid, hid))
    w4 = jnp.stack([_pad_to(fw4[:, :c], (hid, c)), _pad_to(fw4[:, c:], (hid, c))])
    b1 = _pad_to(fb1.reshape(1, -1), (1, hid))
    b2 = _pad_to(fb2.reshape(1, -1), (1, hid))
    b3 = _pad_to(fb3.reshape(1, -1), (1, hid))
    fr = _pad_to(freq, (3, hid))
    tl = min(length, 1024)
    per_half = length // tl
    return pl.pallas_call(
        _filter_kernel,
        grid=(2 * per_half,),
        in_specs=[pl.BlockSpec((tl, hid), lambda i: (i, 0)),
                  _resident((hid, hid)), _resident((1, hid)),
                  _resident((hid, hid)), _resident((1, hid)),
                  _resident((hid, hid)), _resident((1, hid)),
                  pl.BlockSpec((1, hid, c), lambda i: (i // per_half, 0, 0)),
                  _resident((3, hid)),
                  pl.BlockSpec((tl, c), lambda i: (i, 0))],
        out_specs=pl.BlockSpec((tl, c), lambda i: (i, 0)),
        out_shape=jax.ShapeDtypeStruct((2 * length, c), F32),
        compiler_params=_params("parallel"),
        name="hyena_filter",
    )(z, w1, b1, w2, b2, w3, b3, w4, fr, decay)


def _dft_outer_kernel(*refs, nb, c, per_group, hyena):
    if hyena:
        t_ref, z_ref, vv_ref, x0_ref, bias_ref, o_ref = refs
    else:
        t_ref, z_ref, o_ref = refs
    for j in range(nb):
        cols = slice(j * c, (j + 1) * c)
        y = _dot(t_ref[j if per_group else 0], z_ref[0, :, cols].astype(BF16))
        if hyena:
            y = (y + vv_ref[0, :, cols] * bias_ref[...]) * x0_ref[0, :, cols]
        o_ref[0, :, cols] = y.astype(o_ref.dtype)


def _dft_outer(table, z, out_dtype, *, c, hyena_args=None):
    p, k, ncols = z.shape
    nt, m, _ = table.shape
    groups = ncols // c
    nb = min(groups, 8)
    per_group = nt > 1
    t_spec = (pl.BlockSpec((nb, m, k), lambda g, pi: (g, 0, 0)) if per_group
              else _resident((1, m, k)))
    col_spec = lambda rows: pl.BlockSpec((1, rows, nb * c), lambda g, pi: (pi, 0, g))
    in_specs = [t_spec, col_spec(k)]
    args = [table, z]
    if hyena_args is not None:
        vv, x0, bias = hyena_args
        in_specs += [col_spec(m), col_spec(m), _resident((1, c))]
        args += [vv, x0, bias]
    kern = functools.partial(_dft_outer_kernel, nb=nb, c=c, per_group=per_group,
                             hyena=hyena_args is not None)
    return pl.pallas_call(
        kern,
        grid=(groups // nb, p),
        in_specs=in_specs,
        out_specs=col_spec(m),
        out_shape=jax.ShapeDtypeStruct((p, m, ncols), out_dtype),
        compiler_params=_params("parallel", "parallel"),
        name="dft_outer",
    )(*args)


def _dft_strided_kernel(*refs, nb, hyena):
    if hyena:
        t_ref, z_ref, vv_ref, x0_ref, bias_ref, o_ref, zs_ref, os_ref, vs_ref, xs_ref = refs
    else:
        t_ref, z_ref, o_ref, zs_ref, os_ref = refs
    k, cb = z_ref.shape[1], z_ref.shape[3]
    m = o_ref.shape[1]
    nl = cb // LANES
    ng = nb // SUBLANES
    lane = lambda t: slice(t * LANES, (t + 1) * LANES)

    def stage(dst_ref, val):
        for t in range(nl):
            for g in range(ng):
                piece = val[:, g * SUBLANES:(g + 1) * SUBLANES, lane(t)]
                dst_ref[t, g] = piece.reshape(val.shape[0] * SUBLANES, LANES)

    def rows(src_ref, j, n):
        g, r = divmod(j, SUBLANES)
        return jnp.concatenate([src_ref[t, g, pl.ds(r, n, stride=SUBLANES), :]
                                for t in range(nl)], axis=1)

    stage(zs_ref, z_ref[0].astype(F32))
    if hyena:
        stage(vs_ref, vv_ref[0])
        stage(xs_ref, x0_ref[0])
    for j in range(nb):
        y = _dot(t_ref[j], rows(zs_ref, j, k).astype(BF16))
        if hyena:
            y = (y + rows(vs_ref, j, m) * bias_ref[...]) * rows(xs_ref, j, m)
        g, r = divmod(j, SUBLANES)
        for t in range(nl):
            os_ref[t, g, pl.ds(r, m, stride=SUBLANES), :] = y[:, lane(t)]
    o_ref[0] = jnp.concatenate(
        [jnp.concatenate([os_ref[t, g].reshape(m, SUBLANES, LANES) for g in range(ng)], axis=1)
         for t in range(nl)], axis=2).astype(o_ref.dtype)


def _dft_outer_strided(table, z, out_dtype, *, cb, hyena_args=None):
    p, k, n2, c = z.shape
    _, m, _ = table.shape
    nb = 16
    blk = lambda rows: pl.BlockSpec((1, rows, nb, cb), lambda g, ci, pi: (pi, 0, g, ci))
    in_specs = [pl.BlockSpec((nb, m, k), lambda g, ci, pi: (g, 0, 0)), blk(k)]
    args = [table, z]
    staging = lambda rows: pltpu.VMEM((cb // LANES, nb // SUBLANES, rows * SUBLANES, LANES), F32)
    scratch = [staging(k), staging(m)]
    if hyena_args is not None:
        vv, x0, bias = hyena_args
        in_specs += [blk(m), blk(m), pl.BlockSpec((1, cb), lambda g, ci, pi: (0, ci))]
        args += [vv, x0, bias]
        scratch += [staging(m), staging(m)]
    kern = functools.partial(_dft_strided_kernel, nb=nb, hyena=hyena_args is not None)
    return pl.pallas_call(
        kern,
        grid=(n2 // nb, c // cb, p),
        in_specs=in_specs,
        out_specs=blk(m),
        out_shape=jax.ShapeDtypeStruct((p, m, n2, c), out_dtype),
        scratch_shapes=scratch,
        compiler_params=_params("parallel", "parallel", "parallel"),
        name="dft_outer_strided",
    )(*args)


def _dft_inner_kernel(*refs, kc, n2, filtered):
    if filtered:
        mf_ref, mi_ref, x_ref, f_ref, o_ref = refs
    else:
        mf_ref, x_ref, o_ref = refs
    for kk in range(kc):
        zin = jnp.concatenate([x_ref[0, 0, kk], x_ref[0, 1, kk]], axis=0)
        y = _dot(mf_ref[...], zin)
        re, im = y[:n2], y[n2:]
        if filtered:
            fr, fi = f_ref[0, 0, kk], f_ref[0, 1, kk]
            prod = jnp.concatenate([re * fr - im * fi, re * fi + im * fr], axis=0)
            y = _dot(mi_ref[...], prod.astype(BF16))
            re, im = y[:n2], y[n2:]
        o_ref[0, 0, kk] = re.astype(o_ref.dtype)
        o_ref[0, 1, kk] = im.astype(o_ref.dtype)


def _dft_inner(m_fwd, x, out_dtype, m_inv=None, spectrum=None):
    p, _, n1, n2, c = x.shape
    kc = 8
    blk = lambda g, pi: (pi, 0, g, 0, 0)
    filtered = spectrum is not None
    in_specs = [_resident(m_fwd.shape)]
    args = [m_fwd]
    if filtered:
        in_specs.append(_resident(m_inv.shape))
        args.append(m_inv)
    in_specs.append(pl.BlockSpec((1, 2, kc, n2, c), blk))
    args.append(x)
    if filtered:
        in_specs.append(pl.BlockSpec((1, 2, kc, n2, c), lambda g, pi: (0, 0, g, 0, 0)))
        args.append(spectrum)
    kern = functools.partial(_dft_inner_kernel, kc=kc, n2=n2, filtered=filtered)
    return pl.pallas_call(
        kern,
        grid=(n1 // kc, p),
        in_specs=in_specs,
        out_specs=pl.BlockSpec((1, 2, kc, n2, c), blk),
        out_shape=jax.ShapeDtypeStruct(x.shape, out_dtype),
        compiler_params=_params("parallel", "parallel"),
        name="dft_inner",
    )(*args)


def _cmul_kernel(x_ref, f_ref, o_ref, *, n):
    xr, xi = x_ref[0, :n], x_ref[0, n:]
    fr, fi = f_ref[0, :n], f_ref[0, n:]
    o_ref[0, :n] = (xr * fr - xi * fi).astype(o_ref.dtype)
    o_ref[0, n:] = (xr * fi + xi * fr).astype(o_ref.dtype)


def _cmul(x, f, out_dtype):
    p, n2x, c = x.shape
    kern = functools.partial(_cmul_kernel, n=n2x // 2)
    return pl.pallas_call(
        kern,
        grid=(p,),
        in_specs=[pl.BlockSpec((1, n2x, c), lambda pi: (pi, 0, 0)),
                  pl.BlockSpec((1, n2x, c), lambda pi: (0, 0, 0))],
        out_specs=pl.BlockSpec((1, n2x, c), lambda pi: (pi, 0, 0)),
        out_shape=jax.ShapeDtypeStruct(x.shape, out_dtype),
        compiler_params=_params("parallel"),
        name="spectrum_product",
    )(x, f)


def _phase_tables(rows_n1, n1_total, n2_total):
    n = n1_total * n2_total
    n2 = jnp.arange(n2_total, dtype=jnp.int32)[:, None, None]
    k1 = jnp.arange(n1_total, dtype=jnp.int32)[None, :, None]
    n1 = jnp.arange(rows_n1, dtype=jnp.int32)[None, None, :]
    ph = ((n2_total * n1 + n2) * k1) % n
    ang = ph.astype(F32) * (2.0 * math.pi / n)
    return jnp.cos(ang), jnp.sin(ang)


def _forward_tables(length, n1_total, n2_total):
    rows = length // n2_total
    cs, sn = _phase_tables(rows, n1_total, n2_total)
    paired = jnp.concatenate([jnp.concatenate([cs, sn], axis=2),
                              jnp.concatenate([-sn, cs], axis=2)], axis=1)
    cs, sn = _phase_tables(n1_total, n1_total, n2_total)
    real = jnp.concatenate([cs, -sn], axis=1)
    return paired.astype(BF16), real.astype(BF16)


def _inverse_table(length, n1_total, n2_total):
    rows = length // n2_total
    cs, sn = _phase_tables(rows, n1_total, n2_total)
    cs = jnp.swapaxes(cs, 1, 2) / (n1_total * n2_total)
    sn = jnp.swapaxes(sn, 1, 2) / (n1_total * n2_total)
    tb = jnp.concatenate([jnp.concatenate([cs, -sn], axis=2),
                          jnp.concatenate([sn, cs], axis=2)], axis=1)
    return tb.astype(BF16)


def _inner_matrices(n2_total):
    idx = np.arange(n2_total)
    ang = 2.0 * np.pi * ((idx[:, None] * idx[None, :]) % n2_total) / n2_total
    cs, sn = np.cos(ang), np.sin(ang)
    fwd = np.block([[cs, sn], [-sn, cs]])
    inv = np.block([[cs, -sn], [sn, cs]])
    return jnp.asarray(fwd, dtype=BF16), jnp.asarray(inv, dtype=BF16)


def _long_conv_gate(vv, x0, filt, bias):
    p, _, length, c = vv.shape
    n = 2 * length
    n1_total = DFT_N1 if n % DFT_N1 == 0 and n // DFT_N1 >= 8 else n
    n2_total = n // n1_total
    rows = length // n2_total
    t_pair, t_real = _forward_tables(length, n1_total, n2_total)
    t_inv = _inverse_table(length, n1_total, n2_total)

    bias2 = bias.reshape(1, c)
    if n2_total > 1:
        m_fwd, m_inv = _inner_matrices(n2_total)
        shape5 = lambda a: a.reshape(a.shape[0], 2, n1_total, n2_total, c)
        vv4 = vv.reshape(p, 2 * rows, n2_total, c)
        x04 = x0.reshape(p, 2 * rows, n2_total, c)
        spec = _dft_outer_strided(t_real, filt.reshape(1, n1_total, n2_total, c), BF16, cb=c)
        spec = _dft_inner(m_fwd, shape5(spec), F32)
        y = _dft_outer_strided(t_pair, vv4, BF16, cb=c)
        y = _dft_inner(m_fwd, shape5(y), BF16, m_inv=m_inv, spectrum=spec)
        y = y.reshape(p, 2 * n1_total, n2_total, c)
        out = _dft_outer_strided(t_inv, y, BF16, cb=c // 2, hyena_args=(vv4, x04, bias2))
    else:
        vv2 = vv.reshape(p, 2 * rows, c)
        x02 = x0.reshape(p, 2 * rows, c)
        spec = _dft_outer(t_real, filt.reshape(1, n1_total, c), F32, c=c)
        y = _dft_outer(t_pair, vv2, F32, c=c)
        y = _cmul(y, spec, BF16)
        out = _dft_outer(t_inv, y, BF16, c=c, hyena_args=(vv2, x02, bias2))
    return out.reshape(p, 2, length, c)


def _merge_kernel(x_ref, attn_ref, hy_ref, g_ref, gate_ref, woa_ref, woh_ref, wout_ref, o_ref, *, d):
    g = jax.nn.sigmoid(g_ref[0].astype(F32))
    merged = (g[:, :d] * _dot(attn_ref[0], woa_ref[...])
              + g[:, d:] * _dot(hy_ref[0, 0], woh_ref[...]))
    o_ref[0] = x_ref[0] + gate_ref[0] * _dot(merged.astype(BF16), wout_ref[...])


def _merge(x, attn, hyo, g, gate, woa, woh, wout, *, tm):
    b, length, d = x.shape
    p = b // 2
    c = hyo.shape[-1]
    row = lambda bi, i: (bi, i, 0)
    kern = functools.partial(_merge_kernel, d=d)
    return pl.pallas_call(
        kern,
        grid=(b, length // tm),
        in_specs=[pl.BlockSpec((1, tm, d), row),
                  pl.BlockSpec((1, tm, attn.shape[-1]), row),
                  pl.BlockSpec((1, 1, tm, c), lambda bi, i: (bi % p, bi // p, i, 0)),
                  pl.BlockSpec((1, tm, g.shape[-1]), row),
                  pl.BlockSpec((1, 1, d), lambda bi, i: (bi, 0, 0)),
                  _resident(woa.shape), _resident(woh.shape), _resident(wout.shape)],
        out_specs=pl.BlockSpec((1, tm, d), row),
        out_shape=jax.ShapeDtypeStruct(x.shape, F32),
        compiler_params=_params("parallel", "parallel"),
        name="merge",
    )(x, attn, hyo, g, gate, woa, woh, wout)


def _ffn_kernel(x_ref, shift_ref, scale_ref, gate_ref, gain_ref, wgu_ref, wd_ref, fin_ref, o_ref,
                *, d_ff, chunk, final_norm):
    x = x_ref[0]
    h = ((_rms(x) * gain_ref[...]) * (1.0 + scale_ref[0]) + shift_ref[0]).astype(BF16)
    acc = jnp.zeros(x.shape, F32)
    for c0 in range(0, d_ff, chunk):
        gt = _dot(h, wgu_ref[:, c0:c0 + chunk])
        up = _dot(h, wgu_ref[:, d_ff + c0:d_ff + c0 + chunk])
        act = (gt * jax.nn.sigmoid(gt) * up).astype(BF16)
        acc = acc + _dot(act, wd_ref[c0:c0 + chunk, :])
    y = x + gate_ref[0] * acc
    if final_norm:
        y = _rms(y) * fin_ref[...]
    o_ref[0] = y


def _ffn(x, shift, scale, gate, gain, wgu, wd, fin_gain, *, tm, final_norm):
    b, length, d = x.shape
    d_ff = wd.shape[0]
    row = lambda bi, i: (bi, i, 0)
    per_b = lambda bi, i: (bi, 0, 0)
    kern = functools.partial(_ffn_kernel, d_ff=d_ff, chunk=256, final_norm=final_norm)
    return pl.pallas_call(
        kern,
        grid=(b, length // tm),
        in_specs=[pl.BlockSpec((1, tm, d), row),
                  pl.BlockSpec((1, 1, d), per_b),
                  pl.BlockSpec((1, 1, d), per_b),
                  pl.BlockSpec((1, 1, d), per_b),
                  _resident((1, d)),
                  _resident(wgu.shape), _resident(wd.shape),
                  _resident((1, d))],
        out_specs=pl.BlockSpec((1, tm, d), row),
        out_shape=jax.ShapeDtypeStruct(x.shape, F32),
        compiler_params=_params("parallel", "parallel"),
        name="ffn",
    )(x, shift, scale, gate, gain, wgu, wd, fin_gain)


def _rope_tables(length):
    rows = length // GRID_W
    axis_dim = HEAD_DIM // 2
    row = jnp.repeat(jnp.arange(rows, dtype=F32), GRID_W)
    col = jnp.tile(jnp.arange(GRID_W, dtype=F32), rows)
    inv_freq = ROPE_THETA ** (-jnp.arange(0, axis_dim, 2, dtype=F32) / axis_dim)
    ang = jnp.concatenate([row[:, None] * inv_freq, col[:, None] * inv_freq], axis=-1)
    cs, sn = jnp.cos(ang), jnp.sin(ang)
    return jnp.concatenate([cs, cs], axis=-1), jnp.concatenate([-sn, sn], axis=-1)


def _head_perm(n_heads):
    within = np.concatenate([np.arange(0, HEAD_DIM, 2), np.arange(1, HEAD_DIM, 2)])
    return np.concatenate([h * HEAD_DIM + within for h in range(n_heads)]), within


def kernel(x, c, ctx, c_ctx, w_mod, b_mod, norm_mix, w_in, q_norm, k_norm, conv_w, conv_b, filt_w1, filt_b1, filt_w2, filt_b2, filt_w3, filt_b3, filt_w4, filt_freq, hyena_bias, w_o_attn, w_o_hyena, w_out, norm_ffn, w_gate_up, w_down, norm_final):
    b, length, d = x.shape
    ctx_len = ctx.shape[1]
    depth = w_mod.shape[0]
    d_attn = N_HEADS * HEAD_DIM
    d_kv = N_KV_HEADS * HEAD_DIM
    d_hy = conv_w.shape[-1]
    c_hy = d_hy // 3
    assert b % 2 == 0 and length % 512 == 0 and ctx_len % 256 == 0

    perm, within = _head_perm(N_HEADS + N_KV_HEADS)
    col_order = np.concatenate([perm, np.arange(d_attn + d_kv, w_in.shape[-1])])

    cosf, sinf = _rope_tables(length)
    cos_ctx = jnp.ones((ctx_len, HEAD_DIM), F32)
    sin_ctx = jnp.zeros((ctx_len, HEAD_DIM), F32)

    m_rows = 16
    c_rows = jnp.zeros((m_rows, d), F32).at[:b].set(c).at[b].set(c_ctx)
    tm = 512
    tm_ctx = ctx_len
    fin = norm_final.reshape(1, d)

    for layer in range(depth):
        last = layer == depth - 1
        mod_all = _modulation(c_rows, w_mod[layer], b_mod[layer])
        mods = [m.reshape(b, 1, d) for m in jnp.split(mod_all[:b], N_MOD, axis=-1)]
        mods_c = [jnp.broadcast_to(m.reshape(1, 1, d), (b, 1, d))
                  for m in jnp.split(mod_all[b], N_MOD, axis=-1)]
        shift1, scale1, gate1, shift2, scale2, gate2 = mods
        c_shift1, c_scale1, c_gate1, c_shift2, c_scale2, c_gate2 = mods_c

        w_in_bf = w_in[layer][:, col_order].astype(BF16)
        qg = (q_norm[layer][within] * (HEAD_DIM ** -0.5 * math.log2(math.e))).reshape(1, HEAD_DIM)
        kg = k_norm[layer][within].reshape(1, HEAD_DIM)
        gain_mix = norm_mix[layer].reshape(1, d)
        gain_ffn = norm_ffn[layer].reshape(1, d)
        woa = w_o_attn[layer].astype(BF16)
        woh = w_o_hyena[layer].astype(BF16)
        wout = w_out[layer].astype(BF16)
        wgu = w_gate_up[layer].astype(BF16)
        wd = w_down[layer].astype(BF16)
        filt = (filt_w1[layer], filt_b1[layer], filt_w2[layer], filt_b2[layer],
                filt_w3[layer], filt_b3[layer], filt_w4[layer], filt_freq[layer])
        proj = functools.partial(_in_projection, d_attn=d_attn, d_kv=d_kv, d_hy=d_hy)

        qc, kc, vc, hyc, gc = proj(ctx, c_shift1, c_scale1, gain_mix, w_in_bf, qg, kg,
                                   cos_ctx, sin_ctx, tm=tm_ctx)
        q, k, v, hy, g = proj(x, shift1, scale1, gain_mix, w_in_bf, qg, kg, cosf, sinf, tm=tm)

        attn = _attention(q, k, v, kc, vc,
                          tq=256, tk=768 if (ctx_len + length) % 768 == 0 else 256)
        vv, x0 = _short_conv_gate(hy, conv_w[layer], conv_b[layer], tl=512)
        hyo = _long_conv_gate(vv, x0, _hyena_filter(length, *filt, c_hy), hyena_bias[layer])
        x_mid = _merge(x, attn, hyo, g, gate1, woa, woh, wout, tm=tm)

        if not last:
            attn_c = _attention(qc, kc, vc, tq=ctx_len, tk=ctx_len)
            vv_c, x0_c = _short_conv_gate(hyc, conv_w[layer], conv_b[layer], tl=ctx_len)
            hyo_c = _long_conv_gate(vv_c, x0_c, _hyena_filter(ctx_len, *filt, c_hy),
                                    hyena_bias[layer])
            ctx = _merge(ctx, attn_c, hyo_c, gc, c_gate1, woa, woh, wout, tm=tm_ctx)
            ctx = _ffn(ctx, c_shift2, c_scale2, c_gate2, gain_ffn, wgu, wd, fin,
                       tm=tm_ctx, final_norm=False)

        x = _ffn(x_mid, shift2, scale2, gate2, gain_ffn, wgu, wd, fin, tm=tm, final_norm=last)
    return x
```

```python
import functools
import math

import numpy as np
import jax
import jax.numpy as jnp
from jax import lax
from jax.experimental import pallas as pl
from jax.experimental.pallas import tpu as pltpu

F32 = jnp.float32
BF16 = jnp.bfloat16

GRID_W = 64
N_HEADS = 8
N_KV_HEADS = 2
KV_GROUP = N_HEADS // N_KV_HEADS
HEAD_DIM = 128
ROPE_THETA = 10000.0
FILTER_EMB = 33
FILTER_BANDS = (FILTER_EMB - 1) // 2
DECAY_TARGET = 1e-2
FAST_DECAY_PCT = 0.3
SLOW_DECAY_PCT = 1.5
N_MOD = 6
EPS = 1e-6

LANES = 128
SUBLANES = 8
VMEM_LIMIT = 48 * 1024 * 1024
DFT_N1 = 128


def _params(*sem):
    return pltpu.CompilerParams(dimension_semantics=sem, vmem_limit_bytes=VMEM_LIMIT)


def _resident(shape):
    nd = len(shape)
    return pl.BlockSpec(shape, lambda *_: (0,) * nd, pipeline_mode=pl.Buffered(1))


def _dot(a, b):
    return jnp.dot(a, b, preferred_element_type=F32)


def _split_bf16(a):
    hi = a.astype(BF16)
    lo = (a - hi.astype(F32)).astype(BF16)
    return hi, lo


def _dot3(a, b):
    ah, al = _split_bf16(a)
    bh, bl = _split_bf16(b)
    return _dot(ah, bh) + (_dot(ah, bl) + _dot(al, bh))


def _rms(x):
    return x * lax.rsqrt(jnp.mean(x * x, axis=-1, keepdims=True) + EPS)


def _mod_kernel(c_ref, w_ref, b_ref, o_ref):
    c = c_ref[...]
    s = c * jax.nn.sigmoid(c)
    o_ref[...] = _dot3(s, w_ref[...]) + b_ref[...]


def _modulation(c_rows, w, b):
    m, d = c_rows.shape
    n = w.shape[1]
    tn = 1536
    return pl.pallas_call(
        _mod_kernel,
        grid=(n // tn,),
        in_specs=[pl.BlockSpec((m, d), lambda j: (0, 0)),
                  pl.BlockSpec((d, tn), lambda j: (0, j)),
                  pl.BlockSpec((1, tn), lambda j: (0, j))],
        out_specs=pl.BlockSpec((m, tn), lambda j: (0, j)),
        out_shape=jax.ShapeDtypeStruct((m, n), F32),
        compiler_params=_params("parallel"),
        name="modulation",
    )(c_rows, w, b.reshape(1, n))


def _inproj_kernel(x_ref, xp_ref, xn_ref, shift_ref, scale_ref, gain_ref, w_ref, qg_ref, kg_ref,
                   cos_ref, sin_ref, cw_ref, cb_ref,
                   q_ref, k_ref, v_ref, g_ref, vv_ref, x0_ref, *, d_attn, d_kv, d_hy):
    i = pl.program_id(1)
    tm = x_ref.shape[1]
    halo = xp_ref.shape[1]
    x = jnp.concatenate([x_ref[0], xp_ref[0], xn_ref[0]], axis=0)
    h = (_rms(x) * gain_ref[...]) * (1.0 + scale_ref[0]) + shift_ref[0]
    hb_all = h.astype(BF16)
    hb = hb_all[:tm]
    cosf = cos_ref[...]
    sinf = sin_ref[...]

    def head_norm_rope(t, gain):
        t = _rms(t) * gain
        return t * cosf + pltpu.roll(t, HEAD_DIM // 2, axis=1) * sinf

    c0 = 0
    q = _dot(hb, w_ref[:, c0:c0 + d_attn])
    for hd in range(d_attn // HEAD_DIM):
        sl = slice(hd * HEAD_DIM, (hd + 1) * HEAD_DIM)
        q_ref[0, :, sl] = head_norm_rope(q[:, sl], qg_ref[...]).astype(q_ref.dtype)
    c0 += d_attn
    k = _dot(hb, w_ref[:, c0:c0 + d_kv])
    for hd in range(d_kv // HEAD_DIM):
        sl = slice(hd * HEAD_DIM, (hd + 1) * HEAD_DIM)
        k_ref[0, :, sl] = head_norm_rope(k[:, sl], kg_ref[...]).astype(k_ref.dtype)
    c0 += d_kv
    v_ref[0] = _dot(hb, w_ref[:, c0:c0 + d_kv]).astype(v_ref.dtype)
    c0 += d_kv
    u_all = _dot(hb_all, w_ref[:, c0:c0 + d_hy])
    c0 += d_hy
    g_ref[0] = _dot(hb, w_ref[:, c0:]).astype(g_ref.dtype)

    u = u_all[:tm]
    prev_row = jnp.where(i == 0, 0.0, u_all[tm + halo - 1:tm + halo])
    next_row = jnp.where(i == pl.num_programs(1) - 1, 0.0, u_all[tm + halo:tm + halo + 1])
    row = lax.broadcasted_iota(jnp.int32, (tm, 1), 0)
    up = jnp.where(row == 0, prev_row, pltpu.roll(u, 1, axis=0))
    un = jnp.where(row == tm - 1, next_row, pltpu.roll(u, tm - 1, axis=0))
    y = up * cw_ref[0:1, :] + u * cw_ref[1:2, :] + un * cw_ref[2:3, :] + cb_ref[...]
    c = d_hy // 3
    x0_ref[0, 0] = y[:, :c]
    vv_ref[0, 0] = y[:, 2 * c:] * y[:, c:2 * c]


def _in_projection(x, shift, scale, gain, w_bf, qg, kg, cosf, sinf, conv_w, conv_b,
                   *, tm, d_attn, d_kv):
    b, length, d = x.shape
    d_hy = conv_w.shape[-1]
    c = d_hy // 3
    p = b // 2
    d_gate = w_bf.shape[1] - d_attn - 2 * d_kv - d_hy
    halo = SUBLANES
    n_halo = length // halo
    per = tm // halo
    row = lambda bi, i: (bi, i, 0)
    per_b = lambda bi, i: (bi, 0, 0)
    pair_major = lambda bi, i: (bi % p, bi // p, i, 0)
    kern = functools.partial(_inproj_kernel, d_attn=d_attn, d_kv=d_kv, d_hy=d_hy)
    return pl.pallas_call(
        kern,
        grid=(b, length // tm),
        in_specs=[pl.BlockSpec((1, tm, d), row),
                  pl.BlockSpec((1, halo, d), lambda bi, i: (bi, jnp.maximum(i * per - 1, 0), 0)),
                  pl.BlockSpec((1, halo, d),
                               lambda bi, i: (bi, jnp.minimum((i + 1) * per, n_halo - 1), 0)),
                  pl.BlockSpec((1, 1, d), per_b),
                  pl.BlockSpec((1, 1, d), per_b),
                  _resident((1, d)),
                  _resident(w_bf.shape),
                  _resident((1, HEAD_DIM)),
                  _resident((1, HEAD_DIM)),
                  pl.BlockSpec((tm, HEAD_DIM), lambda bi, i: (i, 0)),
                  pl.BlockSpec((tm, HEAD_DIM), lambda bi, i: (i, 0)),
                  _resident((3, d_hy)),
                  _resident((1, d_hy))],
        out_specs=[pl.BlockSpec((1, tm, d_attn), row),
                   pl.BlockSpec((1, tm, d_kv), row),
                   pl.BlockSpec((1, tm, d_kv), row),
                   pl.BlockSpec((1, tm, d_gate), row),
                   pl.BlockSpec((1, 1, tm, c), pair_major),
                   pl.BlockSpec((1, 1, tm, c), pair_major)],
        out_shape=[jax.ShapeDtypeStruct((b, length, d_attn), BF16),
                   jax.ShapeDtypeStruct((b, length, d_kv), BF16),
                   jax.ShapeDtypeStruct((b, length, d_kv), BF16),
                   jax.ShapeDtypeStruct((b, length, d_gate), BF16),
                   jax.ShapeDtypeStruct((p, 2, length, c), F32),
                   jax.ShapeDtypeStruct((p, 2, length, c), F32)],
        compiler_params=_params("parallel", "parallel"),
        name="in_projection",
    )(x, x, x, shift, scale, gain, w_bf, qg, kg, cosf, sinf, conv_w, conv_b.reshape(1, d_hy))


def _flash_kernel(*refs, tq, tk, nk, extra):
    if extra:
        (q_ref, qn_ref, k_ref, v_ref, ke_ref, ve_ref, o_ref,
         kall_ref, vall_ref, qs_ref, s_ref, m_ref, acc_ref) = refs
    else:
        q_ref, qn_ref, k_ref, v_ref, o_ref, kall_ref, vall_ref, qs_ref, s_ref, m_ref, acc_ref = refs
    first_slot = 2
    slot_of = lambda j: first_slot if j == 0 else (j - 1) % 2

    def stack_heads(src_ref, which):
        for g in range(KV_GROUP):
            qs_ref[which, g * tq:(g + 1) * tq, :] = src_ref[0, :, g * HEAD_DIM:(g + 1) * HEAD_DIM]

    def chunk(ref, j):
        return ref[j * tk:(j + 1) * tk, :]

    def scores(j, slot, which=0):
        s_ref[slot] = lax.dot_general(qs_ref[which], chunk(kall_ref, j), (((1,), (1,)), ((), ())),
                                      preferred_element_type=F32)

    stack_heads(q_ref, 0)
    stack_heads(qn_ref, 1)

    @pl.when(pl.program_id(2) == 0)
    def _():
        n_main = k_ref.shape[1]
        kall_ref[0:n_main, :] = k_ref[0]
        vall_ref[0:n_main, 0:HEAD_DIM] = v_ref[0]
        if extra:
            kall_ref[n_main:, :] = ke_ref[0]
            vall_ref[n_main:, 0:HEAD_DIM] = ve_ref[0]
        vall_ref[:, HEAD_DIM:] = jnp.ones((vall_ref.shape[0], HEAD_DIM), vall_ref.dtype)
        scores(0, first_slot)

    m_ref[...] = jnp.full(m_ref.shape, -1e30, F32)
    acc_ref[...] = jnp.zeros(acc_ref.shape, F32)
    nt = tk // LANES

    def consume(j, slot):
        tiles = [s_ref[slot, :, t * LANES:(t + 1) * LANES] for t in range(nt)]
        mt = functools.reduce(jnp.maximum, tiles)
        m_prev = m_ref[...]
        m_new = jnp.maximum(m_prev, jnp.max(mt, axis=-1, keepdims=True))
        alpha = jnp.exp2(m_prev - m_new)
        p = jnp.concatenate([jnp.exp2(t - m_new).astype(BF16) for t in tiles], axis=1)
        pv = _dot(p, chunk(vall_ref, j))
        acc_ref[...] = jnp.concatenate([alpha, alpha], axis=1) * acc_ref[...] + pv
        m_ref[...] = m_new

    for j in range(nk - 1):
        scores(j + 1, slot_of(j + 1))
        consume(j, slot_of(j))
    if nk == 1:
        consume(0, first_slot)
        scores(0, first_slot, which=1)
    else:
        scores(0, first_slot, which=1)
        consume(nk - 1, slot_of(nk - 1))

    acc = acc_ref[...]
    out = acc[:, :HEAD_DIM] / acc[:, HEAD_DIM:]
    for g in range(KV_GROUP):
        o_ref[0, :, g * HEAD_DIM:(g + 1) * HEAD_DIM] = out[g * tq:(g + 1) * tq].astype(o_ref.dtype)


def _attention(q, k, v, k_extra=None, v_extra=None, *, tq, tk):
    b, lq, _ = q.shape
    extra = k_extra is not None
    lk = k.shape[1] + (k_extra.shape[1] if extra else 0)
    gw = KV_GROUP * HEAD_DIM
    rows = KV_GROUP * tq
    assert lk % tk == 0
    nq = lq // tq
    kv_spec = lambda a: pl.BlockSpec((1, a.shape[1], HEAD_DIM), lambda bi, h, i: (bi, 0, h))
    kv_args = [k, v] + ([k_extra, v_extra] if extra else [])
    kern = functools.partial(_flash_kernel, tq=tq, tk=tk, nk=lk // tk, extra=extra)
    return pl.pallas_call(
        kern,
        grid=(b, N_KV_HEADS, nq),
        in_specs=[pl.BlockSpec((1, tq, gw), lambda bi, h, i: (bi, i, h)),
                  pl.BlockSpec((1, tq, gw), lambda bi, h, i: (bi, jnp.minimum(i + 1, nq - 1), h))]
                 + [kv_spec(a) for a in kv_args],
        out_specs=pl.BlockSpec((1, tq, gw), lambda bi, h, i: (bi, i, h)),
        out_shape=jax.ShapeDtypeStruct(q.shape, BF16),
        scratch_shapes=[pltpu.VMEM((lk, HEAD_DIM), BF16),
                        pltpu.VMEM((lk, 2 * HEAD_DIM), BF16),
                        pltpu.VMEM((2, rows, HEAD_DIM), BF16),
                        pltpu.VMEM((3, rows, tk), F32),
                        pltpu.VMEM((rows, LANES), F32),
                        pltpu.VMEM((rows, 2 * HEAD_DIM), F32)],
        compiler_params=_params("arbitrary", "arbitrary", "arbitrary"),
        name="attention",
    )(q, q, *kv_args)


def _filter_kernel(z_ref, w1_ref, b1_ref, w2_ref, b2_ref, w3_ref, b3_ref, w4_ref, fr_ref, dec_ref,
                   o_ref):
    c = o_ref.shape[-1]
    h = jnp.sin(fr_ref[0:1, :] * (_dot3(z_ref[...], w1_ref[...]) + b1_ref[...]))
    h = jnp.sin(fr_ref[1:2, :] * (_dot3(h, w2_ref[...]) + b2_ref[...]))
    h = jnp.sin(fr_ref[2:3, :] * (_dot3(h, w3_ref[...]) + b3_ref[...]))
    out = _dot3(h, w4_ref[...])
    o_ref[0] = out[:, :c] * dec_ref[0]
    o_ref[1] = out[:, c:] * dec_ref[1]


def _pad_to(a, shape):
    return jnp.pad(a, [(0, s - d) for d, s in zip(a.shape, shape)])


def _block_diag2(a, b, shape):
    za = jnp.zeros(shape, a.dtype)
    return jnp.concatenate([jnp.concatenate([_pad_to(a, shape), za], axis=1),
                            jnp.concatenate([za, _pad_to(b, shape)], axis=1)], axis=0)


def _hyena_filter(length, fw1, fb1, fw2, fb2, fw3, fb3, fw4, freq, c):
    half = LANES // 2
    assert fw1.shape[0] <= half and fw1.shape[1] <= half
    rev = lambda a: jnp.concatenate([a[:1], a[:0:-1]])
    t = jnp.linspace(0.0, 1.0, length, dtype=F32)
    w = (2.0 * math.pi / length) * jnp.arange(length, dtype=F32)
    f = jnp.linspace(1e-4, FILTER_BANDS - 1, FILTER_BANDS, dtype=F32)[None, :]

    def features(tt, ww):
        tt, ww = tt[:, None], ww[:, None]
        return _pad_to(jnp.concatenate([tt, jnp.cos(f * ww), -jnp.sin(f * ww)], axis=-1),
                       (length, half))

    z = jnp.concatenate([features(t, w), features(rev(t), rev(w))], axis=1)
    max_decay = math.log(DECAY_TARGET) / FAST_DECAY_PCT
    min_decay = math.log(DECAY_TARGET) / SLOW_DECAY_PCT
    deltas = jnp.abs(jnp.linspace(min_decay, max_decay, c, dtype=F32))
    not_first = (jnp.arange(length) > 0).astype(F32)[:, None]
    decay = jnp.stack([jnp.exp(-t[:, None] * deltas),
                       jnp.exp(-rev(t)[:, None] * deltas) * not_first])

    sq = (half, half)
    w1 = _block_diag2(fw1, fw1, sq)
    w2 = _block_diag2(fw2, fw2, sq)
    w3 = _block_diag2(fw3, fw3, sq)
    w4 = _block_diag2(fw4[:, :c], fw4[:, c:], (half, c))
    twice = lambda v: jnp.tile(_pad_to(v.reshape(-1, v.shape[-1]), (v.size // v.shape[-1], half)),
                               (1, 2))
    b1, b2, b3, fr = twice(fb1), twice(fb2), twice(fb3), twice(freq)
    tl = min(length, 1024)
    out = pl.pallas_call(
        _filter_kernel,
        grid=(length // tl,),
        in_specs=[pl.BlockSpec((tl, LANES), lambda i: (i, 0)),
                  _resident(w1.shape), _resident(b1.shape),
                  _resident(w2.shape), _resident(b2.shape),
                  _resident(w3.shape), _resident(b3.shape),
                  _resident(w4.shape), _resident(fr.shape),
                  pl.BlockSpec((2, tl, c), lambda i: (0, i, 0))],
        out_specs=pl.BlockSpec((2, tl, c), lambda i: (0, i, 0)),
        out_shape=jax.ShapeDtypeStruct((2, length, c), F32),
        compiler_params=_params("parallel"),
        name="hyena_filter",
    )(z, w1, b1, w2, b2, w3, b3, w4, fr, decay)
    return out.reshape(2 * length, c)


def _dft_outer_kernel(*refs, nb, c, per_group, hyena):
    if hyena:
        t_ref, z_ref, vv_ref, x0_ref, bias_ref, o_ref = refs
    else:
        t_ref, z_ref, o_ref = refs
    for j in range(nb):
        cols = slice(j * c, (j + 1) * c)
        y = _dot(t_ref[j if per_group else 0], z_ref[0, :, cols].astype(BF16))
        if hyena:
            y = (y + vv_ref[0, :, cols] * bias_ref[...]) * x0_ref[0, :, cols]
        o_ref[0, :, cols] = y.astype(o_ref.dtype)


def _dft_outer(table, z, out_dtype, *, c, hyena_args=None):
    p, k, ncols = z.shape
    nt, m, _ = table.shape
    groups = ncols // c
    nb = min(groups, 8)
    per_group = nt > 1
    t_spec = (pl.BlockSpec((nb, m, k), lambda g, pi: (g, 0, 0)) if per_group
              else _resident((1, m, k)))
    col_spec = lambda rows: pl.BlockSpec((1, rows, nb * c), lambda g, pi: (pi, 0, g))
    in_specs = [t_spec, col_spec(k)]
    args = [table, z]
    if hyena_args is not None:
        vv, x0, bias = hyena_args
        in_specs += [col_spec(m), col_spec(m), _resident((1, c))]
        args += [vv, x0, bias]
    kern = functools.partial(_dft_outer_kernel, nb=nb, c=c, per_group=per_group,
                             hyena=hyena_args is not None)
    return pl.pallas_call(
        kern,
        grid=(groups // nb, p),
        in_specs=in_specs,
        out_specs=col_spec(m),
        out_shape=jax.ShapeDtypeStruct((p, m, ncols), out_dtype),
        compiler_params=_params("parallel", "parallel"),
        name="dft_outer",
    )(*args)


def _dft_strided_kernel(*refs, nb, hyena):
    if hyena:
        t_ref, z_ref, vv_ref, x0_ref, bias_ref, o_ref, zs_ref, os_ref, vs_ref, xs_ref = refs
    else:
        t_ref, z_ref, o_ref, zs_ref, os_ref = refs
    k, cb = z_ref.shape[1], z_ref.shape[3]
    m = o_ref.shape[1]
    nl = cb // LANES
    ng = nb // SUBLANES
    lane = lambda t: slice(t * LANES, (t + 1) * LANES)

    def stage(dst_ref, val):
        for t in range(nl):
            for g in range(ng):
                piece = val[:, g * SUBLANES:(g + 1) * SUBLANES, lane(t)]
                dst_ref[t, g] = piece.reshape(val.shape[0] * SUBLANES, LANES)

    def rows(src_ref, j, n):
        g, r = divmod(j, SUBLANES)
        return jnp.concatenate([src_ref[t, g, pl.ds(r, n, stride=SUBLANES), :]
                                for t in range(nl)], axis=1)

    stage(zs_ref, z_ref[0].astype(F32))
    if hyena:
        stage(vs_ref, vv_ref[0])
        stage(xs_ref, x0_ref[0])
    for j in range(nb):
        y = _dot(t_ref[j], rows(zs_ref, j, k).astype(BF16))
        if hyena:
            y = (y + rows(vs_ref, j, m) * bias_ref[...]) * rows(xs_ref, j, m)
        g, r = divmod(j, SUBLANES)
        for t in range(nl):
            os_ref[t, g, pl.ds(r, m, stride=SUBLANES), :] = y[:, lane(t)]
    o_ref[0] = jnp.concatenate(
        [jnp.concatenate([os_ref[t, g].reshape(m, SUBLANES, LANES) for g in range(ng)], axis=1)
         for t in range(nl)], axis=2).astype(o_ref.dtype)


def _dft_outer_strided(table, z, out_dtype, *, cb, hyena_args=None):
    p, k, n2, c = z.shape
    _, m, _ = table.shape
    nb = 16
    blk = lambda rows: pl.BlockSpec((1, rows, nb, cb), lambda g, ci, pi: (pi, 0, g, ci))
    in_specs = [pl.BlockSpec((nb, m, k), lambda g, ci, pi: (g, 0, 0)), blk(k)]
    args = [table, z]
    staging = lambda rows: pltpu.VMEM((cb // LANES, nb // SUBLANES, rows * SUBLANES, LANES), F32)
    scratch = [staging(k), staging(m)]
    if hyena_args is not None:
        vv, x0, bias = hyena_args
        in_specs += [blk(m), blk(m), pl.BlockSpec((1, cb), lambda g, ci, pi: (0, ci))]
        args += [vv, x0, bias]
        scratch += [staging(m), staging(m)]
    kern = functools.partial(_dft_strided_kernel, nb=nb, hyena=hyena_args is not None)
    return pl.pallas_call(
        kern,
        grid=(n2 // nb, c // cb, p),
        in_specs=in_specs,
        out_specs=blk(m),
        out_shape=jax.ShapeDtypeStruct((p, m, n2, c), out_dtype),
        scratch_shapes=scratch,
        compiler_params=_params("parallel", "parallel", "parallel"),
        name="dft_outer_strided",
    )(*args)


def _dft_inner_kernel(*refs, kc, n2, filtered):
    if filtered:
        mf_ref, mi_ref, x_ref, f_ref, o_ref = refs
    else:
        mf_ref, x_ref, o_ref = refs
    for kk in range(kc):
        zin = jnp.concatenate([x_ref[0, 0, kk], x_ref[0, 1, kk]], axis=0)
        y = _dot(mf_ref[...], zin)
        re, im = y[:n2], y[n2:]
        if filtered:
            fr, fi = f_ref[0, 0, kk], f_ref[0, 1, kk]
            prod = jnp.concatenate([re * fr - im * fi, re * fi + im * fr], axis=0)
            y = _dot(mi_ref[...], prod.astype(BF16))
            re, im = y[:n2], y[n2:]
        o_ref[0, 0, kk] = re.astype(o_ref.dtype)
        o_ref[0, 1, kk] = im.astype(o_ref.dtype)


def _dft_inner(m_fwd, x, out_dtype, m_inv=None, spectrum=None):
    p, _, n1, n2, c = x.shape
    kc = 8
    blk = lambda g, pi: (pi, 0, g, 0, 0)
    filtered = spectrum is not None
    in_specs = [_resident(m_fwd.shape)]
    args = [m_fwd]
    if filtered:
        in_specs.append(_resident(m_inv.shape))
        args.append(m_inv)
    in_specs.append(pl.BlockSpec((1, 2, kc, n2, c), blk))
    args.append(x)
    if filtered:
        in_specs.append(pl.BlockSpec((1, 2, kc, n2, c), lambda g, pi: (0, 0, g, 0, 0)))
        args.append(spectrum)
    kern = functools.partial(_dft_inner_kernel, kc=kc, n2=n2, filtered=filtered)
    return pl.pallas_call(
        kern,
        grid=(n1 // kc, p),
        in_specs=in_specs,
        out_specs=pl.BlockSpec((1, 2, kc, n2, c), blk),
        out_shape=jax.ShapeDtypeStruct(x.shape, out_dtype),
        compiler_params=_params("parallel", "parallel"),
        name="dft_inner",
    )(*args)


def _cmul_kernel(x_ref, f_ref, o_ref, *, n):
    xr, xi = x_ref[0, :n], x_ref[0, n:]
    fr, fi = f_ref[0, :n], f_ref[0, n:]
    o_ref[0, :n] = (xr * fr - xi * fi).astype(o_ref.dtype)
    o_ref[0, n:] = (xr * fi + xi * fr).astype(o_ref.dtype)


def _cmul(x, f, out_dtype):
    p, n2x, c = x.shape
    kern = functools.partial(_cmul_kernel, n=n2x // 2)
    return pl.pallas_call(
        kern,
        grid=(p,),
        in_specs=[pl.BlockSpec((1, n2x, c), lambda pi: (pi, 0, 0)),
                  pl.BlockSpec((1, n2x, c), lambda pi: (0, 0, 0))],
        out_specs=pl.BlockSpec((1, n2x, c), lambda pi: (pi, 0, 0)),
        out_shape=jax.ShapeDtypeStruct(x.shape, out_dtype),
        compiler_params=_params("parallel"),
        name="spectrum_product",
    )(x, f)


def _phase_tables(rows_n1, n1_total, n2_total):
    n = n1_total * n2_total
    n2 = jnp.arange(n2_total, dtype=jnp.int32)[:, None, None]
    k1 = jnp.arange(n1_total, dtype=jnp.int32)[None, :, None]
    n1 = jnp.arange(rows_n1, dtype=jnp.int32)[None, None, :]
    ph = ((n2_total * n1 + n2) * k1) % n
    ang = ph.astype(F32) * (2.0 * math.pi / n)
    return jnp.cos(ang), jnp.sin(ang)


def _forward_tables(length, n1_total, n2_total):
    rows = length // n2_total
    cs, sn = _phase_tables(rows, n1_total, n2_total)
    paired = jnp.concatenate([jnp.concatenate([cs, sn], axis=2),
                              jnp.concatenate([-sn, cs], axis=2)], axis=1)
    cs, sn = _phase_tables(n1_total, n1_total, n2_total)
    real = jnp.concatenate([cs, -sn], axis=1)
    return paired.astype(BF16), real.astype(BF16)


def _inverse_table(length, n1_total, n2_total):
    rows = length // n2_total
    cs, sn = _phase_tables(rows, n1_total, n2_total)
    cs = jnp.swapaxes(cs, 1, 2) / (n1_total * n2_total)
    sn = jnp.swapaxes(sn, 1, 2) / (n1_total * n2_total)
    tb = jnp.concatenate([jnp.concatenate([cs, -sn], axis=2),
                          jnp.concatenate([sn, cs], axis=2)], axis=1)
    return tb.astype(BF16)


def _inner_matrices(n2_total):
    idx = np.arange(n2_total)
    ang = 2.0 * np.pi * ((idx[:, None] * idx[None, :]) % n2_total) / n2_total
    cs, sn = np.cos(ang), np.sin(ang)
    fwd = np.block([[cs, sn], [-sn, cs]])
    inv = np.block([[cs, -sn], [sn, cs]])
    return jnp.asarray(fwd, dtype=BF16), jnp.asarray(inv, dtype=BF16)


def _long_conv_gate(vv, x0, filt, bias):
    p, _, length, c = vv.shape
    n = 2 * length
    n1_total = DFT_N1 if n % DFT_N1 == 0 and n // DFT_N1 >= 8 else n
    n2_total = n // n1_total
    rows = length // n2_total
    t_pair, t_real = _forward_tables(length, n1_total, n2_total)
    t_inv = _inverse_table(length, n1_total, n2_total)

    bias2 = bias.reshape(1, c)
    if n2_total > 1:
        m_fwd, m_inv = _inner_matrices(n2_total)
        shape5 = lambda a: a.reshape(a.shape[0], 2, n1_total, n2_total, c)
        vv4 = vv.reshape(p, 2 * rows, n2_total, c)
        x04 = x0.reshape(p, 2 * rows, n2_total, c)
        spec = _dft_outer_strided(t_real, filt.reshape(1, n1_total, n2_total, c), BF16, cb=c)
        spec = _dft_inner(m_fwd, shape5(spec), F32)
        y = _dft_outer_strided(t_pair, vv4, BF16, cb=c)
        y = _dft_inner(m_fwd, shape5(y), BF16, m_inv=m_inv, spectrum=spec)
        y = y.reshape(p, 2 * n1_total, n2_total, c)
        out = _dft_outer_strided(t_inv, y, BF16, cb=c // 2, hyena_args=(vv4, x04, bias2))
    else:
        vv2 = vv.reshape(p, 2 * rows, c)
        x02 = x0.reshape(p, 2 * rows, c)
        spec = _dft_outer(t_real, filt.reshape(1, n1_total, c), F32, c=c)
        y = _dft_outer(t_pair, vv2, F32, c=c)
        y = _cmul(y, spec, BF16)
        out = _dft_outer(t_inv, y, BF16, c=c, hyena_args=(vv2, x02, bias2))
    return out.reshape(p, 2, length, c)


def _merge_kernel(x_ref, attn_ref, hy_ref, g_ref, gate_ref, woa_ref, woh_ref, wout_ref, o_ref, *, d):
    g = jax.nn.sigmoid(g_ref[0].astype(F32))
    merged = (g[:, :d] * _dot(attn_ref[0], woa_ref[...])
              + g[:, d:] * _dot(hy_ref[0, 0], woh_ref[...]))
    o_ref[0] = x_ref[0] + gate_ref[0] * _dot(merged.astype(BF16), wout_ref[...])


def _merge(x, attn, hyo, g, gate, woa, woh, wout, *, tm):
    b, length, d = x.shape
    p = b // 2
    c = hyo.shape[-1]
    row = lambda bi, i: (bi, i, 0)
    kern = functools.partial(_merge_kernel, d=d)
    return pl.pallas_call(
        kern,
        grid=(b, length // tm),
        in_specs=[pl.BlockSpec((1, tm, d), row),
                  pl.BlockSpec((1, tm, attn.shape[-1]), row),
                  pl.BlockSpec((1, 1, tm, c), lambda bi, i: (bi % p, bi // p, i, 0)),
                  pl.BlockSpec((1, tm, g.shape[-1]), row),
                  pl.BlockSpec((1, 1, d), lambda bi, i: (bi, 0, 0)),
                  _resident(woa.shape), _resident(woh.shape), _resident(wout.shape)],
        out_specs=pl.BlockSpec((1, tm, d), row),
        out_shape=jax.ShapeDtypeStruct(x.shape, F32),
        compiler_params=_params("parallel", "parallel"),
        name="merge",
    )(x, attn, hyo, g, gate, woa, woh, wout)


def _ffn_kernel(x_ref, shift_ref, scale_ref, gate_ref, gain_ref, wgu_ref, wd_ref, fin_ref, o_ref,
                *, d_ff, chunk, final_norm):
    x = x_ref[0]
    h = ((_rms(x) * gain_ref[...]) * (1.0 + scale_ref[0]) + shift_ref[0]).astype(BF16)
    acc = jnp.zeros(x.shape, F32)
    for c0 in range(0, d_ff, chunk):
        gt = _dot(h, wgu_ref[:, c0:c0 + chunk])
        up = _dot(h, wgu_ref[:, d_ff + c0:d_ff + c0 + chunk])
        act = (gt * jax.nn.sigmoid(gt) * up).astype(BF16)
        acc = acc + _dot(act, wd_ref[c0:c0 + chunk, :])
    y = x + gate_ref[0] * acc
    if final_norm:
        y = _rms(y) * fin_ref[...]
    o_ref[0] = y


def _ffn(x, shift, scale, gate, gain, wgu, wd, fin_gain, *, tm, final_norm):
    b, length, d = x.shape
    d_ff = wd.shape[0]
    row = lambda bi, i: (bi, i, 0)
    per_b = lambda bi, i: (bi, 0, 0)
    kern = functools.partial(_ffn_kernel, d_ff=d_ff, chunk=256, final_norm=final_norm)
    return pl.pallas_call(
        kern,
        grid=(b, length // tm),
        in_specs=[pl.BlockSpec((1, tm, d), row),
                  pl.BlockSpec((1, 1, d), per_b),
                  pl.BlockSpec((1, 1, d), per_b),
                  pl.BlockSpec((1, 1, d), per_b),
                  _resident((1, d)),
                  _resident(wgu.shape), _resident(wd.shape),
                  _resident((1, d))],
        out_specs=pl.BlockSpec((1, tm, d), row),
        out_shape=jax.ShapeDtypeStruct(x.shape, F32),
        compiler_params=_params("parallel", "parallel"),
        name="ffn",
    )(x, shift, scale, gate, gain, wgu, wd, fin_gain)


def _rope_tables(length):
    rows = length // GRID_W
    axis_dim = HEAD_DIM // 2
    row = jnp.repeat(jnp.arange(rows, dtype=F32), GRID_W)
    col = jnp.tile(jnp.arange(GRID_W, dtype=F32), rows)
    inv_freq = ROPE_THETA ** (-jnp.arange(0, axis_dim, 2, dtype=F32) / axis_dim)
    ang = jnp.concatenate([row[:, None] * inv_freq, col[:, None] * inv_freq], axis=-1)
    cs, sn = jnp.cos(ang), jnp.sin(ang)
    return jnp.concatenate([cs, cs], axis=-1), jnp.concatenate([-sn, sn], axis=-1)


def _head_perm(n_heads):
    within = np.concatenate([np.arange(0, HEAD_DIM, 2), np.arange(1, HEAD_DIM, 2)])
    return np.concatenate([h * HEAD_DIM + within for h in range(n_heads)]), within


def kernel(x, c, ctx, c_ctx, w_mod, b_mod, norm_mix, w_in, q_norm, k_norm, conv_w, conv_b, filt_w1, filt_b1, filt_w2, filt_b2, filt_w3, filt_b3, filt_w4, filt_freq, hyena_bias, w_o_attn, w_o_hyena, w_out, norm_ffn, w_gate_up, w_down, norm_final):
    b, length, d = x.shape
    ctx_len = ctx.shape[1]
    depth = w_mod.shape[0]
    d_attn = N_HEADS * HEAD_DIM
    d_kv = N_KV_HEADS * HEAD_DIM
    d_hy = conv_w.shape[-1]
    c_hy = d_hy // 3
    assert b % 2 == 0 and length % 512 == 0 and ctx_len % 256 == 0

    perm, within = _head_perm(N_HEADS + N_KV_HEADS)
    col_order = np.concatenate([perm, np.arange(d_attn + d_kv, w_in.shape[-1])])

    cosf, sinf = _rope_tables(length)
    cos_ctx = jnp.ones((ctx_len, HEAD_DIM), F32)
    sin_ctx = jnp.zeros((ctx_len, HEAD_DIM), F32)

    m_rows = 16
    c_rows = jnp.zeros((m_rows, d), F32).at[:b].set(c).at[b].set(c_ctx)
    tm = 512
    tm_ctx = ctx_len
    fin = norm_final.reshape(1, d)

    for layer in range(depth):
        last = layer == depth - 1
        mod_all = _modulation(c_rows, w_mod[layer], b_mod[layer])
        mods = [m.reshape(b, 1, d) for m in jnp.split(mod_all[:b], N_MOD, axis=-1)]
        mods_c = [jnp.broadcast_to(m.reshape(1, 1, d), (b, 1, d))
                  for m in jnp.split(mod_all[b], N_MOD, axis=-1)]
        shift1, scale1, gate1, shift2, scale2, gate2 = mods
        c_shift1, c_scale1, c_gate1, c_shift2, c_scale2, c_gate2 = mods_c

        w_in_bf = w_in[layer][:, col_order].astype(BF16)
        qg = (q_norm[layer][within] * (HEAD_DIM ** -0.5 * math.log2(math.e))).reshape(1, HEAD_DIM)
        kg = k_norm[layer][within].reshape(1, HEAD_DIM)
        gain_mix = norm_mix[layer].reshape(1, d)
        gain_ffn = norm_ffn[layer].reshape(1, d)
        woa = w_o_attn[layer].astype(BF16)
        woh = w_o_hyena[layer].astype(BF16)
        wout = w_out[layer].astype(BF16)
        wgu = w_gate_up[layer].astype(BF16)
        wd = w_down[layer].astype(BF16)
        filt = (filt_w1[layer], filt_b1[layer], filt_w2[layer], filt_b2[layer],
                filt_w3[layer], filt_b3[layer], filt_w4[layer], filt_freq[layer])
        proj = functools.partial(_in_projection, gain=gain_mix, w_bf=w_in_bf, qg=qg, kg=kg,
                                 conv_w=conv_w[layer], conv_b=conv_b[layer],
                                 d_attn=d_attn, d_kv=d_kv)

        qc, kc, vc, gc, vv_c, x0_c = proj(ctx, c_shift1, c_scale1, cosf=cos_ctx, sinf=sin_ctx,
                                          tm=tm_ctx)
        q, k, v, g, vv, x0 = proj(x, shift1, scale1, cosf=cosf, sinf=sinf, tm=tm)

        attn = _attention(q, k, v, kc, vc,
                          tq=256, tk=768 if (ctx_len + length) % 768 == 0 else 256)
        hyo = _long_conv_gate(vv, x0, _hyena_filter(length, *filt, c_hy), hyena_bias[layer])
        x_mid = _merge(x, attn, hyo, g, gate1, woa, woh, wout, tm=tm)

        if not last:
            attn_c = _attention(qc, kc, vc, tq=ctx_len, tk=ctx_len)
            hyo_c = _long_conv_gate(vv_c, x0_c, _hyena_filter(ctx_len, *filt, c_hy),
                                    hyena_bias[layer])
            ctx = _merge(ctx, attn_c, hyo_c, gc, c_gate1, woa, woh, wout, tm=tm_ctx)
            ctx = _ffn(ctx, c_shift2, c_scale2, c_gate2, gain_ffn, wgu, wd, fin,
                       tm=tm_ctx, final_norm=False)

        x = _ffn(x_mid, shift2, scale2, gate2, gain_ffn, wgu, wd, fin, tm=tm, final_norm=last)
    return x
```

```python
import functools
import math

import numpy as np
import jax
import jax.numpy as jnp
from jax import lax
from jax.experimental import pallas as pl
from jax.experimental.pallas import tpu as pltpu

F32 = jnp.float32
BF16 = jnp.bfloat16

GRID_W = 64
N_HEADS = 8
N_KV_HEADS = 2
KV_GROUP = N_HEADS // N_KV_HEADS
HEAD_DIM = 128
ROPE_THETA = 10000.0
FILTER_EMB = 33
FILTER_BANDS = (FILTER_EMB - 1) // 2
DECAY_TARGET = 1e-2
FAST_DECAY_PCT = 0.3
SLOW_DECAY_PCT = 1.5
N_MOD = 6
EPS = 1e-6

LANES = 128
SUBLANES = 8
VMEM_LIMIT = 48 * 1024 * 1024
DFT_N1 = 128


def _params(*sem):
    return pltpu.CompilerParams(dimension_semantics=sem, vmem_limit_bytes=VMEM_LIMIT)


def _resident(shape):
    nd = len(shape)
    return pl.BlockSpec(shape, lambda *_: (0,) * nd, pipeline_mode=pl.Buffered(1))


def _dot(a, b):
    return jnp.dot(a, b, preferred_element_type=F32)


def _split_bf16(a):
    hi = a.astype(BF16)
    lo = (a - hi.astype(F32)).astype(BF16)
    return hi, lo


def _dot3(a, b):
    ah, al = _split_bf16(a)
    bh, bl = _split_bf16(b)
    return _dot(ah, bh) + (_dot(ah, bl) + _dot(al, bh))


def _rms(x):
    return x * lax.rsqrt(jnp.mean(x * x, axis=-1, keepdims=True) + EPS)


def _mod_kernel(c_ref, w_ref, b_ref, o_ref):
    c = c_ref[...]
    s = c * jax.nn.sigmoid(c)
    o_ref[...] = _dot3(s, w_ref[...]) + b_ref[...]


def _modulation(c_rows, w, b):
    m, d = c_rows.shape
    n = w.shape[1]
    tn = 1536
    return pl.pallas_call(
        _mod_kernel,
        grid=(n // tn,),
        in_specs=[pl.BlockSpec((m, d), lambda j: (0, 0)),
                  pl.BlockSpec((d, tn), lambda j: (0, j)),
                  pl.BlockSpec((1, tn), lambda j: (0, j))],
        out_specs=pl.BlockSpec((m, tn), lambda j: (0, j)),
        out_shape=jax.ShapeDtypeStruct((m, n), F32),
        compiler_params=_params("parallel"),
        name="modulation",
    )(c_rows, w, b.reshape(1, n))


def _inproj_kernel(x_ref, xp_ref, xn_ref, shift_ref, scale_ref, gain_ref, w_ref, qg_ref, kg_ref,
                   cos_ref, sin_ref, cw_ref, cb_ref,
                   q_ref, k_ref, v_ref, g_ref, vv_ref, x0_ref, *, d_attn, d_kv, d_hy):
    i = pl.program_id(1)
    tm = x_ref.shape[1]
    halo = xp_ref.shape[1]
    x = jnp.concatenate([x_ref[0], xp_ref[0], xn_ref[0]], axis=0)
    h = (_rms(x) * gain_ref[...]) * (1.0 + scale_ref[0]) + shift_ref[0]
    hb_all = h.astype(BF16)
    hb = hb_all[:tm]
    cosf = cos_ref[...]
    sinf = sin_ref[...]

    def head_norm_rope(t, gain):
        t = _rms(t) * gain
        return t * cosf + pltpu.roll(t, HEAD_DIM // 2, axis=1) * sinf

    c0 = 0
    q = _dot(hb, w_ref[:, c0:c0 + d_attn])
    for hd in range(d_attn // HEAD_DIM):
        sl = slice(hd * HEAD_DIM, (hd + 1) * HEAD_DIM)
        q_ref[0, :, sl] = head_norm_rope(q[:, sl], qg_ref[...]).astype(q_ref.dtype)
    c0 += d_attn
    k = _dot(hb, w_ref[:, c0:c0 + d_kv])
    for hd in range(d_kv // HEAD_DIM):
        sl = slice(hd * HEAD_DIM, (hd + 1) * HEAD_DIM)
        k_ref[0, :, sl] = head_norm_rope(k[:, sl], kg_ref[...]).astype(k_ref.dtype)
    c0 += d_kv
    v_ref[0] = _dot(hb, w_ref[:, c0:c0 + d_kv]).astype(v_ref.dtype)
    c0 += d_kv
    u_all = _dot(hb_all, w_ref[:, c0:c0 + d_hy])
    c0 += d_hy
    g_ref[0] = _dot(hb, w_ref[:, c0:]).astype(g_ref.dtype)

    u = u_all[:tm]
    prev_row = jnp.where(i == 0, 0.0, u_all[tm + halo - 1:tm + halo])
    next_row = jnp.where(i == pl.num_programs(1) - 1, 0.0, u_all[tm + halo:tm + halo + 1])
    row = lax.broadcasted_iota(jnp.int32, (tm, 1), 0)
    up = jnp.where(row == 0, prev_row, pltpu.roll(u, 1, axis=0))
    un = jnp.where(row == tm - 1, next_row, pltpu.roll(u, tm - 1, axis=0))
    y = up * cw_ref[0:1, :] + u * cw_ref[1:2, :] + un * cw_ref[2:3, :] + cb_ref[...]
    c = d_hy // 3
    x0_ref[0, 0] = y[:, :c]
    vv_ref[0, 0] = y[:, 2 * c:] * y[:, c:2 * c]


def _in_projection(x, shift, scale, gain, w_bf, qg, kg, cosf, sinf, conv_w, conv_b,
                   *, tm, d_attn, d_kv):
    b, length, d = x.shape
    d_hy = conv_w.shape[-1]
    c = d_hy // 3
    p = b // 2
    d_gate = w_bf.shape[1] - d_attn - 2 * d_kv - d_hy
    halo = SUBLANES
    n_halo = length // halo
    per = tm // halo
    row = lambda bi, i: (bi, i, 0)
    per_b = lambda bi, i: (bi, 0, 0)
    pair_major = lambda bi, i: (bi % p, bi // p, i, 0)
    kern = functools.partial(_inproj_kernel, d_attn=d_attn, d_kv=d_kv, d_hy=d_hy)
    return pl.pallas_call(
        kern,
        grid=(b, length // tm),
        in_specs=[pl.BlockSpec((1, tm, d), row),
                  pl.BlockSpec((1, halo, d), lambda bi, i: (bi, jnp.maximum(i * per - 1, 0), 0)),
                  pl.BlockSpec((1, halo, d),
                               lambda bi, i: (bi, jnp.minimum((i + 1) * per, n_halo - 1), 0)),
                  pl.BlockSpec((1, 1, d), per_b),
                  pl.BlockSpec((1, 1, d), per_b),
                  _resident((1, d)),
                  _resident(w_bf.shape),
                  _resident((1, HEAD_DIM)),
                  _resident((1, HEAD_DIM)),
                  pl.BlockSpec((tm, HEAD_DIM), lambda bi, i: (i, 0)),
                  pl.BlockSpec((tm, HEAD_DIM), lambda bi, i: (i, 0)),
                  _resident((3, d_hy)),
                  _resident((1, d_hy))],
        out_specs=[pl.BlockSpec((1, tm, d_attn), row),
                   pl.BlockSpec((1, tm, d_kv), row),
                   pl.BlockSpec((1, tm, d_kv), row),
                   pl.BlockSpec((1, tm, d_gate), row),
                   pl.BlockSpec((1, 1, tm, c), pair_major),
                   pl.BlockSpec((1, 1, tm, c), pair_major)],
        out_shape=[jax.ShapeDtypeStruct((b, length, d_attn), BF16),
                   jax.ShapeDtypeStruct((b, length, d_kv), BF16),
                   jax.ShapeDtypeStruct((b, length, d_kv), BF16),
                   jax.ShapeDtypeStruct((b, length, d_gate), BF16),
                   jax.ShapeDtypeStruct((p, 2, length, c), F32),
                   jax.ShapeDtypeStruct((p, 2, length, c), F32)],
        compiler_params=_params("parallel", "parallel"),
        name="in_projection",
    )(x, x, x, shift, scale, gain, w_bf, qg, kg, cosf, sinf, conv_w, conv_b.reshape(1, d_hy))


def _flash_kernel(*refs, tq, tk, nk, extra):
    if extra:
        (q_ref, qn_ref, k_ref, v_ref, ke_ref, ve_ref, o_ref,
         kall_ref, vall_ref, qs_ref, s_ref, m_ref, acc_ref) = refs
    else:
        q_ref, qn_ref, k_ref, v_ref, o_ref, kall_ref, vall_ref, qs_ref, s_ref, m_ref, acc_ref = refs
    first_slot = 2
    slot_of = lambda j: first_slot if j == 0 else (j - 1) % 2

    def stack_heads(src_ref, which):
        for g in range(KV_GROUP):
            qs_ref[which, g * tq:(g + 1) * tq, :] = src_ref[0, :, g * HEAD_DIM:(g + 1) * HEAD_DIM]

    def chunk(ref, j):
        return ref[j * tk:(j + 1) * tk, :]

    def scores(j, slot, which=0):
        s_ref[slot] = lax.dot_general(qs_ref[which], chunk(kall_ref, j), (((1,), (1,)), ((), ())),
                                      preferred_element_type=F32)

    stack_heads(q_ref, 0)
    stack_heads(qn_ref, 1)

    @pl.when(pl.program_id(2) == 0)
    def _():
        n_main = k_ref.shape[1]
        kall_ref[0:n_main, :] = k_ref[0]
        vall_ref[0:n_main, 0:HEAD_DIM] = v_ref[0]
        if extra:
            kall_ref[n_main:, :] = ke_ref[0]
            vall_ref[n_main:, 0:HEAD_DIM] = ve_ref[0]
        vall_ref[:, HEAD_DIM:] = jnp.ones((vall_ref.shape[0], HEAD_DIM), vall_ref.dtype)
        scores(0, first_slot)

    m_ref[...] = jnp.full(m_ref.shape, -1e30, F32)
    acc_ref[...] = jnp.zeros(acc_ref.shape, F32)
    nt = tk // LANES

    def consume(j, slot):
        tiles = [s_ref[slot, :, t * LANES:(t + 1) * LANES] for t in range(nt)]
        mt = functools.reduce(jnp.maximum, tiles)
        m_prev = m_ref[...]
        m_new = jnp.maximum(m_prev, jnp.max(mt, axis=-1, keepdims=True))
        alpha = jnp.exp2(m_prev - m_new)
        p = jnp.concatenate([jnp.exp2(t - m_new).astype(BF16) for t in tiles], axis=1)
        pv = _dot(p, chunk(vall_ref, j))
        acc_ref[...] = jnp.concatenate([alpha, alpha], axis=1) * acc_ref[...] + pv
        m_ref[...] = m_new

    for j in range(nk - 1):
        scores(j + 1, slot_of(j + 1))
        consume(j, slot_of(j))
    if nk == 1:
        consume(0, first_slot)
        scores(0, first_slot, which=1)
    else:
        scores(0, first_slot, which=1)
        consume(nk - 1, slot_of(nk - 1))

    acc = acc_ref[...]
    out = acc[:, :HEAD_DIM] / acc[:, HEAD_DIM:]
    for g in range(KV_GROUP):
        o_ref[0, :, g * HEAD_DIM:(g + 1) * HEAD_DIM] = out[g * tq:(g + 1) * tq].astype(o_ref.dtype)


def _attention(q, k, v, k_extra=None, v_extra=None, *, tq, tk):
    b, lq, _ = q.shape
    extra = k_extra is not None
    lk = k.shape[1] + (k_extra.shape[1] if extra else 0)
    gw = KV_GROUP * HEAD_DIM
    rows = KV_GROUP * tq
    assert lk % tk == 0
    nq = lq // tq
    kv_spec = lambda a: pl.BlockSpec((1, a.shape[1], HEAD_DIM), lambda bi, h, i: (bi, 0, h))
    kv_args = [k, v] + ([k_extra, v_extra] if extra else [])
    kern = functools.partial(_flash_kernel, tq=tq, tk=tk, nk=lk // tk, extra=extra)
    return pl.pallas_call(
        kern,
        grid=(b, N_KV_HEADS, nq),
        in_specs=[pl.BlockSpec((1, tq, gw), lambda bi, h, i: (bi, i, h)),
                  pl.BlockSpec((1, tq, gw), lambda bi, h, i: (bi, jnp.minimum(i + 1, nq - 1), h))]
                 + [kv_spec(a) for a in kv_args],
        out_specs=pl.BlockSpec((1, tq, gw), lambda bi, h, i: (bi, i, h)),
        out_shape=jax.ShapeDtypeStruct(q.shape, BF16),
        scratch_shapes=[pltpu.VMEM((lk, HEAD_DIM), BF16),
                        pltpu.VMEM((lk, 2 * HEAD_DIM), BF16),
                        pltpu.VMEM((2, rows, HEAD_DIM), BF16),
                        pltpu.VMEM((3, rows, tk), F32),
                        pltpu.VMEM((rows, LANES), F32),
                        pltpu.VMEM((rows, 2 * HEAD_DIM), F32)],
        compiler_params=_params("arbitrary", "arbitrary", "arbitrary"),
        name="attention",
    )(q, q, *kv_args)


def _filter_kernel(z_ref, w1_ref, b1_ref, w2_ref, b2_ref, w3_ref, b3_ref, w4_ref, fr_ref, dec_ref,
                   o_ref):
    c = o_ref.shape[-1]
    h = jnp.sin(fr_ref[0:1, :] * (_dot3(z_ref[...], w1_ref[...]) + b1_ref[...]))
    h = jnp.sin(fr_ref[1:2, :] * (_dot3(h, w2_ref[...]) + b2_ref[...]))
    h = jnp.sin(fr_ref[2:3, :] * (_dot3(h, w3_ref[...]) + b3_ref[...]))
    out = _dot3(h, w4_ref[...])
    o_ref[0] = out[:, :c] * dec_ref[0]
    o_ref[1] = out[:, c:] * dec_ref[1]


def _pad_to(a, shape):
    return jnp.pad(a, [(0, s - d) for d, s in zip(a.shape, shape)])


def _block_diag2(a, b, shape):
    za = jnp.zeros(shape, a.dtype)
    return jnp.concatenate([jnp.concatenate([_pad_to(a, shape), za], axis=1),
                            jnp.concatenate([za, _pad_to(b, shape)], axis=1)], axis=0)


def _hyena_filter(length, fw1, fb1, fw2, fb2, fw3, fb3, fw4, freq, c):
    half = LANES // 2
    assert fw1.shape[0] <= half and fw1.shape[1] <= half
    rev = lambda a: jnp.concatenate([a[:1], a[:0:-1]])
    t = jnp.linspace(0.0, 1.0, length, dtype=F32)
    w = (2.0 * math.pi / length) * jnp.arange(length, dtype=F32)
    f = jnp.linspace(1e-4, FILTER_BANDS - 1, FILTER_BANDS, dtype=F32)[None, :]

    def features(tt, ww):
        tt, ww = tt[:, None], ww[:, None]
        return _pad_to(jnp.concatenate([tt, jnp.cos(f * ww), -jnp.sin(f * ww)], axis=-1),
                       (length, half))

    z = jnp.concatenate([features(t, w), features(rev(t), rev(w))], axis=1)
    max_decay = math.log(DECAY_TARGET) / FAST_DECAY_PCT
    min_decay = math.log(DECAY_TARGET) / SLOW_DECAY_PCT
    deltas = jnp.abs(jnp.linspace(min_decay, max_decay, c, dtype=F32))
    not_first = (jnp.arange(length) > 0).astype(F32)[:, None]
    decay = jnp.stack([jnp.exp(-t[:, None] * deltas),
                       jnp.exp(-rev(t)[:, None] * deltas) * not_first])

    sq = (half, half)
    w1 = _block_diag2(fw1, fw1, sq)
    w2 = _block_diag2(fw2, fw2, sq)
    w3 = _block_diag2(fw3, fw3, sq)
    w4 = _block_diag2(fw4[:, :c], fw4[:, c:], (half, c))
    twice = lambda v: jnp.tile(_pad_to(v.reshape(-1, v.shape[-1]), (v.size // v.shape[-1], half)),
                               (1, 2))
    b1, b2, b3, fr = twice(fb1), twice(fb2), twice(fb3), twice(freq)
    tl = min(length, 1024)
    out = pl.pallas_call(
        _filter_kernel,
        grid=(length // tl,),
        in_specs=[pl.BlockSpec((tl, LANES), lambda i: (i, 0)),
                  _resident(w1.shape), _resident(b1.shape),
                  _resident(w2.shape), _resident(b2.shape),
                  _resident(w3.shape), _resident(b3.shape),
                  _resident(w4.shape), _resident(fr.shape),
                  pl.BlockSpec((2, tl, c), lambda i: (0, i, 0))],
        out_specs=pl.BlockSpec((2, tl, c), lambda i: (0, i, 0)),
        out_shape=jax.ShapeDtypeStruct((2, length, c), F32),
        compiler_params=_params("parallel"),
        name="hyena_filter",
    )(z, w1, b1, w2, b2, w3, b3, w4, fr, decay)
    return out.reshape(2 * length, c)


def _dft_outer_kernel(*refs, nb, c, per_group, hyena):
    if hyena:
        t_ref, z_ref, vv_ref, x0_ref, bias_ref, o_ref = refs
    else:
        t_ref, z_ref, o_ref = refs
    for j in range(nb):
        cols = slice(j * c, (j + 1) * c)
        y = _dot(t_ref[j if per_group else 0], z_ref[0, :, cols].astype(BF16))
        if hyena:
            y = (y + vv_ref[0, :, cols] * bias_ref[...]) * x0_ref[0, :, cols]
        o_ref[0, :, cols] = y.astype(o_ref.dtype)


def _dft_outer(table, z, out_dtype, *, c, hyena_args=None):
    p, k, ncols = z.shape
    nt, m, _ = table.shape
    groups = ncols // c
    nb = min(groups, 8)
    per_group = nt > 1
    t_spec = (pl.BlockSpec((nb, m, k), lambda g, pi: (g, 0, 0)) if per_group
              else _resident((1, m, k)))
    col_spec = lambda rows: pl.BlockSpec((1, rows, nb * c), lambda g, pi: (pi, 0, g))
    in_specs = [t_spec, col_spec(k)]
    args = [table, z]
    if hyena_args is not None:
        vv, x0, bias = hyena_args
        in_specs += [col_spec(m), col_spec(m), _resident((1, c))]
        args += [vv, x0, bias]
    kern = functools.partial(_dft_outer_kernel, nb=nb, c=c, per_group=per_group,
                             hyena=hyena_args is not None)
    return pl.pallas_call(
        kern,
        grid=(groups // nb, p),
        in_specs=in_specs,
        out_specs=col_spec(m),
        out_shape=jax.ShapeDtypeStruct((p, m, ncols), out_dtype),
        compiler_params=_params("parallel", "parallel"),
        name="dft_outer",
    )(*args)


def _dft_strided_kernel(*refs, nb, hyena):
    if hyena:
        t_ref, z_ref, vv_ref, x0_ref, bias_ref, o_ref, zs_ref, os_ref, vs_ref, xs_ref = refs
    else:
        t_ref, z_ref, o_ref, zs_ref, os_ref = refs
    k, cb = z_ref.shape[1], z_ref.shape[3]
    m = o_ref.shape[1]
    nl = cb // LANES
    ng = nb // SUBLANES
    lane = lambda t: slice(t * LANES, (t + 1) * LANES)

    def stage(dst_ref, val):
        for t in range(nl):
            for g in range(ng):
                piece = val[:, g * SUBLANES:(g + 1) * SUBLANES, lane(t)]
                dst_ref[t, g] = piece.reshape(val.shape[0] * SUBLANES, LANES)

    def rows(src_ref, j, n):
        g, r = divmod(j, SUBLANES)
        return jnp.concatenate([src_ref[t, g, pl.ds(r, n, stride=SUBLANES), :]
                                for t in range(nl)], axis=1)

    stage(zs_ref, z_ref[0].astype(F32))
    if hyena:
        stage(vs_ref, vv_ref[0])
        stage(xs_ref, x0_ref[0])
    for j in range(nb):
        y = _dot(t_ref[j], rows(zs_ref, j, k).astype(BF16))
        if hyena:
            y = (y + rows(vs_ref, j, m) * bias_ref[...]) * rows(xs_ref, j, m)
        g, r = divmod(j, SUBLANES)
        for t in range(nl):
            os_ref[t, g, pl.ds(r, m, stride=SUBLANES), :] = y[:, lane(t)]
    o_ref[0] = jnp.concatenate(
        [jnp.concatenate([os_ref[t, g].reshape(m, SUBLANES, LANES) for g in range(ng)], axis=1)
         for t in range(nl)], axis=2).astype(o_ref.dtype)


def _dft_outer_strided(table, z, out_dtype, *, cb, hyena_args=None):
    p, k, n2, c = z.shape
    _, m, _ = table.shape
    nb = 16
    blk = lambda rows: pl.BlockSpec((1, rows, nb, cb), lambda g, ci, pi: (pi, 0, g, ci))
    in_specs = [pl.BlockSpec((nb, m, k), lambda g, ci, pi: (g, 0, 0)), blk(k)]
    args = [table, z]
    staging = lambda rows: pltpu.VMEM((cb // LANES, nb // SUBLANES, rows * SUBLANES, LANES), F32)
    scratch = [staging(k), staging(m)]
    if hyena_args is not None:
        vv, x0, bias = hyena_args
        in_specs += [blk(m), blk(m), pl.BlockSpec((1, cb), lambda g, ci, pi: (0, ci))]
        args += [vv, x0, bias]
        scratch += [staging(m), staging(m)]
    kern = functools.partial(_dft_strided_kernel, nb=nb, hyena=hyena_args is not None)
    return pl.pallas_call(
        kern,
        grid=(n2 // nb, c // cb, p),
        in_specs=in_specs,
        out_specs=blk(m),
        out_shape=jax.ShapeDtypeStruct((p, m, n2, c), out_dtype),
        scratch_shapes=scratch,
        compiler_params=_params("parallel", "parallel", "parallel"),
        name="dft_outer_strided",
    )(*args)


def _dft_inner_kernel(*refs, kc, n2, filtered):
    if filtered:
        mf_ref, mi_ref, x_ref, f_ref, o_ref = refs
    else:
        mf_ref, x_ref, o_ref = refs
    for kk in range(kc):
        zin = jnp.concatenate([x_ref[0, 0, kk], x_ref[0, 1, kk]], axis=0)
        y = _dot(mf_ref[...], zin)
        re, im = y[:n2], y[n2:]
        if filtered:
            fr, fi = f_ref[0, 0, kk], f_ref[0, 1, kk]
            prod = jnp.concatenate([re * fr - im * fi, re * fi + im * fr], axis=0)
            y = _dot(mi_ref[...], prod.astype(BF16))
            re, im = y[:n2], y[n2:]
        o_ref[0, 0, kk] = re.astype(o_ref.dtype)
        o_ref[0, 1, kk] = im.astype(o_ref.dtype)


def _dft_inner(m_fwd, x, out_dtype, m_inv=None, spectrum=None):
    p, _, n1, n2, c = x.shape
    kc = 16
    blk = lambda g, pi: (pi, 0, g, 0, 0)
    filtered = spectrum is not None
    in_specs = [_resident(m_fwd.shape)]
    args = [m_fwd]
    if filtered:
        in_specs.append(_resident(m_inv.shape))
        args.append(m_inv)
    in_specs.append(pl.BlockSpec((1, 2, kc, n2, c), blk))
    args.append(x)
    if filtered:
        in_specs.append(pl.BlockSpec((1, 2, kc, n2, c), lambda g, pi: (0, 0, g, 0, 0)))
        args.append(spectrum)
    kern = functools.partial(_dft_inner_kernel, kc=kc, n2=n2, filtered=filtered)
    return pl.pallas_call(
        kern,
        grid=(n1 // kc, p),
        in_specs=in_specs,
        out_specs=pl.BlockSpec((1, 2, kc, n2, c), blk),
        out_shape=jax.ShapeDtypeStruct(x.shape, out_dtype),
        compiler_params=_params("parallel", "parallel"),
        name="dft_inner",
    )(*args)


def _cmul_kernel(x_ref, f_ref, o_ref, *, n):
    xr, xi = x_ref[0, :n], x_ref[0, n:]
    fr, fi = f_ref[0, :n], f_ref[0, n:]
    o_ref[0, :n] = (xr * fr - xi * fi).astype(o_ref.dtype)
    o_ref[0, n:] = (xr * fi + xi * fr).astype(o_ref.dtype)


def _cmul(x, f, out_dtype):
    p, n2x, c = x.shape
    kern = functools.partial(_cmul_kernel, n=n2x // 2)
    return pl.pallas_call(
        kern,
        grid=(p,),
        in_specs=[pl.BlockSpec((1, n2x, c), lambda pi: (pi, 0, 0)),
                  pl.BlockSpec((1, n2x, c), lambda pi: (0, 0, 0))],
        out_specs=pl.BlockSpec((1, n2x, c), lambda pi: (pi, 0, 0)),
        out_shape=jax.ShapeDtypeStruct(x.shape, out_dtype),
        compiler_params=_params("parallel"),
        name="spectrum_product",
    )(x, f)


def _phase_tables(rows_n1, n1_total, n2_total):
    n = n1_total * n2_total
    n2 = jnp.arange(n2_total, dtype=jnp.int32)[:, None, None]
    k1 = jnp.arange(n1_total, dtype=jnp.int32)[None, :, None]
    n1 = jnp.arange(rows_n1, dtype=jnp.int32)[None, None, :]
    ph = ((n2_total * n1 + n2) * k1) % n
    ang = ph.astype(F32) * (2.0 * math.pi / n)
    return jnp.cos(ang), jnp.sin(ang)


def _forward_tables(length, n1_total, n2_total):
    rows = length // n2_total
    cs, sn = _phase_tables(rows, n1_total, n2_total)
    paired = jnp.concatenate([jnp.concatenate([cs, sn], axis=2),
                              jnp.concatenate([-sn, cs], axis=2)], axis=1)
    cs, sn = _phase_tables(n1_total, n1_total, n2_total)
    real = jnp.concatenate([cs, -sn], axis=1)
    return paired.astype(BF16), real.astype(BF16)


def _inverse_table(length, n1_total, n2_total):
    rows = length // n2_total
    cs, sn = _phase_tables(rows, n1_total, n2_total)
    cs = jnp.swapaxes(cs, 1, 2) / (n1_total * n2_total)
    sn = jnp.swapaxes(sn, 1, 2) / (n1_total * n2_total)
    tb = jnp.concatenate([jnp.concatenate([cs, -sn], axis=2),
                          jnp.concatenate([sn, cs], axis=2)], axis=1)
    return tb.astype(BF16)


def _inner_matrices(n2_total):
    idx = np.arange(n2_total)
    ang = 2.0 * np.pi * ((idx[:, None] * idx[None, :]) % n2_total) / n2_total
    cs, sn = np.cos(ang), np.sin(ang)
    fwd = np.block([[cs, sn], [-sn, cs]])
    inv = np.block([[cs, -sn], [sn, cs]])
    return jnp.asarray(fwd, dtype=BF16), jnp.asarray(inv, dtype=BF16)


def _long_conv_gate(vv, x0, filt, bias):
    p, _, length, c = vv.shape
    n = 2 * length
    n1_total = DFT_N1 if n % DFT_N1 == 0 and n // DFT_N1 >= 8 else n
    n2_total = n // n1_total
    rows = length // n2_total
    t_pair, t_real = _forward_tables(length, n1_total, n2_total)
    t_inv = _inverse_table(length, n1_total, n2_total)

    bias2 = bias.reshape(1, c)
    if n2_total > 1:
        m_fwd, m_inv = _inner_matrices(n2_total)
        shape5 = lambda a: a.reshape(a.shape[0], 2, n1_total, n2_total, c)
        vv4 = vv.reshape(p, 2 * rows, n2_total, c)
        x04 = x0.reshape(p, 2 * rows, n2_total, c)
        spec = _dft_outer_strided(t_real, filt.reshape(1, n1_total, n2_total, c), BF16, cb=c)
        spec = _dft_inner(m_fwd, shape5(spec), F32)
        y = _dft_outer_strided(t_pair, vv4, BF16, cb=c)
        y = _dft_inner(m_fwd, shape5(y), BF16, m_inv=m_inv, spectrum=spec)
        y = y.reshape(p, 2 * n1_total, n2_total, c)
        out = _dft_outer_strided(t_inv, y, BF16, cb=c // 2, hyena_args=(vv4, x04, bias2))
    else:
        vv2 = vv.reshape(p, 2 * rows, c)
        x02 = x0.reshape(p, 2 * rows, c)
        spec = _dft_outer(t_real, filt.reshape(1, n1_total, c), F32, c=c)
        y = _dft_outer(t_pair, vv2, F32, c=c)
        y = _cmul(y, spec, BF16)
        out = _dft_outer(t_inv, y, BF16, c=c, hyena_args=(vv2, x02, bias2))
    return out.reshape(p, 2, length, c)


def _merge_kernel(x_ref, attn_ref, hy_ref, g_ref, gate_ref, woa_ref, woh_ref, wout_ref, o_ref, *, d):
    g = jax.nn.sigmoid(g_ref[0].astype(F32))
    merged = (g[:, :d] * _dot(attn_ref[0], woa_ref[...])
              + g[:, d:] * _dot(hy_ref[0, 0], woh_ref[...]))
    o_ref[0] = x_ref[0] + gate_ref[0] * _dot(merged.astype(BF16), wout_ref[...])


def _merge(x, attn, hyo, g, gate, woa, woh, wout, *, tm):
    b, length, d = x.shape
    p = b // 2
    c = hyo.shape[-1]
    row = lambda bi, i: (bi, i, 0)
    kern = functools.partial(_merge_kernel, d=d)
    return pl.pallas_call(
        kern,
        grid=(b, length // tm),
        in_specs=[pl.BlockSpec((1, tm, d), row),
                  pl.BlockSpec((1, tm, attn.shape[-1]), row),
                  pl.BlockSpec((1, 1, tm, c), lambda bi, i: (bi % p, bi // p, i, 0)),
                  pl.BlockSpec((1, tm, g.shape[-1]), row),
                  pl.BlockSpec((1, 1, d), lambda bi, i: (bi, 0, 0)),
                  _resident(woa.shape), _resident(woh.shape), _resident(wout.shape)],
        out_specs=pl.BlockSpec((1, tm, d), row),
        out_shape=jax.ShapeDtypeStruct(x.shape, F32),
        compiler_params=_params("parallel", "parallel"),
        name="merge",
    )(x, attn, hyo, g, gate, woa, woh, wout)


def _ffn_kernel(x_ref, shift_ref, scale_ref, gate_ref, gain_ref, wgu_ref, wd_ref, fin_ref, o_ref,
                *, d_ff, chunk, final_norm):
    x = x_ref[0]
    h = ((_rms(x) * gain_ref[...]) * (1.0 + scale_ref[0]) + shift_ref[0]).astype(BF16)
    acc = jnp.zeros(x.shape, F32)
    for c0 in range(0, d_ff, chunk):
        c1 = min(c0 + chunk, d_ff)
        gt = _dot(h, wgu_ref[:, c0:c1])
        up = _dot(h, wgu_ref[:, d_ff + c0:d_ff + c1])
        act = (gt * jax.nn.sigmoid(gt) * up).astype(BF16)
        acc = acc + _dot(act, wd_ref[c0:c1, :])
    y = x + gate_ref[0] * acc
    if final_norm:
        y = _rms(y) * fin_ref[...]
    o_ref[0] = y


def _ffn(x, shift, scale, gate, gain, wgu, wd, fin_gain, *, tm, final_norm):
    b, length, d = x.shape
    d_ff = wd.shape[0]
    row = lambda bi, i: (bi, i, 0)
    per_b = lambda bi, i: (bi, 0, 0)
    kern = functools.partial(_ffn_kernel, d_ff=d_ff, chunk=256, final_norm=final_norm)
    return pl.pallas_call(
        kern,
        grid=(b, length // tm),
        in_specs=[pl.BlockSpec((1, tm, d), row),
                  pl.BlockSpec((1, 1, d), per_b),
                  pl.BlockSpec((1, 1, d), per_b),
                  pl.BlockSpec((1, 1, d), per_b),
                  _resident((1, d)),
                  _resident(wgu.shape), _resident(wd.shape),
                  _resident((1, d))],
        out_specs=pl.BlockSpec((1, tm, d), row),
        out_shape=jax.ShapeDtypeStruct(x.shape, F32),
        compiler_params=_params("parallel", "parallel"),
        name="ffn",
    )(x, shift, scale, gate, gain, wgu, wd, fin_gain)


def _rope_tables(length):
    rows = length // GRID_W
    axis_dim = HEAD_DIM // 2
    row = jnp.repeat(jnp.arange(rows, dtype=F32), GRID_W)
    col = jnp.tile(jnp.arange(GRID_W, dtype=F32), rows)
    inv_freq = ROPE_THETA ** (-jnp.arange(0, axis_dim, 2, dtype=F32) / axis_dim)
    ang = jnp.concatenate([row[:, None] * inv_freq, col[:, None] * inv_freq], axis=-1)
    cs, sn = jnp.cos(ang), jnp.sin(ang)
    return jnp.concatenate([cs, cs], axis=-1), jnp.concatenate([-sn, sn], axis=-1)


def _head_perm(n_heads):
    within = np.concatenate([np.arange(0, HEAD_DIM, 2), np.arange(1, HEAD_DIM, 2)])
    return np.concatenate([h * HEAD_DIM + within for h in range(n_heads)]), within


def kernel(x, c, ctx, c_ctx, w_mod, b_mod, norm_mix, w_in, q_norm, k_norm, conv_w, conv_b, filt_w1, filt_b1, filt_w2, filt_b2, filt_w3, filt_b3, filt_w4, filt_freq, hyena_bias, w_o_attn, w_o_hyena, w_out, norm_ffn, w_gate_up, w_down, norm_final):
    b, length, d = x.shape
    ctx_len = ctx.shape[1]
    depth = w_mod.shape[0]
    d_attn = N_HEADS * HEAD_DIM
    d_kv = N_KV_HEADS * HEAD_DIM
    d_hy = conv_w.shape[-1]
    c_hy = d_hy // 3
    assert b % 2 == 0 and length % 512 == 0 and ctx_len % 256 == 0

    perm, within = _head_perm(N_HEADS + N_KV_HEADS)
    col_order = np.concatenate([perm, np.arange(d_attn + d_kv, w_in.shape[-1])])

    cosf, sinf = _rope_tables(length)
    cos_ctx = jnp.ones((ctx_len, HEAD_DIM), F32)
    sin_ctx = jnp.zeros((ctx_len, HEAD_DIM), F32)

    m_rows = 16
    c_rows = jnp.zeros((m_rows, d), F32).at[:b].set(c).at[b].set(c_ctx)
    tm = 512
    tm_ctx = ctx_len
    fin = norm_final.reshape(1, d)

    for layer in range(depth):
        last = layer == depth - 1
        mod_all = _modulation(c_rows, w_mod[layer], b_mod[layer])
        mods = [m.reshape(b, 1, d) for m in jnp.split(mod_all[:b], N_MOD, axis=-1)]
        mods_c = [jnp.broadcast_to(m.reshape(1, 1, d), (b, 1, d))
                  for m in jnp.split(mod_all[b], N_MOD, axis=-1)]
        shift1, scale1, gate1, shift2, scale2, gate2 = mods
        c_shift1, c_scale1, c_gate1, c_shift2, c_scale2, c_gate2 = mods_c

        w_in_bf = w_in[layer][:, col_order].astype(BF16)
        qg = (q_norm[layer][within] * (HEAD_DIM ** -0.5 * math.log2(math.e))).reshape(1, HEAD_DIM)
        kg = k_norm[layer][within].reshape(1, HEAD_DIM)
        gain_mix = norm_mix[layer].reshape(1, d)
        gain_ffn = norm_ffn[layer].reshape(1, d)
        woa = w_o_attn[layer].astype(BF16)
        woh = w_o_hyena[layer].astype(BF16)
        wout = w_out[layer].astype(BF16)
        wgu = w_gate_up[layer].astype(BF16)
        wd = w_down[layer].astype(BF16)
        filt = (filt_w1[layer], filt_b1[layer], filt_w2[layer], filt_b2[layer],
                filt_w3[layer], filt_b3[layer], filt_w4[layer], filt_freq[layer])
        proj = functools.partial(_in_projection, gain=gain_mix, w_bf=w_in_bf, qg=qg, kg=kg,
                                 conv_w=conv_w[layer], conv_b=conv_b[layer],
                                 d_attn=d_attn, d_kv=d_kv)

        qc, kc, vc, gc, vv_c, x0_c = proj(ctx, c_shift1, c_scale1, cosf=cos_ctx, sinf=sin_ctx,
                                          tm=tm_ctx)
        q, k, v, g, vv, x0 = proj(x, shift1, scale1, cosf=cosf, sinf=sinf, tm=tm)

        attn = _attention(q, k, v, kc, vc,
                          tq=256, tk=768 if (ctx_len + length) % 768 == 0 else 256)
        hyo = _long_conv_gate(vv, x0, _hyena_filter(length, *filt, c_hy), hyena_bias[layer])
        x_mid = _merge(x, attn, hyo, g, gate1, woa, woh, wout, tm=tm)

        if not last:
            attn_c = _attention(qc, kc, vc, tq=ctx_len, tk=ctx_len)
            hyo_c = _long_conv_gate(vv_c, x0_c, _hyena_filter(ctx_len, *filt, c_hy),
                                    hyena_bias[layer])
            ctx = _merge(ctx, attn_c, hyo_c, gc, c_gate1, woa, woh, wout, tm=tm_ctx)
            ctx = _ffn(ctx, c_shift2, c_scale2, c_gate2, gain_ffn, wgu, wd, fin,
                       tm=tm_ctx, final_norm=False)

        x = _ffn(x_mid, shift2, scale2, gate2, gain_ffn, wgu, wd, fin, tm=2 * tm, final_norm=last)
    return x
```

```python
import functools
import math

import numpy as np
import jax
import jax.numpy as jnp
from jax import lax
from jax.experimental import pallas as pl
from jax.experimental.pallas import tpu as pltpu

F32 = jnp.float32
BF16 = jnp.bfloat16

GRID_W = 64
N_HEADS = 8
N_KV_HEADS = 2
KV_GROUP = N_HEADS // N_KV_HEADS
HEAD_DIM = 128
ROPE_THETA = 10000.0
FILTER_EMB = 33
FILTER_BANDS = (FILTER_EMB - 1) // 2
DECAY_TARGET = 1e-2
FAST_DECAY_PCT = 0.3
SLOW_DECAY_PCT = 1.5
N_MOD = 6
EPS = 1e-6

LANES = 128
SUBLANES = 8
VMEM_LIMIT = 48 * 1024 * 1024
DFT_N1 = 128

ROW_TILE = 512
FFN_ROW_TILE = 1024
ATTN_Q_TILE = 256
ATTN_KEY_CHUNK = 768
MOD_COL_TILE = 1536


def _params(*sem):
    return pltpu.CompilerParams(dimension_semantics=sem, vmem_limit_bytes=VMEM_LIMIT)


def _resident(shape):
    nd = len(shape)
    return pl.BlockSpec(shape, lambda *_: (0,) * nd, pipeline_mode=pl.Buffered(1))


def _dot(a, b):
    return jnp.dot(a, b, preferred_element_type=F32)


def _split_bf16(a):
    hi = a.astype(BF16)
    lo = (a - hi.astype(F32)).astype(BF16)
    return hi, lo


def _dot3(a, b):
    ah, al = _split_bf16(a)
    bh, bl = _split_bf16(b)
    return _dot(ah, bh) + (_dot(ah, bl) + _dot(al, bh))


def _rms(x):
    return x * lax.rsqrt(jnp.mean(x * x, axis=-1, keepdims=True) + EPS)


def _mod_kernel(c_ref, w_ref, b_ref, o_ref):
    c = c_ref[...]
    s = c * jax.nn.sigmoid(c)
    o_ref[0] = _dot3(s, w_ref[0]) + b_ref[0]


def _modulation(c_rows, w, b):
    m, d = c_rows.shape
    depth, _, n = w.shape
    tn = MOD_COL_TILE
    return pl.pallas_call(
        _mod_kernel,
        grid=(depth, n // tn),
        in_specs=[pl.BlockSpec((m, d), lambda l, j: (0, 0)),
                  pl.BlockSpec((1, d, tn), lambda l, j: (l, 0, j)),
                  pl.BlockSpec((1, 1, tn), lambda l, j: (l, 0, j))],
        out_specs=pl.BlockSpec((1, m, tn), lambda l, j: (l, 0, j)),
        out_shape=jax.ShapeDtypeStruct((depth, m, n), F32),
        compiler_params=_params("parallel", "parallel"),
        name="modulation",
    )(c_rows, w, b.reshape(depth, 1, n))


def _inproj_kernel(x_ref, xp_ref, xn_ref, shift_ref, scale_ref, gain_ref, w_ref, qg_ref, kg_ref,
                   cos_ref, sin_ref, cw_ref, cb_ref,
                   q_ref, k_ref, v_ref, g_ref, vv_ref, x0_ref, *, d_attn, d_kv, d_hy):
    i = pl.program_id(1)
    tm = x_ref.shape[1]
    halo = xp_ref.shape[1]
    x = jnp.concatenate([x_ref[0], xp_ref[0], xn_ref[0]], axis=0)
    h = (_rms(x) * gain_ref[...]) * (1.0 + scale_ref[0]) + shift_ref[0]
    hb_all = h.astype(BF16)
    hb = hb_all[:tm]
    cosf = cos_ref[...]
    sinf = sin_ref[...]

    def head_norm_rope(t, gain):
        t = _rms(t) * gain
        return t * cosf + pltpu.roll(t, HEAD_DIM // 2, axis=1) * sinf

    c0 = 0
    q = _dot(hb, w_ref[:, c0:c0 + d_attn])
    for hd in range(d_attn // HEAD_DIM):
        sl = slice(hd * HEAD_DIM, (hd + 1) * HEAD_DIM)
        q_ref[0, :, sl] = head_norm_rope(q[:, sl], qg_ref[...]).astype(q_ref.dtype)
    c0 += d_attn
    k = _dot(hb, w_ref[:, c0:c0 + d_kv])
    for hd in range(d_kv // HEAD_DIM):
        sl = slice(hd * HEAD_DIM, (hd + 1) * HEAD_DIM)
        k_ref[0, :, sl] = head_norm_rope(k[:, sl], kg_ref[...]).astype(k_ref.dtype)
    c0 += d_kv
    v_ref[0] = _dot(hb, w_ref[:, c0:c0 + d_kv]).astype(v_ref.dtype)
    c0 += d_kv
    u_all = _dot(hb_all, w_ref[:, c0:c0 + d_hy])
    c0 += d_hy
    g_ref[0] = _dot(hb, w_ref[:, c0:]).astype(g_ref.dtype)

    u = u_all[:tm]
    prev_row = jnp.where(i == 0, 0.0, u_all[tm + halo - 1:tm + halo])
    next_row = jnp.where(i == pl.num_programs(1) - 1, 0.0, u_all[tm + halo:tm + halo + 1])
    row = lax.broadcasted_iota(jnp.int32, (tm, 1), 0)
    up = jnp.where(row == 0, prev_row, pltpu.roll(u, 1, axis=0))
    un = jnp.where(row == tm - 1, next_row, pltpu.roll(u, tm - 1, axis=0))
    y = up * cw_ref[0:1, :] + u * cw_ref[1:2, :] + un * cw_ref[2:3, :] + cb_ref[...]
    c = d_hy // 3
    x0_ref[0, 0] = y[:, :c].astype(x0_ref.dtype)
    vv_ref[0, 0] = (y[:, 2 * c:] * y[:, c:2 * c]).astype(vv_ref.dtype)


def _in_projection(x, shift, scale, gain, w_bf, qg, kg, cosf, sinf, conv_w, conv_b,
                   *, tm, d_attn, d_kv):
    b, length, d = x.shape
    d_hy = conv_w.shape[-1]
    c = d_hy // 3
    p = b // 2
    d_gate = w_bf.shape[1] - d_attn - 2 * d_kv - d_hy
    halo = SUBLANES
    n_halo = length // halo
    per = tm // halo
    row = lambda bi, i: (bi, i, 0)
    per_b = lambda bi, i: (bi, 0, 0)
    pair_major = lambda bi, i: (bi % p, bi // p, i, 0)
    kern = functools.partial(_inproj_kernel, d_attn=d_attn, d_kv=d_kv, d_hy=d_hy)
    return pl.pallas_call(
        kern,
        grid=(b, length // tm),
        in_specs=[pl.BlockSpec((1, tm, d), row),
                  pl.BlockSpec((1, halo, d), lambda bi, i: (bi, jnp.maximum(i * per - 1, 0), 0)),
                  pl.BlockSpec((1, halo, d),
                               lambda bi, i: (bi, jnp.minimum((i + 1) * per, n_halo - 1), 0)),
                  pl.BlockSpec((1, 1, d), per_b),
                  pl.BlockSpec((1, 1, d), per_b),
                  _resident((1, d)),
                  _resident(w_bf.shape),
                  _resident((1, HEAD_DIM)),
                  _resident((1, HEAD_DIM)),
                  pl.BlockSpec((tm, HEAD_DIM), lambda bi, i: (i, 0)),
                  pl.BlockSpec((tm, HEAD_DIM), lambda bi, i: (i, 0)),
                  _resident((3, d_hy)),
                  _resident((1, d_hy))],
        out_specs=[pl.BlockSpec((1, tm, d_attn), row),
                   pl.BlockSpec((1, tm, d_kv), row),
                   pl.BlockSpec((1, tm, d_kv), row),
                   pl.BlockSpec((1, tm, d_gate), row),
                   pl.BlockSpec((1, 1, tm, c), pair_major),
                   pl.BlockSpec((1, 1, tm, c), pair_major)],
        out_shape=[jax.ShapeDtypeStruct((b, length, d_attn), BF16),
                   jax.ShapeDtypeStruct((b, length, d_kv), BF16),
                   jax.ShapeDtypeStruct((b, length, d_kv), BF16),
                   jax.ShapeDtypeStruct((b, length, d_gate), BF16),
                   jax.ShapeDtypeStruct((p, 2, length, c), BF16),
                   jax.ShapeDtypeStruct((p, 2, length, c), BF16)],
        compiler_params=_params("parallel", "parallel"),
        name="in_projection",
    )(x, x, x, shift, scale, gain, w_bf, qg, kg, cosf, sinf, conv_w, conv_b.reshape(1, d_hy))


def _flash_kernel(*refs, tq, tk, nk, extra):
    if extra:
        (q_ref, qn_ref, k_ref, v_ref, ke_ref, ve_ref, o_ref,
         kall_ref, vall_ref, qs_ref, s_ref, m_ref, acc_ref) = refs
    else:
        q_ref, qn_ref, k_ref, v_ref, o_ref, kall_ref, vall_ref, qs_ref, s_ref, m_ref, acc_ref = refs
    first_slot = 2
    slot_of = lambda j: first_slot if j == 0 else (j - 1) % 2

    def stack_heads(src_ref, which):
        for g in range(KV_GROUP):
            qs_ref[which, g * tq:(g + 1) * tq, :] = src_ref[0, :, g * HEAD_DIM:(g + 1) * HEAD_DIM]

    def chunk(ref, j):
        return ref[j * tk:(j + 1) * tk, :]

    def scores(j, slot, which=0):
        s_ref[slot] = lax.dot_general(qs_ref[which], chunk(kall_ref, j), (((1,), (1,)), ((), ())),
                                      preferred_element_type=F32)

    stack_heads(q_ref, 0)
    stack_heads(qn_ref, 1)

    @pl.when(pl.program_id(2) == 0)
    def _():
        n_main = k_ref.shape[1]
        kall_ref[0:n_main, :] = k_ref[0]
        vall_ref[0:n_main, 0:HEAD_DIM] = v_ref[0]
        if extra:
            kall_ref[n_main:, :] = ke_ref[0]
            vall_ref[n_main:, 0:HEAD_DIM] = ve_ref[0]
        vall_ref[:, HEAD_DIM:] = jnp.ones((vall_ref.shape[0], HEAD_DIM), vall_ref.dtype)
        scores(0, first_slot)

    m_ref[...] = jnp.full(m_ref.shape, -1e30, F32)
    acc_ref[...] = jnp.zeros(acc_ref.shape, F32)
    nt = tk // LANES

    def consume(j, slot):
        tiles = [s_ref[slot, :, t * LANES:(t + 1) * LANES] for t in range(nt)]
        mt = functools.reduce(jnp.maximum, tiles)
        m_prev = m_ref[...]
        m_new = jnp.maximum(m_prev, jnp.max(mt, axis=-1, keepdims=True))
        alpha = jnp.exp2(m_prev - m_new)
        p = jnp.concatenate([jnp.exp2(t - m_new).astype(BF16) for t in tiles], axis=1)
        pv = _dot(p, chunk(vall_ref, j))
        acc_ref[...] = jnp.concatenate([alpha, alpha], axis=1) * acc_ref[...] + pv
        m_ref[...] = m_new

    for j in range(nk - 1):
        scores(j + 1, slot_of(j + 1))
        consume(j, slot_of(j))
    if nk == 1:
        consume(0, first_slot)
        scores(0, first_slot, which=1)
    else:
        scores(0, first_slot, which=1)
        consume(nk - 1, slot_of(nk - 1))

    acc = acc_ref[...]
    out = acc[:, :HEAD_DIM] / acc[:, HEAD_DIM:]
    for g in range(KV_GROUP):
        o_ref[0, :, g * HEAD_DIM:(g + 1) * HEAD_DIM] = out[g * tq:(g + 1) * tq].astype(o_ref.dtype)


def _attention(q, k, v, k_extra=None, v_extra=None, *, tq, tk):
    b, lq, _ = q.shape
    extra = k_extra is not None
    lk = k.shape[1] + (k_extra.shape[1] if extra else 0)
    gw = KV_GROUP * HEAD_DIM
    rows = KV_GROUP * tq
    assert lk % tk == 0
    nq = lq // tq
    kv_spec = lambda a: pl.BlockSpec((1, a.shape[1], HEAD_DIM), lambda bi, h, i: (bi, 0, h))
    kv_args = [k, v] + ([k_extra, v_extra] if extra else [])
    kern = functools.partial(_flash_kernel, tq=tq, tk=tk, nk=lk // tk, extra=extra)
    return pl.pallas_call(
        kern,
        grid=(b, N_KV_HEADS, nq),
        in_specs=[pl.BlockSpec((1, tq, gw), lambda bi, h, i: (bi, i, h)),
                  pl.BlockSpec((1, tq, gw), lambda bi, h, i: (bi, jnp.minimum(i + 1, nq - 1), h))]
                 + [kv_spec(a) for a in kv_args],
        out_specs=pl.BlockSpec((1, tq, gw), lambda bi, h, i: (bi, i, h)),
        out_shape=jax.ShapeDtypeStruct(q.shape, BF16),
        scratch_shapes=[pltpu.VMEM((lk, HEAD_DIM), BF16),
                        pltpu.VMEM((lk, 2 * HEAD_DIM), BF16),
                        pltpu.VMEM((2, rows, HEAD_DIM), BF16),
                        pltpu.VMEM((3, rows, tk), F32),
                        pltpu.VMEM((rows, LANES), F32),
                        pltpu.VMEM((rows, 2 * HEAD_DIM), F32)],
        compiler_params=_params("arbitrary", "arbitrary", "arbitrary"),
        name="attention",
    )(q, q, *kv_args)


def _filter_kernel(z_ref, w1_ref, b1_ref, w2_ref, b2_ref, w3_ref, b3_ref, w4_ref, fr_ref, dec_ref,
                   o_ref):
    c = o_ref.shape[-1]
    h = jnp.sin(fr_ref[0:1, :] * (_dot3(z_ref[...], w1_ref[...]) + b1_ref[...]))
    h = jnp.sin(fr_ref[1:2, :] * (_dot3(h, w2_ref[...]) + b2_ref[...]))
    h = jnp.sin(fr_ref[2:3, :] * (_dot3(h, w3_ref[...]) + b3_ref[...]))
    out = _dot3(h, w4_ref[...])
    o_ref[0] = out[:, :c] * dec_ref[0]
    o_ref[1] = out[:, c:] * dec_ref[1]


def _pad_to(a, shape):
    return jnp.pad(a, [(0, s - d) for d, s in zip(a.shape, shape)])


def _block_diag2(a, b, shape):
    za = jnp.zeros(shape, a.dtype)
    return jnp.concatenate([jnp.concatenate([_pad_to(a, shape), za], axis=1),
                            jnp.concatenate([za, _pad_to(b, shape)], axis=1)], axis=0)


def _hyena_filter(length, fw1, fb1, fw2, fb2, fw3, fb3, fw4, freq, c):
    half = LANES // 2
    assert fw1.shape[0] <= half and fw1.shape[1] <= half
    rev = lambda a: jnp.concatenate([a[:1], a[:0:-1]])
    t = jnp.linspace(0.0, 1.0, length, dtype=F32)
    w = (2.0 * math.pi / length) * jnp.arange(length, dtype=F32)
    f = jnp.linspace(1e-4, FILTER_BANDS - 1, FILTER_BANDS, dtype=F32)[None, :]

    def features(tt, ww):
        tt, ww = tt[:, None], ww[:, None]
        return _pad_to(jnp.concatenate([tt, jnp.cos(f * ww), -jnp.sin(f * ww)], axis=-1),
                       (length, half))

    z = jnp.concatenate([features(t, w), features(rev(t), rev(w))], axis=1)
    max_decay = math.log(DECAY_TARGET) / FAST_DECAY_PCT
    min_decay = math.log(DECAY_TARGET) / SLOW_DECAY_PCT
    deltas = jnp.abs(jnp.linspace(min_decay, max_decay, c, dtype=F32))
    not_first = (jnp.arange(length) > 0).astype(F32)[:, None]
    decay = jnp.stack([jnp.exp(-t[:, None] * deltas),
                       jnp.exp(-rev(t)[:, None] * deltas) * not_first])

    sq = (half, half)
    w1 = _block_diag2(fw1, fw1, sq)
    w2 = _block_diag2(fw2, fw2, sq)
    w3 = _block_diag2(fw3, fw3, sq)
    w4 = _block_diag2(fw4[:, :c], fw4[:, c:], (half, c))
    twice = lambda v: jnp.tile(_pad_to(v.reshape(-1, v.shape[-1]), (v.size // v.shape[-1], half)),
                               (1, 2))
    b1, b2, b3, fr = twice(fb1), twice(fb2), twice(fb3), twice(freq)
    tl = min(length, 1024)
    out = pl.pallas_call(
        _filter_kernel,
        grid=(length // tl,),
        in_specs=[pl.BlockSpec((tl, LANES), lambda i: (i, 0)),
                  _resident(w1.shape), _resident(b1.shape),
                  _resident(w2.shape), _resident(b2.shape),
                  _resident(w3.shape), _resident(b3.shape),
                  _resident(w4.shape), _resident(fr.shape),
                  pl.BlockSpec((2, tl, c), lambda i: (0, i, 0))],
        out_specs=pl.BlockSpec((2, tl, c), lambda i: (0, i, 0)),
        out_shape=jax.ShapeDtypeStruct((2, length, c), F32),
        compiler_params=_params("parallel"),
        name="hyena_filter",
    )(z, w1, b1, w2, b2, w3, b3, w4, fr, decay)
    return out.reshape(2 * length, c)


def _dft_outer_kernel(*refs, nb, c, per_group, hyena):
    if hyena:
        t_ref, z_ref, vv_ref, x0_ref, bias_ref, o_ref = refs
    else:
        t_ref, z_ref, o_ref = refs
    for j in range(nb):
        cols = slice(j * c, (j + 1) * c)
        y = _dot(t_ref[j if per_group else 0], z_ref[0, :, cols].astype(BF16))
        if hyena:
            y = (y + vv_ref[0, :, cols] * bias_ref[...]) * x0_ref[0, :, cols]
        o_ref[0, :, cols] = y.astype(o_ref.dtype)


def _dft_outer(table, z, out_dtype, *, c, hyena_args=None):
    p, k, ncols = z.shape
    nt, m, _ = table.shape
    groups = ncols // c
    nb = min(groups, 8)
    per_group = nt > 1
    t_spec = (pl.BlockSpec((nb, m, k), lambda g, pi: (g, 0, 0)) if per_group
              else _resident((1, m, k)))
    col_spec = lambda rows: pl.BlockSpec((1, rows, nb * c), lambda g, pi: (pi, 0, g))
    in_specs = [t_spec, col_spec(k)]
    args = [table, z]
    if hyena_args is not None:
        vv, x0, bias = hyena_args
        in_specs += [col_spec(m), col_spec(m), _resident((1, c))]
        args += [vv, x0, bias]
    kern = functools.partial(_dft_outer_kernel, nb=nb, c=c, per_group=per_group,
                             hyena=hyena_args is not None)
    return pl.pallas_call(
        kern,
        grid=(groups // nb, p),
        in_specs=in_specs,
        out_specs=col_spec(m),
        out_shape=jax.ShapeDtypeStruct((p, m, ncols), out_dtype),
        compiler_params=_params("parallel", "parallel"),
        name="dft_outer",
    )(*args)


def _dft_strided_kernel(*refs, nb, hyena):
    if hyena:
        t_ref, z_ref, vv_ref, x0_ref, bias_ref, o_ref, zs_ref, os_ref, vs_ref, xs_ref = refs
    else:
        t_ref, z_ref, o_ref, zs_ref, os_ref = refs
    k, cb = z_ref.shape[1], z_ref.shape[3]
    m = o_ref.shape[1]
    nl = cb // LANES
    ng = nb // SUBLANES
    lane = lambda t: slice(t * LANES, (t + 1) * LANES)

    def stage(dst_ref, val):
        for t in range(nl):
            for g in range(ng):
                piece = val[:, g * SUBLANES:(g + 1) * SUBLANES, lane(t)]
                dst_ref[t, g] = piece.reshape(val.shape[0] * SUBLANES, LANES)

    def rows(src_ref, j, n):
        g, r = divmod(j, SUBLANES)
        return jnp.concatenate([src_ref[t, g, pl.ds(r, n, stride=SUBLANES), :]
                                for t in range(nl)], axis=1)

    stage(zs_ref, z_ref[0].astype(F32))
    if hyena:
        stage(vs_ref, vv_ref[0].astype(F32))
        stage(xs_ref, x0_ref[0].astype(F32))
    for j in range(nb):
        y = _dot(t_ref[j], rows(zs_ref, j, k).astype(BF16))
        if hyena:
            y = (y + rows(vs_ref, j, m) * bias_ref[...]) * rows(xs_ref, j, m)
        g, r = divmod(j, SUBLANES)
        for t in range(nl):
            os_ref[t, g, pl.ds(r, m, stride=SUBLANES), :] = y[:, lane(t)]
    o_ref[0] = jnp.concatenate(
        [jnp.concatenate([os_ref[t, g].reshape(m, SUBLANES, LANES) for g in range(ng)], axis=1)
         for t in range(nl)], axis=2).astype(o_ref.dtype)


def _dft_outer_strided(table, z, out_dtype, *, cb, hyena_args=None):
    p, k, n2, c = z.shape
    _, m, _ = table.shape
    nb = 16
    blk = lambda rows: pl.BlockSpec((1, rows, nb, cb), lambda g, ci, pi: (pi, 0, g, ci))
    in_specs = [pl.BlockSpec((nb, m, k), lambda g, ci, pi: (g, 0, 0)), blk(k)]
    args = [table, z]
    staging = lambda rows: pltpu.VMEM((cb // LANES, nb // SUBLANES, rows * SUBLANES, LANES), F32)
    scratch = [staging(k), staging(m)]
    if hyena_args is not None:
        vv, x0, bias = hyena_args
        in_specs += [blk(m), blk(m), pl.BlockSpec((1, cb), lambda g, ci, pi: (0, ci))]
        args += [vv, x0, bias]
        scratch += [staging(m), staging(m)]
    kern = functools.partial(_dft_strided_kernel, nb=nb, hyena=hyena_args is not None)
    return pl.pallas_call(
        kern,
        grid=(n2 // nb, c // cb, p),
        in_specs=in_specs,
        out_specs=blk(m),
        out_shape=jax.ShapeDtypeStruct((p, m, n2, c), out_dtype),
        scratch_shapes=scratch,
        compiler_params=_params("parallel", "parallel", "parallel"),
        name="dft_outer_strided",
    )(*args)


def _dft_inner_kernel(*refs, kc, n2, filtered):
    if filtered:
        mf_ref, mi_ref, x_ref, f_ref, o_ref = refs
    else:
        mf_ref, x_ref, o_ref = refs
    for kk in range(kc):
        zin = jnp.concatenate([x_ref[0, 0, kk], x_ref[0, 1, kk]], axis=0)
        y = _dot(mf_ref[...], zin)
        re, im = y[:n2], y[n2:]
        if filtered:
            fr, fi = f_ref[0, 0, kk], f_ref[0, 1, kk]
            prod = jnp.concatenate([re * fr - im * fi, re * fi + im * fr], axis=0)
            y = _dot(mi_ref[...], prod.astype(BF16))
            re, im = y[:n2], y[n2:]
        o_ref[0, 0, kk] = re.astype(o_ref.dtype)
        o_ref[0, 1, kk] = im.astype(o_ref.dtype)


def _dft_inner(m_fwd, x, out_dtype, m_inv=None, spectrum=None):
    p, _, n1, n2, c = x.shape
    kc = 16
    blk = lambda g, pi: (pi, 0, g, 0, 0)
    filtered = spectrum is not None
    in_specs = [_resident(m_fwd.shape)]
    args = [m_fwd]
    if filtered:
        in_specs.append(_resident(m_inv.shape))
        args.append(m_inv)
    in_specs.append(pl.BlockSpec((1, 2, kc, n2, c), blk))
    args.append(x)
    if filtered:
        in_specs.append(pl.BlockSpec((1, 2, kc, n2, c), lambda g, pi: (0, 0, g, 0, 0)))
        args.append(spectrum)
    kern = functools.partial(_dft_inner_kernel, kc=kc, n2=n2, filtered=filtered)
    return pl.pallas_call(
        kern,
        grid=(n1 // kc, p),
        in_specs=in_specs,
        out_specs=pl.BlockSpec((1, 2, kc, n2, c), blk),
        out_shape=jax.ShapeDtypeStruct(x.shape, out_dtype),
        compiler_params=_params("parallel", "parallel"),
        name="dft_inner",
    )(*args)


def _cmul_kernel(x_ref, f_ref, o_ref, *, n):
    xr, xi = x_ref[0, :n], x_ref[0, n:]
    fr, fi = f_ref[0, :n], f_ref[0, n:]
    o_ref[0, :n] = (xr * fr - xi * fi).astype(o_ref.dtype)
    o_ref[0, n:] = (xr * fi + xi * fr).astype(o_ref.dtype)


def _cmul(x, f, out_dtype):
    p, n2x, c = x.shape
    kern = functools.partial(_cmul_kernel, n=n2x // 2)
    return pl.pallas_call(
        kern,
        grid=(p,),
        in_specs=[pl.BlockSpec((1, n2x, c), lambda pi: (pi, 0, 0)),
                  pl.BlockSpec((1, n2x, c), lambda pi: (0, 0, 0))],
        out_specs=pl.BlockSpec((1, n2x, c), lambda pi: (pi, 0, 0)),
        out_shape=jax.ShapeDtypeStruct(x.shape, out_dtype),
        compiler_params=_params("parallel"),
        name="spectrum_product",
    )(x, f)


def _phase_tables(rows_n1, n1_total, n2_total):
    n = n1_total * n2_total
    n2 = jnp.arange(n2_total, dtype=jnp.int32)[:, None, None]
    k1 = jnp.arange(n1_total, dtype=jnp.int32)[None, :, None]
    n1 = jnp.arange(rows_n1, dtype=jnp.int32)[None, None, :]
    ph = ((n2_total * n1 + n2) * k1) % n
    ang = ph.astype(F32) * (2.0 * math.pi / n)
    return jnp.cos(ang), jnp.sin(ang)


def _forward_tables(length, n1_total, n2_total):
    rows = length // n2_total
    cs, sn = _phase_tables(rows, n1_total, n2_total)
    paired = jnp.concatenate([jnp.concatenate([cs, sn], axis=2),
                              jnp.concatenate([-sn, cs], axis=2)], axis=1)
    cs, sn = _phase_tables(n1_total, n1_total, n2_total)
    real = jnp.concatenate([cs, -sn], axis=1)
    return paired.astype(BF16), real.astype(BF16)


def _inverse_table(length, n1_total, n2_total):
    rows = length // n2_total
    cs, sn = _phase_tables(rows, n1_total, n2_total)
    cs = jnp.swapaxes(cs, 1, 2) / (n1_total * n2_total)
    sn = jnp.swapaxes(sn, 1, 2) / (n1_total * n2_total)
    tb = jnp.concatenate([jnp.concatenate([cs, -sn], axis=2),
                          jnp.concatenate([sn, cs], axis=2)], axis=1)
    return tb.astype(BF16)


def _inner_matrices(n2_total):
    idx = np.arange(n2_total)
    ang = 2.0 * np.pi * ((idx[:, None] * idx[None, :]) % n2_total) / n2_total
    cs, sn = np.cos(ang), np.sin(ang)
    fwd = np.block([[cs, sn], [-sn, cs]])
    inv = np.block([[cs, -sn], [sn, cs]])
    return jnp.asarray(fwd, dtype=BF16), jnp.asarray(inv, dtype=BF16)


def _long_conv_gate(vv, x0, filt, bias):
    p, _, length, c = vv.shape
    n = 2 * length
    n1_total = DFT_N1 if n % DFT_N1 == 0 and n // DFT_N1 >= 8 else n
    n2_total = n // n1_total
    rows = length // n2_total
    t_pair, t_real = _forward_tables(length, n1_total, n2_total)
    t_inv = _inverse_table(length, n1_total, n2_total)

    bias2 = bias.reshape(1, c)
    if n2_total > 1:
        m_fwd, m_inv = _inner_matrices(n2_total)
        shape5 = lambda a: a.reshape(a.shape[0], 2, n1_total, n2_total, c)
        vv4 = vv.reshape(p, 2 * rows, n2_total, c)
        x04 = x0.reshape(p, 2 * rows, n2_total, c)
        spec = _dft_outer_strided(t_real, filt.reshape(1, n1_total, n2_total, c), BF16, cb=c)
        spec = _dft_inner(m_fwd, shape5(spec), F32)
        y = _dft_outer_strided(t_pair, vv4, BF16, cb=c)
        y = _dft_inner(m_fwd, shape5(y), BF16, m_inv=m_inv, spectrum=spec)
        y = y.reshape(p, 2 * n1_total, n2_total, c)
        out = _dft_outer_strided(t_inv, y, BF16, cb=c // 2, hyena_args=(vv4, x04, bias2))
    else:
        vv2 = vv.reshape(p, 2 * rows, c)
        x02 = x0.reshape(p, 2 * rows, c)
        spec = _dft_outer(t_real, filt.reshape(1, n1_total, c), F32, c=c)
        y = _dft_outer(t_pair, vv2, F32, c=c)
        y = _cmul(y, spec, BF16)
        out = _dft_outer(t_inv, y, BF16, c=c, hyena_args=(vv2, x02, bias2))
    return out.reshape(p, 2, length, c)


def _merge_kernel(x_ref, attn_ref, hy_ref, g_ref, gate_ref, woa_ref, woh_ref, wout_ref, o_ref, *, d):
    g = jax.nn.sigmoid(g_ref[0].astype(F32))
    merged = (g[:, :d] * _dot(attn_ref[0], woa_ref[...])
              + g[:, d:] * _dot(hy_ref[0, 0], woh_ref[...]))
    o_ref[0] = x_ref[0] + gate_ref[0] * _dot(merged.astype(BF16), wout_ref[...])


def _merge(x, attn, hyo, g, gate, woa, woh, wout, *, tm):
    b, length, d = x.shape
    p = b // 2
    c = hyo.shape[-1]
    row = lambda bi, i: (bi, i, 0)
    kern = functools.partial(_merge_kernel, d=d)
    return pl.pallas_call(
        kern,
        grid=(b, length // tm),
        in_specs=[pl.BlockSpec((1, tm, d), row),
                  pl.BlockSpec((1, tm, attn.shape[-1]), row),
                  pl.BlockSpec((1, 1, tm, c), lambda bi, i: (bi % p, bi // p, i, 0)),
                  pl.BlockSpec((1, tm, g.shape[-1]), row),
                  pl.BlockSpec((1, 1, d), lambda bi, i: (bi, 0, 0)),
                  _resident(woa.shape), _resident(woh.shape), _resident(wout.shape)],
        out_specs=pl.BlockSpec((1, tm, d), row),
        out_shape=jax.ShapeDtypeStruct(x.shape, F32),
        compiler_params=_params("parallel", "parallel"),
        name="merge",
    )(x, attn, hyo, g, gate, woa, woh, wout)


def _ffn_kernel(x_ref, shift_ref, scale_ref, gate_ref, gain_ref, wgu_ref, wd_ref, fin_ref, o_ref,
                *, d_ff, chunk, final_norm):
    x = x_ref[0]
    h = ((_rms(x) * gain_ref[...]) * (1.0 + scale_ref[0]) + shift_ref[0]).astype(BF16)
    acc = jnp.zeros(x.shape, F32)
    for c0 in range(0, d_ff, chunk):
        c1 = min(c0 + chunk, d_ff)
        gt = _dot(h, wgu_ref[:, c0:c1])
        up = _dot(h, wgu_ref[:, d_ff + c0:d_ff + c1])
        act = (gt * jax.nn.sigmoid(gt) * up).astype(BF16)
        acc = acc + _dot(act, wd_ref[c0:c1, :])
    y = x + gate_ref[0] * acc
    if final_norm:
        y = _rms(y) * fin_ref[...]
    o_ref[0] = y


def _ffn(x, shift, scale, gate, gain, wgu, wd, fin_gain, *, tm, final_norm):
    b, length, d = x.shape
    d_ff = wd.shape[0]
    row = lambda bi, i: (bi, i, 0)
    per_b = lambda bi, i: (bi, 0, 0)
    kern = functools.partial(_ffn_kernel, d_ff=d_ff, chunk=256, final_norm=final_norm)
    return pl.pallas_call(
        kern,
        grid=(b, length // tm),
        in_specs=[pl.BlockSpec((1, tm, d), row),
                  pl.BlockSpec((1, 1, d), per_b),
                  pl.BlockSpec((1, 1, d), per_b),
                  pl.BlockSpec((1, 1, d), per_b),
                  _resident((1, d)),
                  _resident(wgu.shape), _resident(wd.shape),
                  _resident((1, d))],
        out_specs=pl.BlockSpec((1, tm, d), row),
        out_shape=jax.ShapeDtypeStruct(x.shape, F32),
        compiler_params=_params("parallel", "parallel"),
        name="ffn",
    )(x, shift, scale, gate, gain, wgu, wd, fin_gain)


def _rope_tables(length):
    rows = length // GRID_W
    axis_dim = HEAD_DIM // 2
    row = jnp.repeat(jnp.arange(rows, dtype=F32), GRID_W)
    col = jnp.tile(jnp.arange(GRID_W, dtype=F32), rows)
    inv_freq = ROPE_THETA ** (-jnp.arange(0, axis_dim, 2, dtype=F32) / axis_dim)
    ang = jnp.concatenate([row[:, None] * inv_freq, col[:, None] * inv_freq], axis=-1)
    cs, sn = jnp.cos(ang), jnp.sin(ang)
    return jnp.concatenate([cs, cs], axis=-1), jnp.concatenate([-sn, sn], axis=-1)


def _head_perm(n_heads):
    within = np.concatenate([np.arange(0, HEAD_DIM, 2), np.arange(1, HEAD_DIM, 2)])
    return np.concatenate([h * HEAD_DIM + within for h in range(n_heads)]), within


def kernel(x, c, ctx, c_ctx, w_mod, b_mod, norm_mix, w_in, q_norm, k_norm, conv_w, conv_b, filt_w1, filt_b1, filt_w2, filt_b2, filt_w3, filt_b3, filt_w4, filt_freq, hyena_bias, w_o_attn, w_o_hyena, w_out, norm_ffn, w_gate_up, w_down, norm_final):
    b, length, d = x.shape
    ctx_len = ctx.shape[1]
    depth = w_mod.shape[0]
    d_attn = N_HEADS * HEAD_DIM
    d_kv = N_KV_HEADS * HEAD_DIM
    d_hy = conv_w.shape[-1]
    c_hy = d_hy // 3
    tm = ROW_TILE
    tm_ctx = ctx_len
    key_chunk = ATTN_KEY_CHUNK if (ctx_len + length) % ATTN_KEY_CHUNK == 0 else ATTN_Q_TILE
    assert b % 2 == 0 and length % FFN_ROW_TILE == 0 and ctx_len % ATTN_Q_TILE == 0

    perm, within = _head_perm(N_HEADS + N_KV_HEADS)
    d_qk = d_attn + d_kv

    cosf, sinf = _rope_tables(length)
    cos_ctx = jnp.ones((ctx_len, HEAD_DIM), F32)
    sin_ctx = jnp.zeros((ctx_len, HEAD_DIM), F32)

    m_rows = 2 * SUBLANES
    assert b + 1 <= m_rows
    c_rows = jnp.zeros((m_rows, d), F32).at[:b].set(c).at[b].set(c_ctx)
    mod_layers = _modulation(c_rows, w_mod, b_mod)
    fin = norm_final.reshape(1, d)

    for layer in range(depth):
        last = layer == depth - 1
        mod_all = mod_layers[layer]
        mods = [m.reshape(b, 1, d) for m in jnp.split(mod_all[:b], N_MOD, axis=-1)]
        mods_c = [jnp.broadcast_to(m.reshape(1, 1, d), (b, 1, d))
                  for m in jnp.split(mod_all[b], N_MOD, axis=-1)]
        shift1, scale1, gate1, shift2, scale2, gate2 = mods
        c_shift1, c_scale1, c_gate1, c_shift2, c_scale2, c_gate2 = mods_c

        w_in_bf = jnp.concatenate([w_in[layer][:, :d_qk][:, perm], w_in[layer][:, d_qk:]],
                                  axis=1).astype(BF16)
        qg = (q_norm[layer][within] * (HEAD_DIM ** -0.5 * math.log2(math.e))).reshape(1, HEAD_DIM)
        kg = k_norm[layer][within].reshape(1, HEAD_DIM)
        gain_mix = norm_mix[layer].reshape(1, d)
        gain_ffn = norm_ffn[layer].reshape(1, d)
        woa = w_o_attn[layer].astype(BF16)
        woh = w_o_hyena[layer].astype(BF16)
        wout = w_out[layer].astype(BF16)
        wgu = w_gate_up[layer].astype(BF16)
        wd = w_down[layer].astype(BF16)
        filt = (filt_w1[layer], filt_b1[layer], filt_w2[layer], filt_b2[layer],
                filt_w3[layer], filt_b3[layer], filt_w4[layer], filt_freq[layer])
        proj = functools.partial(_in_projection, gain=gain_mix, w_bf=w_in_bf, qg=qg, kg=kg,
                                 conv_w=conv_w[layer], conv_b=conv_b[layer],
                                 d_attn=d_attn, d_kv=d_kv)

        qc, kc, vc, gc, vv_c, x0_c = proj(ctx, c_shift1, c_scale1, cosf=cos_ctx, sinf=sin_ctx,
                                          tm=tm_ctx)
        q, k, v, g, vv, x0 = proj(x, shift1, scale1, cosf=cosf, sinf=sinf, tm=tm)

        attn = _attention(q, k, v, kc, vc, tq=ATTN_Q_TILE, tk=key_chunk)
        hyo = _long_conv_gate(vv, x0, _hyena_filter(length, *filt, c_hy), hyena_bias[layer])
        x_mid = _merge(x, attn, hyo, g, gate1, woa, woh, wout, tm=tm)

        if not last:
            attn_c = _attention(qc, kc, vc, tq=ctx_len, tk=ctx_len)
            hyo_c = _long_conv_gate(vv_c, x0_c, _hyena_filter(ctx_len, *filt, c_hy),
                                    hyena_bias[layer])
            ctx = _merge(ctx, attn_c, hyo_c, gc, c_gate1, woa, woh, wout, tm=tm_ctx)
            ctx = _ffn(ctx, c_shift2, c_scale2, c_gate2, gain_ffn, wgu, wd, fin,
                       tm=tm_ctx, final_norm=False)

        x = _ffn(x_mid, shift2, scale2, gate2, gain_ffn, wgu, wd, fin, tm=FFN_ROW_TILE,
                 final_norm=last)
    return x
```

```python
import functools
import math

import numpy as np
import jax
import jax.numpy as jnp
from jax import lax
from jax.experimental import pallas as pl
from jax.experimental.pallas import tpu as pltpu

F32 = jnp.float32
BF16 = jnp.bfloat16

GRID_W = 64
N_HEADS = 8
N_KV_HEADS = 2
KV_GROUP = N_HEADS // N_KV_HEADS
HEAD_DIM = 128
ROPE_THETA = 10000.0
FILTER_EMB = 33
FILTER_BANDS = (FILTER_EMB - 1) // 2
DECAY_TARGET = 1e-2
FAST_DECAY_PCT = 0.3
SLOW_DECAY_PCT = 1.5
N_MOD = 6
EPS = 1e-6

LANES = 128
SUBLANES = 8
MXU_TILE = 256
VMEM_LIMIT = 48 * 1024 * 1024
DFT_N1 = 128

ROW_TILE = 512
FFN_ROW_TILE = 1024
ATTN_Q_TILE = 256
ATTN_KEY_CHUNK = 768
MOD_COL_TILE = 1536


def _params(*sem):
    return pltpu.CompilerParams(dimension_semantics=sem, vmem_limit_bytes=VMEM_LIMIT)


def _resident(shape):
    nd = len(shape)
    return pl.BlockSpec(shape, lambda *_: (0,) * nd, pipeline_mode=pl.Buffered(1))


def _dot(a, b):
    return jnp.dot(a, b, preferred_element_type=F32)


def _split_bf16(a):
    hi = a.astype(BF16)
    lo = (a - hi.astype(F32)).astype(BF16)
    return hi, lo


def _dot3(a, b):
    ah, al = _split_bf16(a)
    bh, bl = _split_bf16(b)
    return _dot(ah, bh) + (_dot(ah, bl) + _dot(al, bh))


def _rms(x):
    return x * lax.rsqrt(jnp.mean(x * x, axis=-1, keepdims=True) + EPS)


def _mod_kernel(c_ref, w_ref, b_ref, o_ref):
    c = c_ref[...]
    s = c * jax.nn.sigmoid(c)
    o_ref[0] = _dot3(s, w_ref[0]) + b_ref[0]


def _modulation(c_rows, w, b):
    m, d = c_rows.shape
    depth, _, n = w.shape
    tn = MOD_COL_TILE
    return pl.pallas_call(
        _mod_kernel,
        grid=(depth, n // tn),
        in_specs=[pl.BlockSpec((m, d), lambda l, j: (0, 0)),
                  pl.BlockSpec((1, d, tn), lambda l, j: (l, 0, j)),
                  pl.BlockSpec((1, 1, tn), lambda l, j: (l, 0, j))],
        out_specs=pl.BlockSpec((1, m, tn), lambda l, j: (l, 0, j)),
        out_shape=jax.ShapeDtypeStruct((depth, m, n), F32),
        compiler_params=_params("parallel", "parallel"),
        name="modulation",
    )(c_rows, w, b.reshape(depth, 1, n))


def _inproj_kernel(x_ref, xp_ref, xn_ref, shift_ref, scale_ref, gain_ref, w_ref, qg_ref, kg_ref,
                   cos_ref, sin_ref, cw_ref, cb_ref,
                   q_ref, k_ref, v_ref, g_ref, vv_ref, x0_ref, us_ref, *, d_attn, d_kv, d_hy):
    i = pl.program_id(1)
    tm = x_ref.shape[1]
    halo = xp_ref.shape[1]
    x = jnp.concatenate([x_ref[0], xp_ref[0], xn_ref[0]], axis=0)
    h = (_rms(x) * gain_ref[...]) * (1.0 + scale_ref[0]) + shift_ref[0]
    hb_all = h.astype(BF16)
    hb = hb_all[:tm]
    cosf = cos_ref[...]
    sinf = sin_ref[...]

    def head_norm_rope(t, gain):
        t = _rms(t) * gain
        return t * cosf + pltpu.roll(t, HEAD_DIM // 2, axis=1) * sinf

    c0 = 0
    q = _dot(hb, w_ref[:, c0:c0 + d_attn])
    for hd in range(d_attn // HEAD_DIM):
        sl = slice(hd * HEAD_DIM, (hd + 1) * HEAD_DIM)
        q_ref[0, :, sl] = head_norm_rope(q[:, sl], qg_ref[...]).astype(q_ref.dtype)
    c0 += d_attn
    k = _dot(hb, w_ref[:, c0:c0 + d_kv])
    for hd in range(d_kv // HEAD_DIM):
        sl = slice(hd * HEAD_DIM, (hd + 1) * HEAD_DIM)
        k_ref[0, :, sl] = head_norm_rope(k[:, sl], kg_ref[...]).astype(k_ref.dtype)
    c0 += d_kv
    v_ref[0] = _dot(hb, w_ref[:, c0:c0 + d_kv]).astype(v_ref.dtype)
    c0 += d_kv
    u_all = _dot(hb_all, w_ref[:, c0:c0 + d_hy])
    c0 += d_hy
    g_ref[0] = _dot(hb, w_ref[:, c0:]).astype(g_ref.dtype)

    u = u_all[:tm]
    keep_prev = (i > 0).astype(F32)
    keep_next = (i < pl.num_programs(1) - 1).astype(F32)
    n_slabs = d_hy // LANES
    for t in range(n_slabs):
        lanes = slice(t * LANES, (t + 1) * LANES)
        us_ref[t, 0:halo, :] = u_all[tm:tm + halo, lanes] * keep_prev
        us_ref[t, halo:halo + tm, :] = u_all[:tm, lanes]
        us_ref[t, halo + tm:, :] = u_all[tm + halo:, lanes] * keep_next
    up = jnp.concatenate([us_ref[t, pl.ds(halo - 1, tm), :] for t in range(n_slabs)], axis=1)
    un = jnp.concatenate([us_ref[t, pl.ds(halo + 1, tm), :] for t in range(n_slabs)], axis=1)
    y = up * cw_ref[0:1, :] + u * cw_ref[1:2, :] + un * cw_ref[2:3, :] + cb_ref[...]
    c = d_hy // 3
    x0_ref[0, 0] = y[:, :c].astype(x0_ref.dtype)
    vv_ref[0, 0] = (y[:, 2 * c:] * y[:, c:2 * c]).astype(vv_ref.dtype)


def _in_projection(x, shift, scale, gain, w_bf, qg, kg, cosf, sinf, conv_w, conv_b,
                   *, tm, d_attn, d_kv):
    b, length, d = x.shape
    d_hy = conv_w.shape[-1]
    c = d_hy // 3
    p = b // 2
    d_gate = w_bf.shape[1] - d_attn - 2 * d_kv - d_hy
    halo = SUBLANES
    n_halo = length // halo
    per = tm // halo
    row = lambda bi, i: (bi, i, 0)
    per_b = lambda bi, i: (bi, 0, 0)
    pair_major = lambda bi, i: (bi % p, bi // p, i, 0)
    kern = functools.partial(_inproj_kernel, d_attn=d_attn, d_kv=d_kv, d_hy=d_hy)
    return pl.pallas_call(
        kern,
        grid=(b, length // tm),
        in_specs=[pl.BlockSpec((1, tm, d), row),
                  pl.BlockSpec((1, halo, d), lambda bi, i: (bi, jnp.maximum(i * per - 1, 0), 0)),
                  pl.BlockSpec((1, halo, d),
                               lambda bi, i: (bi, jnp.minimum((i + 1) * per, n_halo - 1), 0)),
                  pl.BlockSpec((1, 1, d), per_b),
                  pl.BlockSpec((1, 1, d), per_b),
                  _resident((1, d)),
                  _resident(w_bf.shape),
                  _resident((1, HEAD_DIM)),
                  _resident((1, HEAD_DIM)),
                  pl.BlockSpec((tm, HEAD_DIM), lambda bi, i: (i, 0)),
                  pl.BlockSpec((tm, HEAD_DIM), lambda bi, i: (i, 0)),
                  _resident((3, d_hy)),
                  _resident((1, d_hy))],
        out_specs=[pl.BlockSpec((1, tm, d_attn), row),
                   pl.BlockSpec((1, tm, d_kv), row),
                   pl.BlockSpec((1, tm, d_kv), row),
                   pl.BlockSpec((1, tm, d_gate), row),
                   pl.BlockSpec((1, 1, tm, c), pair_major),
                   pl.BlockSpec((1, 1, tm, c), pair_major)],
        out_shape=[jax.ShapeDtypeStruct((b, length, d_attn), BF16),
                   jax.ShapeDtypeStruct((b, length, d_kv), BF16),
                   jax.ShapeDtypeStruct((b, length, d_kv), BF16),
                   jax.ShapeDtypeStruct((b, length, d_gate), BF16),
                   jax.ShapeDtypeStruct((p, 2, length, c), BF16),
                   jax.ShapeDtypeStruct((p, 2, length, c), BF16)],
        scratch_shapes=[pltpu.VMEM((d_hy // LANES, tm + 2 * halo, LANES), F32)],
        compiler_params=_params("parallel", "parallel"),
        name="in_projection",
    )(x, x, x, shift, scale, gain, w_bf, qg, kg, cosf, sinf, conv_w, conv_b.reshape(1, d_hy))


def _flash_kernel(*refs, tq, tk, nk, extra):
    if extra:
        (q_ref, qn_ref, k_ref, v_ref, ke_ref, ve_ref, o_ref,
         kall_ref, vall_ref, qs_ref, s_ref, m_ref, acc_ref) = refs
    else:
        q_ref, qn_ref, k_ref, v_ref, o_ref, kall_ref, vall_ref, qs_ref, s_ref, m_ref, acc_ref = refs
    first_slot = 2
    slot_of = lambda j: first_slot if j == 0 else (j - 1) % 2

    def stack_heads(src_ref, which):
        for g in range(KV_GROUP):
            qs_ref[which, g * tq:(g + 1) * tq, :] = src_ref[0, :, g * HEAD_DIM:(g + 1) * HEAD_DIM]

    def chunk(ref, j):
        return ref[j * tk:(j + 1) * tk, :]

    def scores(j, slot, which=0):
        s_ref[slot] = lax.dot_general(qs_ref[which], chunk(kall_ref, j), (((1,), (1,)), ((), ())),
                                      preferred_element_type=F32)

    stack_heads(q_ref, 0)
    stack_heads(qn_ref, 1)

    @pl.when(pl.program_id(2) == 0)
    def _():
        n_main = k_ref.shape[1]
        kall_ref[0:n_main, :] = k_ref[0]
        vall_ref[0:n_main, 0:HEAD_DIM] = v_ref[0]
        if extra:
            kall_ref[n_main:, :] = ke_ref[0]
            vall_ref[n_main:, 0:HEAD_DIM] = ve_ref[0]
        vall_ref[:, HEAD_DIM:] = jnp.ones((vall_ref.shape[0], HEAD_DIM), vall_ref.dtype)
        scores(0, first_slot)

    m_ref[...] = jnp.full(m_ref.shape, -1e30, F32)
    acc_ref[...] = jnp.zeros(acc_ref.shape, F32)
    nt = tk // LANES

    def consume(j, slot):
        tiles = [s_ref[slot, :, t * LANES:(t + 1) * LANES] for t in range(nt)]
        mt = functools.reduce(jnp.maximum, tiles)
        m_prev = m_ref[...]
        m_new = jnp.maximum(m_prev, jnp.max(mt, axis=-1, keepdims=True))
        alpha = jnp.exp2(m_prev - m_new)
        p = jnp.concatenate([jnp.exp2(t - m_new).astype(BF16) for t in tiles], axis=1)
        pv = _dot(p, chunk(vall_ref, j))
        acc_ref[...] = jnp.concatenate([alpha, alpha], axis=1) * acc_ref[...] + pv
        m_ref[...] = m_new

    for j in range(nk - 1):
        scores(j + 1, slot_of(j + 1))
        consume(j, slot_of(j))
    if nk == 1:
        consume(0, first_slot)
        scores(0, first_slot, which=1)
    else:
        scores(0, first_slot, which=1)
        consume(nk - 1, slot_of(nk - 1))

    acc = acc_ref[...]
    out = acc[:, :HEAD_DIM] / acc[:, HEAD_DIM:]
    for g in range(KV_GROUP):
        o_ref[0, :, g * HEAD_DIM:(g + 1) * HEAD_DIM] = out[g * tq:(g + 1) * tq].astype(o_ref.dtype)


def _attention(q, k, v, k_extra=None, v_extra=None, *, tq, tk):
    b, lq, _ = q.shape
    extra = k_extra is not None
    lk = k.shape[1] + (k_extra.shape[1] if extra else 0)
    gw = KV_GROUP * HEAD_DIM
    rows = KV_GROUP * tq
    assert lk % tk == 0
    nq = lq // tq
    kv_spec = lambda a: pl.BlockSpec((1, a.shape[1], HEAD_DIM), lambda bi, h, i: (bi, 0, h))
    kv_args = [k, v] + ([k_extra, v_extra] if extra else [])
    kern = functools.partial(_flash_kernel, tq=tq, tk=tk, nk=lk // tk, extra=extra)
    return pl.pallas_call(
        kern,
        grid=(b, N_KV_HEADS, nq),
        in_specs=[pl.BlockSpec((1, tq, gw), lambda bi, h, i: (bi, i, h)),
                  pl.BlockSpec((1, tq, gw), lambda bi, h, i: (bi, jnp.minimum(i + 1, nq - 1), h))]
                 + [kv_spec(a) for a in kv_args],
        out_specs=pl.BlockSpec((1, tq, gw), lambda bi, h, i: (bi, i, h)),
        out_shape=jax.ShapeDtypeStruct(q.shape, BF16),
        scratch_shapes=[pltpu.VMEM((lk, HEAD_DIM), BF16),
                        pltpu.VMEM((lk, 2 * HEAD_DIM), BF16),
                        pltpu.VMEM((2, rows, HEAD_DIM), BF16),
                        pltpu.VMEM((3, rows, tk), F32),
                        pltpu.VMEM((rows, LANES), F32),
                        pltpu.VMEM((rows, 2 * HEAD_DIM), F32)],
        compiler_params=_params("arbitrary", "arbitrary", "arbitrary"),
        name="attention",
    )(q, q, *kv_args)


def _filter_kernel(z_ref, w1_ref, b1_ref, w2_ref, b2_ref, w3_ref, b3_ref, w4_ref, fr_ref, dec_ref,
                   o_ref):
    c = o_ref.shape[-1]
    h = jnp.sin(fr_ref[0:1, :] * (_dot3(z_ref[...], w1_ref[...]) + b1_ref[...]))
    h = jnp.sin(fr_ref[1:2, :] * (_dot3(h, w2_ref[...]) + b2_ref[...]))
    h = jnp.sin(fr_ref[2:3, :] * (_dot3(h, w3_ref[...]) + b3_ref[...]))
    out = _dot3(h, w4_ref[...])
    o_ref[0] = out[:, :c] * dec_ref[0]
    o_ref[1] = out[:, c:] * dec_ref[1]


def _pad_to(a, shape):
    return jnp.pad(a, [(0, s - d) for d, s in zip(a.shape, shape)])


def _block_diag2(a, b, shape):
    za = jnp.zeros(shape, a.dtype)
    return jnp.concatenate([jnp.concatenate([_pad_to(a, shape), za], axis=1),
                            jnp.concatenate([za, _pad_to(b, shape)], axis=1)], axis=0)


def _hyena_filter(length, fw1, fb1, fw2, fb2, fw3, fb3, fw4, freq, c):
    half = LANES // 2
    assert fw1.shape[0] <= half and fw1.shape[1] <= half
    rev = lambda a: jnp.concatenate([a[:1], a[:0:-1]])
    t = jnp.linspace(0.0, 1.0, length, dtype=F32)
    w = (2.0 * math.pi / length) * jnp.arange(length, dtype=F32)
    f = jnp.linspace(1e-4, FILTER_BANDS - 1, FILTER_BANDS, dtype=F32)[None, :]

    def features(tt, ww):
        tt, ww = tt[:, None], ww[:, None]
        return _pad_to(jnp.concatenate([tt, jnp.cos(f * ww), -jnp.sin(f * ww)], axis=-1),
                       (length, half))

    z = jnp.concatenate([features(t, w), features(rev(t), rev(w))], axis=1)
    max_decay = math.log(DECAY_TARGET) / FAST_DECAY_PCT
    min_decay = math.log(DECAY_TARGET) / SLOW_DECAY_PCT
    deltas = jnp.abs(jnp.linspace(min_decay, max_decay, c, dtype=F32))
    not_first = (jnp.arange(length) > 0).astype(F32)[:, None]
    decay = jnp.stack([jnp.exp(-t[:, None] * deltas),
                       jnp.exp(-rev(t)[:, None] * deltas) * not_first])

    sq = (half, half)
    w1 = _block_diag2(fw1, fw1, sq)
    w2 = _block_diag2(fw2, fw2, sq)
    w3 = _block_diag2(fw3, fw3, sq)
    w4 = _block_diag2(fw4[:, :c], fw4[:, c:], (half, c))
    twice = lambda v: jnp.tile(_pad_to(v.reshape(-1, v.shape[-1]), (v.size // v.shape[-1], half)),
                               (1, 2))
    b1, b2, b3, fr = twice(fb1), twice(fb2), twice(fb3), twice(freq)
    tl = min(length, 1024)
    out = pl.pallas_call(
        _filter_kernel,
        grid=(length // tl,),
        in_specs=[pl.BlockSpec((tl, LANES), lambda i: (i, 0)),
                  _resident(w1.shape), _resident(b1.shape),
                  _resident(w2.shape), _resident(b2.shape),
                  _resident(w3.shape), _resident(b3.shape),
                  _resident(w4.shape), _resident(fr.shape),
                  pl.BlockSpec((2, tl, c), lambda i: (0, i, 0))],
        out_specs=pl.BlockSpec((2, tl, c), lambda i: (0, i, 0)),
        out_shape=jax.ShapeDtypeStruct((2, length, c), F32),
        compiler_params=_params("parallel"),
        name="hyena_filter",
    )(z, w1, b1, w2, b2, w3, b3, w4, fr, decay)
    return out.reshape(2 * length, c)


def _dft_outer_kernel(*refs, nb, c, per_group, hyena):
    if hyena:
        t_ref, z_ref, vv_ref, x0_ref, bias_ref, o_ref = refs
    else:
        t_ref, z_ref, o_ref = refs
    for j in range(nb):
        cols = slice(j * c, (j + 1) * c)
        y = _dot(t_ref[j if per_group else 0], z_ref[0, :, cols].astype(BF16))
        if hyena:
            y = (y + vv_ref[0, :, cols] * bias_ref[...]) * x0_ref[0, :, cols]
        o_ref[0, :, cols] = y.astype(o_ref.dtype)


def _dft_outer(table, z, out_dtype, *, c, hyena_args=None):
    p, k, ncols = z.shape
    nt, m, _ = table.shape
    groups = ncols // c
    nb = min(groups, 8)
    per_group = nt > 1
    t_spec = (pl.BlockSpec((nb, m, k), lambda g, pi: (g, 0, 0)) if per_group
              else _resident((1, m, k)))
    col_spec = lambda rows: pl.BlockSpec((1, rows, nb * c), lambda g, pi: (pi, 0, g))
    in_specs = [t_spec, col_spec(k)]
    args = [table, z]
    if hyena_args is not None:
        vv, x0, bias = hyena_args
        in_specs += [col_spec(m), col_spec(m), _resident((1, c))]
        args += [vv, x0, bias]
    kern = functools.partial(_dft_outer_kernel, nb=nb, c=c, per_group=per_group,
                             hyena=hyena_args is not None)
    return pl.pallas_call(
        kern,
        grid=(groups // nb, p),
        in_specs=in_specs,
        out_specs=col_spec(m),
        out_shape=jax.ShapeDtypeStruct((p, m, ncols), out_dtype),
        compiler_params=_params("parallel", "parallel"),
        name="dft_outer",
    )(*args)


def _dft_strided_kernel(*refs, nb, hyena):
    if hyena:
        t_ref, z_ref, vv_ref, x0_ref, bias_ref, o_ref, zs_ref, os_ref, vs_ref, xs_ref = refs
    else:
        t_ref, z_ref, o_ref, zs_ref, os_ref = refs
    k, cb = z_ref.shape[1], z_ref.shape[3]
    m = o_ref.shape[1]
    nl = cb // LANES
    ng = nb // SUBLANES
    lane = lambda t: slice(t * LANES, (t + 1) * LANES)

    def stage(dst_ref, val):
        for t in range(nl):
            for g in range(ng):
                piece = val[:, g * SUBLANES:(g + 1) * SUBLANES, lane(t)]
                dst_ref[t, g] = piece.reshape(val.shape[0] * SUBLANES, LANES)

    def rows(src_ref, j, n):
        g, r = divmod(j, SUBLANES)
        return jnp.concatenate([src_ref[t, g, pl.ds(r, n, stride=SUBLANES), :]
                                for t in range(nl)], axis=1)

    stage(zs_ref, z_ref[0].astype(F32))
    if hyena:
        stage(vs_ref, vv_ref[0].astype(F32))
        stage(xs_ref, x0_ref[0].astype(F32))
    for j in range(nb):
        y = _dot(t_ref[j], rows(zs_ref, j, k).astype(BF16))
        if hyena:
            y = (y + rows(vs_ref, j, m) * bias_ref[...]) * rows(xs_ref, j, m)
        g, r = divmod(j, SUBLANES)
        for t in range(nl):
            os_ref[t, g, pl.ds(r, m, stride=SUBLANES), :] = y[:, lane(t)]
    o_ref[0] = jnp.concatenate(
        [jnp.concatenate([os_ref[t, g].reshape(m, SUBLANES, LANES) for g in range(ng)], axis=1)
         for t in range(nl)], axis=2).astype(o_ref.dtype)


def _dft_outer_strided(table, z, out_dtype, *, cb, hyena_args=None):
    p, k, n2, c = z.shape
    _, m, _ = table.shape
    nb = 16
    blk = lambda rows: pl.BlockSpec((1, rows, nb, cb), lambda g, ci, pi: (pi, 0, g, ci))
    in_specs = [pl.BlockSpec((nb, m, k), lambda g, ci, pi: (g, 0, 0)), blk(k)]
    args = [table, z]
    staging = lambda rows: pltpu.VMEM((cb // LANES, nb // SUBLANES, rows * SUBLANES, LANES), F32)
    scratch = [staging(k), staging(m)]
    if hyena_args is not None:
        vv, x0, bias = hyena_args
        in_specs += [blk(m), blk(m), pl.BlockSpec((1, cb), lambda g, ci, pi: (0, ci))]
        args += [vv, x0, bias]
        scratch += [staging(m), staging(m)]
    kern = functools.partial(_dft_strided_kernel, nb=nb, hyena=hyena_args is not None)
    return pl.pallas_call(
        kern,
        grid=(n2 // nb, c // cb, p),
        in_specs=in_specs,
        out_specs=blk(m),
        out_shape=jax.ShapeDtypeStruct((p, m, n2, c), out_dtype),
        scratch_shapes=scratch,
        compiler_params=_params("parallel", "parallel", "parallel"),
        name="dft_outer_strided",
    )(*args)


def _dft_inner_kernel(*refs, kc, n2, filtered):
    if filtered:
        mf_ref, mi_ref, x_ref, f_ref, o_ref = refs
    else:
        mf_ref, x_ref, o_ref = refs
    for kk in range(kc):
        zin = jnp.concatenate([x_ref[0, 0, kk], x_ref[0, 1, kk]], axis=0)
        y = _dot(mf_ref[...], zin)
        re, im = y[:n2], y[n2:]
        if filtered:
            fr, fi = f_ref[0, 0, kk], f_ref[0, 1, kk]
            prod = jnp.concatenate([re * fr - im * fi, re * fi + im * fr], axis=0)
            y = _dot(mi_ref[...], prod.astype(BF16))
            re, im = y[:n2], y[n2:]
        o_ref[0, 0, kk] = re.astype(o_ref.dtype)
        o_ref[0, 1, kk] = im.astype(o_ref.dtype)


def _dft_inner(m_fwd, x, out_dtype, m_inv=None, spectrum=None):
    p, _, n1, n2, c = x.shape
    kc = 16
    blk = lambda g, pi: (pi, 0, g, 0, 0)
    filtered = spectrum is not None
    in_specs = [_resident(m_fwd.shape)]
    args = [m_fwd]
    if filtered:
        in_specs.append(_resident(m_inv.shape))
        args.append(m_inv)
    in_specs.append(pl.BlockSpec((1, 2, kc, n2, c), blk))
    args.append(x)
    if filtered:
        in_specs.append(pl.BlockSpec((1, 2, kc, n2, c), lambda g, pi: (0, 0, g, 0, 0)))
        args.append(spectrum)
    kern = functools.partial(_dft_inner_kernel, kc=kc, n2=n2, filtered=filtered)
    return pl.pallas_call(
        kern,
        grid=(n1 // kc, p),
        in_specs=in_specs,
        out_specs=pl.BlockSpec((1, 2, kc, n2, c), blk),
        out_shape=jax.ShapeDtypeStruct(x.shape, out_dtype),
        compiler_params=_params("parallel", "parallel"),
        name="dft_inner",
    )(*args)


def _cmul_kernel(x_ref, f_ref, o_ref, *, n):
    xr, xi = x_ref[0, :n], x_ref[0, n:]
    fr, fi = f_ref[0, :n], f_ref[0, n:]
    o_ref[0, :n] = (xr * fr - xi * fi).astype(o_ref.dtype)
    o_ref[0, n:] = (xr * fi + xi * fr).astype(o_ref.dtype)


def _cmul(x, f, out_dtype):
    p, n2x, c = x.shape
    kern = functools.partial(_cmul_kernel, n=n2x // 2)
    return pl.pallas_call(
        kern,
        grid=(p,),
        in_specs=[pl.BlockSpec((1, n2x, c), lambda pi: (pi, 0, 0)),
                  pl.BlockSpec((1, n2x, c), lambda pi: (0, 0, 0))],
        out_specs=pl.BlockSpec((1, n2x, c), lambda pi: (pi, 0, 0)),
        out_shape=jax.ShapeDtypeStruct(x.shape, out_dtype),
        compiler_params=_params("parallel"),
        name="spectrum_product",
    )(x, f)


def _phase_tables(rows_n1, n1_total, n2_total):
    n = n1_total * n2_total
    n2 = jnp.arange(n2_total, dtype=jnp.int32)[:, None, None]
    k1 = jnp.arange(n1_total, dtype=jnp.int32)[None, :, None]
    n1 = jnp.arange(rows_n1, dtype=jnp.int32)[None, None, :]
    ph = ((n2_total * n1 + n2) * k1) % n
    ang = ph.astype(F32) * (2.0 * math.pi / n)
    return jnp.cos(ang), jnp.sin(ang)


def _forward_tables(length, n1_total, n2_total):
    rows = length // n2_total
    cs, sn = _phase_tables(rows, n1_total, n2_total)
    paired = jnp.concatenate([jnp.concatenate([cs, sn], axis=2),
                              jnp.concatenate([-sn, cs], axis=2)], axis=1)
    cs, sn = _phase_tables(n1_total, n1_total, n2_total)
    real = jnp.concatenate([cs, -sn], axis=1)
    return paired.astype(BF16), real.astype(BF16)


def _inverse_table(length, n1_total, n2_total):
    rows = length // n2_total
    cs, sn = _phase_tables(rows, n1_total, n2_total)
    cs = jnp.swapaxes(cs, 1, 2) / (n1_total * n2_total)
    sn = jnp.swapaxes(sn, 1, 2) / (n1_total * n2_total)
    tb = jnp.concatenate([jnp.concatenate([cs, -sn], axis=2),
                          jnp.concatenate([sn, cs], axis=2)], axis=1)
    return tb.astype(BF16)


def _inner_matrices(n2_total):
    idx = np.arange(n2_total)
    ang = 2.0 * np.pi * ((idx[:, None] * idx[None, :]) % n2_total) / n2_total
    cs, sn = np.cos(ang), np.sin(ang)
    fwd = np.block([[cs, sn], [-sn, cs]])
    inv = np.block([[cs, -sn], [sn, cs]])
    return jnp.asarray(fwd, dtype=BF16), jnp.asarray(inv, dtype=BF16)


def _long_conv_gate(vv, x0, filt, bias):
    p, _, length, c = vv.shape
    n = 2 * length
    n1_total = DFT_N1 if n % DFT_N1 == 0 and n // DFT_N1 >= 8 else n
    n2_total = n // n1_total
    rows = length // n2_total
    t_pair, t_real = _forward_tables(length, n1_total, n2_total)
    t_inv = _inverse_table(length, n1_total, n2_total)

    bias2 = bias.reshape(1, c)
    if n2_total > 1:
        m_fwd, m_inv = _inner_matrices(n2_total)
        shape5 = lambda a: a.reshape(a.shape[0], 2, n1_total, n2_total, c)
        vv4 = vv.reshape(p, 2 * rows, n2_total, c)
        x04 = x0.reshape(p, 2 * rows, n2_total, c)
        spec = _dft_outer_strided(t_real, filt.reshape(1, n1_total, n2_total, c), BF16, cb=c)
        spec = _dft_inner(m_fwd, shape5(spec), F32)
        y = _dft_outer_strided(t_pair, vv4, BF16, cb=c)
        y = _dft_inner(m_fwd, shape5(y), BF16, m_inv=m_inv, spectrum=spec)
        y = y.reshape(p, 2 * n1_total, n2_total, c)
        out = _dft_outer_strided(t_inv, y, BF16, cb=c // 2, hyena_args=(vv4, x04, bias2))
    else:
        vv2 = vv.reshape(p, 2 * rows, c)
        x02 = x0.reshape(p, 2 * rows, c)
        spec = _dft_outer(t_real, filt.reshape(1, n1_total, c), F32, c=c)
        y = _dft_outer(t_pair, vv2, F32, c=c)
        y = _cmul(y, spec, BF16)
        out = _dft_outer(t_inv, y, BF16, c=c, hyena_args=(vv2, x02, bias2))
    return out.reshape(p, 2, length, c)


def _merge_kernel(x_ref, attn_ref, hy_ref, g_ref, gate_ref, woa_ref, woh_ref, wout_ref, o_ref, *, d):
    g = jax.nn.sigmoid(g_ref[0].astype(F32))
    merged = (g[:, :d] * _dot(attn_ref[0], woa_ref[...])
              + g[:, d:] * _dot(hy_ref[0, 0], woh_ref[...]))
    o_ref[0] = x_ref[0] + gate_ref[0] * _dot(merged.astype(BF16), wout_ref[...])


def _merge(x, attn, hyo, g, gate, woa, woh, wout, *, tm):
    b, length, d = x.shape
    p = b // 2
    c = hyo.shape[-1]
    row = lambda bi, i: (bi, i, 0)
    kern = functools.partial(_merge_kernel, d=d)
    return pl.pallas_call(
        kern,
        grid=(b, length // tm),
        in_specs=[pl.BlockSpec((1, tm, d), row),
                  pl.BlockSpec((1, tm, attn.shape[-1]), row),
                  pl.BlockSpec((1, 1, tm, c), lambda bi, i: (bi % p, bi // p, i, 0)),
                  pl.BlockSpec((1, tm, g.shape[-1]), row),
                  pl.BlockSpec((1, 1, d), lambda bi, i: (bi, 0, 0)),
                  _resident(woa.shape), _resident(woh.shape), _resident(wout.shape)],
        out_specs=pl.BlockSpec((1, tm, d), row),
        out_shape=jax.ShapeDtypeStruct(x.shape, F32),
        compiler_params=_params("parallel", "parallel"),
        name="merge",
    )(x, attn, hyo, g, gate, woa, woh, wout)


def _ffn_kernel(x_ref, shift_ref, scale_ref, gate_ref, gain_ref, wgu_ref, wd_ref, fin_ref, o_ref,
                *, d_ff, chunk, final_norm):
    x = x_ref[0]
    h = ((_rms(x) * gain_ref[...]) * (1.0 + scale_ref[0]) + shift_ref[0]).astype(BF16)
    acc = jnp.zeros(x.shape, F32)
    for c0 in range(0, d_ff, chunk):
        c1 = min(c0 + chunk, d_ff)
        gt = _dot(h, wgu_ref[:, c0:c1])
        up = _dot(h, wgu_ref[:, d_ff + c0:d_ff + c1])
        act = (gt * jax.nn.sigmoid(gt) * up).astype(BF16)
        acc = acc + _dot(act, wd_ref[c0:c1, :])
    y = x + gate_ref[0] * acc
    if final_norm:
        y = _rms(y) * fin_ref[...]
    o_ref[0] = y


def _ffn(x, shift, scale, gate, gain, wgu, wd, fin_gain, *, tm, final_norm):
    b, length, d = x.shape
    d_ff = wd.shape[0]
    row = lambda bi, i: (bi, i, 0)
    per_b = lambda bi, i: (bi, 0, 0)
    kern = functools.partial(_ffn_kernel, d_ff=d_ff, chunk=256, final_norm=final_norm)
    return pl.pallas_call(
        kern,
        grid=(b, length // tm),
        in_specs=[pl.BlockSpec((1, tm, d), row),
                  pl.BlockSpec((1, 1, d), per_b),
                  pl.BlockSpec((1, 1, d), per_b),
                  pl.BlockSpec((1, 1, d), per_b),
                  _resident((1, d)),
                  _resident(wgu.shape), _resident(wd.shape),
                  _resident((1, d))],
        out_specs=pl.BlockSpec((1, tm, d), row),
        out_shape=jax.ShapeDtypeStruct(x.shape, F32),
        compiler_params=_params("parallel", "parallel"),
        name="ffn",
    )(x, shift, scale, gate, gain, wgu, wd, fin_gain)


def _rope_tables(length):
    rows = length // GRID_W
    axis_dim = HEAD_DIM // 2
    row = jnp.repeat(jnp.arange(rows, dtype=F32), GRID_W)
    col = jnp.tile(jnp.arange(GRID_W, dtype=F32), rows)
    inv_freq = ROPE_THETA ** (-jnp.arange(0, axis_dim, 2, dtype=F32) / axis_dim)
    ang = jnp.concatenate([row[:, None] * inv_freq, col[:, None] * inv_freq], axis=-1)
    cs, sn = jnp.cos(ang), jnp.sin(ang)
    return jnp.concatenate([cs, cs], axis=-1), jnp.concatenate([-sn, sn], axis=-1)


def _head_perm(n_heads):
    within = np.concatenate([np.arange(0, HEAD_DIM, 2), np.arange(1, HEAD_DIM, 2)])
    return np.concatenate([h * HEAD_DIM + within for h in range(n_heads)]), within


def kernel(x, c, ctx, c_ctx, w_mod, b_mod, norm_mix, w_in, q_norm, k_norm, conv_w, conv_b, filt_w1, filt_b1, filt_w2, filt_b2, filt_w3, filt_b3, filt_w4, filt_freq, hyena_bias, w_o_attn, w_o_hyena, w_out, norm_ffn, w_gate_up, w_down, norm_final):
    b, length, d = x.shape
    ctx_len = ctx.shape[1]
    depth = w_mod.shape[0]
    d_attn = N_HEADS * HEAD_DIM
    d_kv = N_KV_HEADS * HEAD_DIM
    d_hy = conv_w.shape[-1]
    c_hy = d_hy // 3
    tm = ROW_TILE
    tm_ctx = ctx_len
    key_chunk = ATTN_KEY_CHUNK if (ctx_len + length) % ATTN_KEY_CHUNK == 0 else MXU_TILE
    assert b % 2 == 0 and length % FFN_ROW_TILE == 0 and ctx_len % MXU_TILE == 0

    perm, within = _head_perm(N_HEADS + N_KV_HEADS)
    d_qk = d_attn + d_kv

    cosf, sinf = _rope_tables(length)
    cos_ctx = jnp.ones((ctx_len, HEAD_DIM), F32)
    sin_ctx = jnp.zeros((ctx_len, HEAD_DIM), F32)

    m_rows = 2 * SUBLANES
    assert b + 1 <= m_rows
    c_rows = jnp.zeros((m_rows, d), F32).at[:b].set(c).at[b].set(c_ctx)
    mod_layers = _modulation(c_rows, w_mod, b_mod)
    fin = norm_final.reshape(1, d)

    for layer in range(depth):
        last = layer == depth - 1
        mod_all = mod_layers[layer]
        mods = [m.reshape(b, 1, d) for m in jnp.split(mod_all[:b], N_MOD, axis=-1)]
        mods_c = [jnp.broadcast_to(m.reshape(1, 1, d), (b, 1, d))
                  for m in jnp.split(mod_all[b], N_MOD, axis=-1)]
        shift1, scale1, gate1, shift2, scale2, gate2 = mods
        c_shift1, c_scale1, c_gate1, c_shift2, c_scale2, c_gate2 = mods_c

        w_in_bf = jnp.concatenate([w_in[layer][:, :d_qk][:, perm], w_in[layer][:, d_qk:]],
                                  axis=1).astype(BF16)
        qg = (q_norm[layer][within] * (HEAD_DIM ** -0.5 * math.log2(math.e))).reshape(1, HEAD_DIM)
        kg = k_norm[layer][within].reshape(1, HEAD_DIM)
        gain_mix = norm_mix[layer].reshape(1, d)
        gain_ffn = norm_ffn[layer].reshape(1, d)
        woa = w_o_attn[layer].astype(BF16)
        woh = w_o_hyena[layer].astype(BF16)
        wout = w_out[layer].astype(BF16)
        wgu = w_gate_up[layer].astype(BF16)
        wd = w_down[layer].astype(BF16)
        filt = (filt_w1[layer], filt_b1[layer], filt_w2[layer], filt_b2[layer],
                filt_w3[layer], filt_b3[layer], filt_w4[layer], filt_freq[layer])
        proj = functools.partial(_in_projection, gain=gain_mix, w_bf=w_in_bf, qg=qg, kg=kg,
                                 conv_w=conv_w[layer], conv_b=conv_b[layer],
                                 d_attn=d_attn, d_kv=d_kv)

        qc, kc, vc, gc, vv_c, x0_c = proj(ctx, c_shift1, c_scale1, cosf=cos_ctx, sinf=sin_ctx,
                                          tm=tm_ctx)
        q, k, v, g, vv, x0 = proj(x, shift1, scale1, cosf=cosf, sinf=sinf, tm=tm)

        attn = _attention(q, k, v, kc, vc, tq=ATTN_Q_TILE, tk=key_chunk)
        hyo = _long_conv_gate(vv, x0, _hyena_filter(length, *filt, c_hy), hyena_bias[layer])
        x_mid = _merge(x, attn, hyo, g, gate1, woa, woh, wout, tm=tm)

        if not last:
            attn_c = _attention(qc, kc, vc, tq=ctx_len, tk=ctx_len)
            hyo_c = _long_conv_gate(vv_c, x0_c, _hyena_filter(ctx_len, *filt, c_hy),
                                    hyena_bias[layer])
            ctx = _merge(ctx, attn_c, hyo_c, gc, c_gate1, woa, woh, wout, tm=tm_ctx)
            ctx = _ffn(ctx, c_shift2, c_scale2, c_gate2, gain_ffn, wgu, wd, fin,
                       tm=tm_ctx, final_norm=False)

        x = _ffn(x_mid, shift2, scale2, gate2, gain_ffn, wgu, wd, fin, tm=FFN_ROW_TILE,
                 final_norm=last)
    return x
```

```python
import functools
import math

import numpy as np
import jax
import jax.numpy as jnp
from jax import lax
from jax.experimental import pallas as pl
from jax.experimental.pallas import tpu as pltpu

F32 = jnp.float32
BF16 = jnp.bfloat16

GRID_W = 64
N_HEADS = 8
N_KV_HEADS = 2
KV_GROUP = N_HEADS // N_KV_HEADS
HEAD_DIM = 128
ROPE_THETA = 10000.0
FILTER_EMB = 33
FILTER_BANDS = (FILTER_EMB - 1) // 2
DECAY_TARGET = 1e-2
FAST_DECAY_PCT = 0.3
SLOW_DECAY_PCT = 1.5
N_MOD = 6
EPS = 1e-6

LANES = 128
SUBLANES = 8
MXU_TILE = 256
VMEM_LIMIT = 48 * 1024 * 1024
DFT_N1 = 128

ROW_TILE = 512
FFN_ROW_TILE = 1024
ATTN_Q_TILE = 256
ATTN_KEY_CHUNK = 768
MOD_COL_TILE = 1536


def _params(*sem):
    return pltpu.CompilerParams(dimension_semantics=sem, vmem_limit_bytes=VMEM_LIMIT)


def _resident(shape):
    nd = len(shape)
    return pl.BlockSpec(shape, lambda *_: (0,) * nd, pipeline_mode=pl.Buffered(1))


def _dot(a, b):
    return jnp.dot(a, b, preferred_element_type=F32)


def _split_bf16(a):
    hi = a.astype(BF16)
    lo = (a - hi.astype(F32)).astype(BF16)
    return hi, lo


def _dot3(a, b):
    ah, al = _split_bf16(a)
    bh, bl = _split_bf16(b)
    return _dot(ah, bh) + (_dot(ah, bl) + _dot(al, bh))


def _rms(x):
    return x * lax.rsqrt(jnp.mean(x * x, axis=-1, keepdims=True) + EPS)


def _mod_kernel(c_ref, w_ref, b_ref, o_ref):
    c = c_ref[...]
    s = c * jax.nn.sigmoid(c)
    o_ref[0] = _dot3(s, w_ref[0]) + b_ref[0]


def _modulation(c_rows, w, b):
    m, d = c_rows.shape
    depth, _, n = w.shape
    tn = MOD_COL_TILE
    return pl.pallas_call(
        _mod_kernel,
        grid=(depth, n // tn),
        in_specs=[pl.BlockSpec((m, d), lambda l, j: (0, 0)),
                  pl.BlockSpec((1, d, tn), lambda l, j: (l, 0, j)),
                  pl.BlockSpec((1, 1, tn), lambda l, j: (l, 0, j))],
        out_specs=pl.BlockSpec((1, m, tn), lambda l, j: (l, 0, j)),
        out_shape=jax.ShapeDtypeStruct((depth, m, n), F32),
        compiler_params=_params("parallel", "parallel"),
        name="modulation",
    )(c_rows, w, b.reshape(depth, 1, n))


def _inproj_kernel(x_ref, xp_ref, xn_ref, shift_ref, scale_ref, gain_ref, w_ref, qg_ref, kg_ref,
                   cos_ref, sin_ref, cw_ref, cb_ref,
                   q_ref, k_ref, v_ref, g_ref, vv_ref, x0_ref, us_ref, *, d_attn, d_kv, d_hy):
    i = pl.program_id(1)
    tm = x_ref.shape[1]
    halo = xp_ref.shape[1]
    x = jnp.concatenate([x_ref[0], xp_ref[0], xn_ref[0]], axis=0)
    h = (_rms(x) * gain_ref[...]) * (1.0 + scale_ref[0]) + shift_ref[0]
    hb_all = h.astype(BF16)
    hb = hb_all[:tm]
    cosf = cos_ref[...]
    sinf = sin_ref[...]

    def head_norm_rope(t, gain):
        t = _rms(t) * gain
        return t * cosf + pltpu.roll(t, HEAD_DIM // 2, axis=1) * sinf

    c0 = 0
    q = _dot(hb, w_ref[:, c0:c0 + d_attn])
    for hd in range(d_attn // HEAD_DIM):
        sl = slice(hd * HEAD_DIM, (hd + 1) * HEAD_DIM)
        q_ref[0, :, sl] = head_norm_rope(q[:, sl], qg_ref[...]).astype(q_ref.dtype)
    c0 += d_attn
    k = _dot(hb, w_ref[:, c0:c0 + d_kv])
    for hd in range(d_kv // HEAD_DIM):
        sl = slice(hd * HEAD_DIM, (hd + 1) * HEAD_DIM)
        k_ref[0, :, sl] = head_norm_rope(k[:, sl], kg_ref[...]).astype(k_ref.dtype)
    c0 += d_kv
    v_ref[0] = _dot(hb, w_ref[:, c0:c0 + d_kv]).astype(v_ref.dtype)
    c0 += d_kv
    u_all = _dot(hb_all, w_ref[:, c0:c0 + d_hy])
    c0 += d_hy
    g_ref[0] = _dot(hb, w_ref[:, c0:]).astype(g_ref.dtype)

    u = u_all[:tm]
    has_prev = i > 0
    has_next = i < pl.num_programs(1) - 1
    n_slabs = d_hy // LANES
    for t in range(n_slabs):
        lanes = slice(t * LANES, (t + 1) * LANES)
        us_ref[t, 0:halo, :] = jnp.where(has_prev, u_all[tm:tm + halo, lanes], 0.0)
        us_ref[t, halo:halo + tm, :] = u_all[:tm, lanes]
        us_ref[t, halo + tm:, :] = jnp.where(has_next, u_all[tm + halo:, lanes], 0.0)
    up = jnp.concatenate([us_ref[t, pl.ds(halo - 1, tm), :] for t in range(n_slabs)], axis=1)
    un = jnp.concatenate([us_ref[t, pl.ds(halo + 1, tm), :] for t in range(n_slabs)], axis=1)
    y = up * cw_ref[0:1, :] + u * cw_ref[1:2, :] + un * cw_ref[2:3, :] + cb_ref[...]
    c = d_hy // 3
    x0_ref[0, 0] = y[:, :c].astype(x0_ref.dtype)
    vv_ref[0, 0] = (y[:, 2 * c:] * y[:, c:2 * c]).astype(vv_ref.dtype)


def _in_projection(x, shift, scale, gain, w_bf, qg, kg, cosf, sinf, conv_w, conv_b,
                   *, tm, d_attn, d_kv):
    b, length, d = x.shape
    d_hy = conv_w.shape[-1]
    c = d_hy // 3
    p = b // 2
    d_gate = w_bf.shape[1] - d_attn - 2 * d_kv - d_hy
    halo = SUBLANES
    n_halo = length // halo
    per = tm // halo
    row = lambda bi, i: (bi, i, 0)
    per_b = lambda bi, i: (bi, 0, 0)
    pair_major = lambda bi, i: (bi % p, bi // p, i, 0)
    kern = functools.partial(_inproj_kernel, d_attn=d_attn, d_kv=d_kv, d_hy=d_hy)
    return pl.pallas_call(
        kern,
        grid=(b, length // tm),
        in_specs=[pl.BlockSpec((1, tm, d), row),
                  pl.BlockSpec((1, halo, d), lambda bi, i: (bi, jnp.maximum(i * per - 1, 0), 0)),
                  pl.BlockSpec((1, halo, d),
                               lambda bi, i: (bi, jnp.minimum((i + 1) * per, n_halo - 1), 0)),
                  pl.BlockSpec((1, 1, d), per_b),
                  pl.BlockSpec((1, 1, d), per_b),
                  _resident((1, d)),
                  _resident(w_bf.shape),
                  _resident((1, HEAD_DIM)),
                  _resident((1, HEAD_DIM)),
                  pl.BlockSpec((tm, HEAD_DIM), lambda bi, i: (i, 0)),
                  pl.BlockSpec((tm, HEAD_DIM), lambda bi, i: (i, 0)),
                  _resident((3, d_hy)),
                  _resident((1, d_hy))],
        out_specs=[pl.BlockSpec((1, tm, d_attn), row),
                   pl.BlockSpec((1, tm, d_kv), row),
                   pl.BlockSpec((1, tm, d_kv), row),
                   pl.BlockSpec((1, tm, d_gate), row),
                   pl.BlockSpec((1, 1, tm, c), pair_major),
                   pl.BlockSpec((1, 1, tm, c), pair_major)],
        out_shape=[jax.ShapeDtypeStruct((b, length, d_attn), BF16),
                   jax.ShapeDtypeStruct((b, length, d_kv), BF16),
                   jax.ShapeDtypeStruct((b, length, d_kv), BF16),
                   jax.ShapeDtypeStruct((b, length, d_gate), BF16),
                   jax.ShapeDtypeStruct((p, 2, length, c), BF16),
                   jax.ShapeDtypeStruct((p, 2, length, c), BF16)],
        scratch_shapes=[pltpu.VMEM((d_hy // LANES, tm + 2 * halo, LANES), F32)],
        compiler_params=_params("parallel", "parallel"),
        name="in_projection",
    )(x, x, x, shift, scale, gain, w_bf, qg, kg, cosf, sinf, conv_w, conv_b.reshape(1, d_hy))


def _flash_kernel(*refs, tq, tk, nk, extra):
    if extra:
        (q_ref, qn_ref, k_ref, v_ref, ke_ref, ve_ref, o_ref,
         kall_ref, vall_ref, qs_ref, s_ref, m_ref, acc_ref) = refs
    else:
        q_ref, qn_ref, k_ref, v_ref, o_ref, kall_ref, vall_ref, qs_ref, s_ref, m_ref, acc_ref = refs
    first_slot = 2
    slot_of = lambda j: first_slot if j == 0 else (j - 1) % 2

    def stack_heads(src_ref, which):
        for g in range(KV_GROUP):
            qs_ref[which, g * tq:(g + 1) * tq, :] = src_ref[0, :, g * HEAD_DIM:(g + 1) * HEAD_DIM]

    def chunk(ref, j):
        return ref[j * tk:(j + 1) * tk, :]

    def scores(j, slot, which=0):
        s_ref[slot] = lax.dot_general(qs_ref[which], chunk(kall_ref, j), (((1,), (1,)), ((), ())),
                                      preferred_element_type=F32)

    stack_heads(q_ref, 0)
    stack_heads(qn_ref, 1)

    @pl.when(pl.program_id(2) == 0)
    def _():
        n_main = k_ref.shape[1]
        kall_ref[0:n_main, :] = k_ref[0]
        vall_ref[0:n_main, 0:HEAD_DIM] = v_ref[0]
        if extra:
            kall_ref[n_main:, :] = ke_ref[0]
            vall_ref[n_main:, 0:HEAD_DIM] = ve_ref[0]
        vall_ref[:, HEAD_DIM:] = jnp.ones((vall_ref.shape[0], HEAD_DIM), vall_ref.dtype)
        scores(0, first_slot)

    m_ref[...] = jnp.full(m_ref.shape, -1e30, F32)
    acc_ref[...] = jnp.zeros(acc_ref.shape, F32)
    nt = tk // LANES

    def consume(j, slot):
        tiles = [s_ref[slot, :, t * LANES:(t + 1) * LANES] for t in range(nt)]
        mt = functools.reduce(jnp.maximum, tiles)
        m_prev = m_ref[...]
        m_new = jnp.maximum(m_prev, jnp.max(mt, axis=-1, keepdims=True))
        alpha = jnp.exp2(m_prev - m_new)
        p = jnp.concatenate([jnp.exp2(t - m_new).astype(BF16) for t in tiles], axis=1)
        pv = _dot(p, chunk(vall_ref, j))
        acc_ref[...] = jnp.concatenate([alpha, alpha], axis=1) * acc_ref[...] + pv
        m_ref[...] = m_new

    for j in range(nk - 1):
        scores(j + 1, slot_of(j + 1))
        consume(j, slot_of(j))
    if nk == 1:
        consume(0, first_slot)
        scores(0, first_slot, which=1)
    else:
        scores(0, first_slot, which=1)
        consume(nk - 1, slot_of(nk - 1))

    acc = acc_ref[...]
    out = acc[:, :HEAD_DIM] / acc[:, HEAD_DIM:]
    for g in range(KV_GROUP):
        o_ref[0, :, g * HEAD_DIM:(g + 1) * HEAD_DIM] = out[g * tq:(g + 1) * tq].astype(o_ref.dtype)


def _attention(q, k, v, k_extra=None, v_extra=None, *, tq, tk):
    b, lq, _ = q.shape
    extra = k_extra is not None
    lk = k.shape[1] + (k_extra.shape[1] if extra else 0)
    gw = KV_GROUP * HEAD_DIM
    rows = KV_GROUP * tq
    assert lk % tk == 0
    nq = lq // tq
    kv_spec = lambda a: pl.BlockSpec((1, a.shape[1], HEAD_DIM), lambda bi, h, i: (bi, 0, h))
    kv_args = [k, v] + ([k_extra, v_extra] if extra else [])
    kern = functools.partial(_flash_kernel, tq=tq, tk=tk, nk=lk // tk, extra=extra)
    return pl.pallas_call(
        kern,
        grid=(b, N_KV_HEADS, nq),
        in_specs=[pl.BlockSpec((1, tq, gw), lambda bi, h, i: (bi, i, h)),
                  pl.BlockSpec((1, tq, gw), lambda bi, h, i: (bi, jnp.minimum(i + 1, nq - 1), h))]
                 + [kv_spec(a) for a in kv_args],
        out_specs=pl.BlockSpec((1, tq, gw), lambda bi, h, i: (bi, i, h)),
        out_shape=jax.ShapeDtypeStruct(q.shape, BF16),
        scratch_shapes=[pltpu.VMEM((lk, HEAD_DIM), BF16),
                        pltpu.VMEM((lk, 2 * HEAD_DIM), BF16),
                        pltpu.VMEM((2, rows, HEAD_DIM), BF16),
                        pltpu.VMEM((3, rows, tk), F32),
                        pltpu.VMEM((rows, LANES), F32),
                        pltpu.VMEM((rows, 2 * HEAD_DIM), F32)],
        compiler_params=_params("arbitrary", "arbitrary", "arbitrary"),
        name="attention",
    )(q, q, *kv_args)


def _filter_kernel(z_ref, w1_ref, b1_ref, w2_ref, b2_ref, w3_ref, b3_ref, w4_ref, fr_ref, dec_ref,
                   o_ref):
    c = o_ref.shape[-1]
    h = jnp.sin(fr_ref[0:1, :] * (_dot3(z_ref[...], w1_ref[...]) + b1_ref[...]))
    h = jnp.sin(fr_ref[1:2, :] * (_dot3(h, w2_ref[...]) + b2_ref[...]))
    h = jnp.sin(fr_ref[2:3, :] * (_dot3(h, w3_ref[...]) + b3_ref[...]))
    out = _dot3(h, w4_ref[...])
    o_ref[0] = out[:, :c] * dec_ref[0]
    o_ref[1] = out[:, c:] * dec_ref[1]


def _pad_to(a, shape):
    return jnp.pad(a, [(0, s - d) for d, s in zip(a.shape, shape)])


def _block_diag2(a, b, shape):
    za = jnp.zeros(shape, a.dtype)
    return jnp.concatenate([jnp.concatenate([_pad_to(a, shape), za], axis=1),
                            jnp.concatenate([za, _pad_to(b, shape)], axis=1)], axis=0)


def _hyena_filter(length, fw1, fb1, fw2, fb2, fw3, fb3, fw4, freq, c):
    half = LANES // 2
    assert fw1.shape[0] <= half and fw1.shape[1] <= half
    rev = lambda a: jnp.concatenate([a[:1], a[:0:-1]])
    t = jnp.linspace(0.0, 1.0, length, dtype=F32)
    w = (2.0 * math.pi / length) * jnp.arange(length, dtype=F32)
    f = jnp.linspace(1e-4, FILTER_BANDS - 1, FILTER_BANDS, dtype=F32)[None, :]

    def features(tt, ww):
        tt, ww = tt[:, None], ww[:, None]
        return _pad_to(jnp.concatenate([tt, jnp.cos(f * ww), -jnp.sin(f * ww)], axis=-1),
                       (length, half))

    z = jnp.concatenate([features(t, w), features(rev(t), rev(w))], axis=1)
    max_decay = math.log(DECAY_TARGET) / FAST_DECAY_PCT
    min_decay = math.log(DECAY_TARGET) / SLOW_DECAY_PCT
    deltas = jnp.abs(jnp.linspace(min_decay, max_decay, c, dtype=F32))
    not_first = (jnp.arange(length) > 0).astype(F32)[:, None]
    decay = jnp.stack([jnp.exp(-t[:, None] * deltas),
                       jnp.exp(-rev(t)[:, None] * deltas) * not_first])

    sq = (half, half)
    w1 = _block_diag2(fw1, fw1, sq)
    w2 = _block_diag2(fw2, fw2, sq)
    w3 = _block_diag2(fw3, fw3, sq)
    w4 = _block_diag2(fw4[:, :c], fw4[:, c:], (half, c))
    twice = lambda v: jnp.tile(_pad_to(v.reshape(-1, v.shape[-1]), (v.size // v.shape[-1], half)),
                               (1, 2))
    b1, b2, b3, fr = twice(fb1), twice(fb2), twice(fb3), twice(freq)
    tl = min(length, 1024)
    out = pl.pallas_call(
        _filter_kernel,
        grid=(length // tl,),
        in_specs=[pl.BlockSpec((tl, LANES), lambda i: (i, 0)),
                  _resident(w1.shape), _resident(b1.shape),
                  _resident(w2.shape), _resident(b2.shape),
                  _resident(w3.shape), _resident(b3.shape),
                  _resident(w4.shape), _resident(fr.shape),
                  pl.BlockSpec((2, tl, c), lambda i: (0, i, 0))],
        out_specs=pl.BlockSpec((2, tl, c), lambda i: (0, i, 0)),
        out_shape=jax.ShapeDtypeStruct((2, length, c), F32),
        compiler_params=_params("parallel"),
        name="hyena_filter",
    )(z, w1, b1, w2, b2, w3, b3, w4, fr, decay)
    return out.reshape(2 * length, c)


def _dft_outer_kernel(*refs, nb, c, per_group, hyena):
    if hyena:
        t_ref, z_ref, vv_ref, x0_ref, bias_ref, o_ref = refs
    else:
        t_ref, z_ref, o_ref = refs
    for j in range(nb):
        cols = slice(j * c, (j + 1) * c)
        y = _dot(t_ref[j if per_group else 0], z_ref[0, :, cols].astype(BF16))
        if hyena:
            y = (y + vv_ref[0, :, cols] * bias_ref[...]) * x0_ref[0, :, cols]
        o_ref[0, :, cols] = y.astype(o_ref.dtype)


def _dft_outer(table, z, out_dtype, *, c, hyena_args=None):
    p, k, ncols = z.shape
    nt, m, _ = table.shape
    groups = ncols // c
    nb = min(groups, 8)
    per_group = nt > 1
    t_spec = (pl.BlockSpec((nb, m, k), lambda g, pi: (g, 0, 0)) if per_group
              else _resident((1, m, k)))
    col_spec = lambda rows: pl.BlockSpec((1, rows, nb * c), lambda g, pi: (pi, 0, g))
    in_specs = [t_spec, col_spec(k)]
    args = [table, z]
    if hyena_args is not None:
        vv, x0, bias = hyena_args
        in_specs += [col_spec(m), col_spec(m), _resident((1, c))]
        args += [vv, x0, bias]
    kern = functools.partial(_dft_outer_kernel, nb=nb, c=c, per_group=per_group,
                             hyena=hyena_args is not None)
    return pl.pallas_call(
        kern,
        grid=(groups // nb, p),
        in_specs=in_specs,
        out_specs=col_spec(m),
        out_shape=jax.ShapeDtypeStruct((p, m, ncols), out_dtype),
        compiler_params=_params("parallel", "parallel"),
        name="dft_outer",
    )(*args)


def _dft_strided_kernel(*refs, nb, hyena):
    if hyena:
        t_ref, z_ref, vv_ref, x0_ref, bias_ref, o_ref, zs_ref, os_ref, vs_ref, xs_ref = refs
    else:
        t_ref, z_ref, o_ref, zs_ref, os_ref = refs
    k, cb = z_ref.shape[1], z_ref.shape[3]
    m = o_ref.shape[1]
    nl = cb // LANES
    ng = nb // SUBLANES
    lane = lambda t: slice(t * LANES, (t + 1) * LANES)

    def stage(dst_ref, val):
        for t in range(nl):
            for g in range(ng):
                piece = val[:, g * SUBLANES:(g + 1) * SUBLANES, lane(t)]
                dst_ref[t, g] = piece.reshape(val.shape[0] * SUBLANES, LANES)

    def rows(src_ref, j, n):
        g, r = divmod(j, SUBLANES)
        return jnp.concatenate([src_ref[t, g, pl.ds(r, n, stride=SUBLANES), :]
                                for t in range(nl)], axis=1)

    stage(zs_ref, z_ref[0].astype(F32))
    if hyena:
        stage(vs_ref, vv_ref[0].astype(F32))
        stage(xs_ref, x0_ref[0].astype(F32))
    for j in range(nb):
        y = _dot(t_ref[j], rows(zs_ref, j, k).astype(BF16))
        if hyena:
            y = (y + rows(vs_ref, j, m) * bias_ref[...]) * rows(xs_ref, j, m)
        g, r = divmod(j, SUBLANES)
        for t in range(nl):
            os_ref[t, g, pl.ds(r, m, stride=SUBLANES), :] = y[:, lane(t)]
    o_ref[0] = jnp.concatenate(
        [jnp.concatenate([os_ref[t, g].reshape(m, SUBLANES, LANES) for g in range(ng)], axis=1)
         for t in range(nl)], axis=2).astype(o_ref.dtype)


def _dft_outer_strided(table, z, out_dtype, *, cb, hyena_args=None):
    p, k, n2, c = z.shape
    _, m, _ = table.shape
    nb = 16
    blk = lambda rows: pl.BlockSpec((1, rows, nb, cb), lambda g, ci, pi: (pi, 0, g, ci))
    in_specs = [pl.BlockSpec((nb, m, k), lambda g, ci, pi: (g, 0, 0)), blk(k)]
    args = [table, z]
    staging = lambda rows: pltpu.VMEM((cb // LANES, nb // SUBLANES, rows * SUBLANES, LANES), F32)
    scratch = [staging(k), staging(m)]
    if hyena_args is not None:
        vv, x0, bias = hyena_args
        in_specs += [blk(m), blk(m), pl.BlockSpec((1, cb), lambda g, ci, pi: (0, ci))]
        args += [vv, x0, bias]
        scratch += [staging(m), staging(m)]
    kern = functools.partial(_dft_strided_kernel, nb=nb, hyena=hyena_args is not None)
    return pl.pallas_call(
        kern,
        grid=(n2 // nb, c // cb, p),
        in_specs=in_specs,
        out_specs=blk(m),
        out_shape=jax.ShapeDtypeStruct((p, m, n2, c), out_dtype),
        scratch_shapes=scratch,
        compiler_params=_params("parallel", "parallel", "parallel"),
        name="dft_outer_strided",
    )(*args)


def _dft_inner_kernel(*refs, kc, n2, filtered):
    if filtered:
        mf_ref, mi_ref, x_ref, f_ref, o_ref = refs
    else:
        mf_ref, x_ref, o_ref = refs
    for kk in range(kc):
        zin = jnp.concatenate([x_ref[0, 0, kk], x_ref[0, 1, kk]], axis=0)
        y = _dot(mf_ref[...], zin)
        re, im = y[:n2], y[n2:]
        if filtered:
            fr, fi = f_ref[0, 0, kk], f_ref[0, 1, kk]
            prod = jnp.concatenate([re * fr - im * fi, re * fi + im * fr], axis=0)
            y = _dot(mi_ref[...], prod.astype(BF16))
            re, im = y[:n2], y[n2:]
        o_ref[0, 0, kk] = re.astype(o_ref.dtype)
        o_ref[0, 1, kk] = im.astype(o_ref.dtype)


def _dft_inner(m_fwd, x, out_dtype, m_inv=None, spectrum=None):
    p, _, n1, n2, c = x.shape
    kc = 16
    blk = lambda g, pi: (pi, 0, g, 0, 0)
    filtered = spectrum is not None
    in_specs = [_resident(m_fwd.shape)]
    args = [m_fwd]
    if filtered:
        in_specs.append(_resident(m_inv.shape))
        args.append(m_inv)
    in_specs.append(pl.BlockSpec((1, 2, kc, n2, c), blk))
    args.append(x)
    if filtered:
        in_specs.append(pl.BlockSpec((1, 2, kc, n2, c), lambda g, pi: (0, 0, g, 0, 0)))
        args.append(spectrum)
    kern = functools.partial(_dft_inner_kernel, kc=kc, n2=n2, filtered=filtered)
    return pl.pallas_call(
        kern,
        grid=(n1 // kc, p),
        in_specs=in_specs,
        out_specs=pl.BlockSpec((1, 2, kc, n2, c), blk),
        out_shape=jax.ShapeDtypeStruct(x.shape, out_dtype),
        compiler_params=_params("parallel", "parallel"),
        name="dft_inner",
    )(*args)


def _cmul_kernel(x_ref, f_ref, o_ref, *, n):
    xr, xi = x_ref[0, :n], x_ref[0, n:]
    fr, fi = f_ref[0, :n], f_ref[0, n:]
    o_ref[0, :n] = (xr * fr - xi * fi).astype(o_ref.dtype)
    o_ref[0, n:] = (xr * fi + xi * fr).astype(o_ref.dtype)


def _cmul(x, f, out_dtype):
    p, n2x, c = x.shape
    kern = functools.partial(_cmul_kernel, n=n2x // 2)
    return pl.pallas_call(
        kern,
        grid=(p,),
        in_specs=[pl.BlockSpec((1, n2x, c), lambda pi: (pi, 0, 0)),
                  pl.BlockSpec((1, n2x, c), lambda pi: (0, 0, 0))],
        out_specs=pl.BlockSpec((1, n2x, c), lambda pi: (pi, 0, 0)),
        out_shape=jax.ShapeDtypeStruct(x.shape, out_dtype),
        compiler_params=_params("parallel"),
        name="spectrum_product",
    )(x, f)


def _phase_tables(rows_n1, n1_total, n2_total):
    n = n1_total * n2_total
    n2 = jnp.arange(n2_total, dtype=jnp.int32)[:, None, None]
    k1 = jnp.arange(n1_total, dtype=jnp.int32)[None, :, None]
    n1 = jnp.arange(rows_n1, dtype=jnp.int32)[None, None, :]
    ph = ((n2_total * n1 + n2) * k1) % n
    ang = ph.astype(F32) * (2.0 * math.pi / n)
    return jnp.cos(ang), jnp.sin(ang)


def _forward_tables(length, n1_total, n2_total):
    rows = length // n2_total
    cs, sn = _phase_tables(rows, n1_total, n2_total)
    paired = jnp.concatenate([jnp.concatenate([cs, sn], axis=2),
                              jnp.concatenate([-sn, cs], axis=2)], axis=1)
    cs, sn = _phase_tables(n1_total, n1_total, n2_total)
    real = jnp.concatenate([cs, -sn], axis=1)
    return paired.astype(BF16), real.astype(BF16)


def _inverse_table(length, n1_total, n2_total):
    rows = length // n2_total
    cs, sn = _phase_tables(rows, n1_total, n2_total)
    cs = jnp.swapaxes(cs, 1, 2) / (n1_total * n2_total)
    sn = jnp.swapaxes(sn, 1, 2) / (n1_total * n2_total)
    tb = jnp.concatenate([jnp.concatenate([cs, -sn], axis=2),
                          jnp.concatenate([sn, cs], axis=2)], axis=1)
    return tb.astype(BF16)


def _inner_matrices(n2_total):
    idx = np.arange(n2_total)
    ang = 2.0 * np.pi * ((idx[:, None] * idx[None, :]) % n2_total) / n2_total
    cs, sn = np.cos(ang), np.sin(ang)
    fwd = np.block([[cs, sn], [-sn, cs]])
    inv = np.block([[cs, -sn], [sn, cs]])
    return jnp.asarray(fwd, dtype=BF16), jnp.asarray(inv, dtype=BF16)


def _long_conv_gate(vv, x0, filt, bias):
    p, _, length, c = vv.shape
    n = 2 * length
    n1_total = DFT_N1 if n % DFT_N1 == 0 and n // DFT_N1 >= 8 else n
    n2_total = n // n1_total
    rows = length // n2_total
    t_pair, t_real = _forward_tables(length, n1_total, n2_total)
    t_inv = _inverse_table(length, n1_total, n2_total)

    bias2 = bias.reshape(1, c)
    if n2_total > 1:
        m_fwd, m_inv = _inner_matrices(n2_total)
        shape5 = lambda a: a.reshape(a.shape[0], 2, n1_total, n2_total, c)
        vv4 = vv.reshape(p, 2 * rows, n2_total, c)
        x04 = x0.reshape(p, 2 * rows, n2_total, c)
        spec = _dft_outer_strided(t_real, filt.reshape(1, n1_total, n2_total, c), BF16, cb=c)
        spec = _dft_inner(m_fwd, shape5(spec), F32)
        y = _dft_outer_strided(t_pair, vv4, BF16, cb=c // 2)
        y = _dft_inner(m_fwd, shape5(y), BF16, m_inv=m_inv, spectrum=spec)
        y = y.reshape(p, 2 * n1_total, n2_total, c)
        out = _dft_outer_strided(t_inv, y, BF16, cb=c // 2, hyena_args=(vv4, x04, bias2))
    else:
        vv2 = vv.reshape(p, 2 * rows, c)
        x02 = x0.reshape(p, 2 * rows, c)
        spec = _dft_outer(t_real, filt.reshape(1, n1_total, c), F32, c=c)
        y = _dft_outer(t_pair, vv2, F32, c=c)
        y = _cmul(y, spec, BF16)
        out = _dft_outer(t_inv, y, BF16, c=c, hyena_args=(vv2, x02, bias2))
    return out.reshape(p, 2, length, c)


def _merge_kernel(x_ref, attn_ref, hy_ref, g_ref, gate_ref, woa_ref, woh_ref, wout_ref, o_ref, *, d):
    g = jax.nn.sigmoid(g_ref[0].astype(F32))
    merged = (g[:, :d] * _dot(attn_ref[0], woa_ref[...])
              + g[:, d:] * _dot(hy_ref[0, 0], woh_ref[...]))
    o_ref[0] = x_ref[0] + gate_ref[0] * _dot(merged.astype(BF16), wout_ref[...])


def _merge(x, attn, hyo, g, gate, woa, woh, wout, *, tm):
    b, length, d = x.shape
    p = b // 2
    c = hyo.shape[-1]
    row = lambda bi, i: (bi, i, 0)
    kern = functools.partial(_merge_kernel, d=d)
    return pl.pallas_call(
        kern,
        grid=(b, length // tm),
        in_specs=[pl.BlockSpec((1, tm, d), row),
                  pl.BlockSpec((1, tm, attn.shape[-1]), row),
                  pl.BlockSpec((1, 1, tm, c), lambda bi, i: (bi % p, bi // p, i, 0)),
                  pl.BlockSpec((1, tm, g.shape[-1]), row),
                  pl.BlockSpec((1, 1, d), lambda bi, i: (bi, 0, 0)),
                  _resident(woa.shape), _resident(woh.shape), _resident(wout.shape)],
        out_specs=pl.BlockSpec((1, tm, d), row),
        out_shape=jax.ShapeDtypeStruct(x.shape, F32),
        compiler_params=_params("parallel", "parallel"),
        name="merge",
    )(x, attn, hyo, g, gate, woa, woh, wout)


def _ffn_kernel(x_ref, shift_ref, scale_ref, gate_ref, gain_ref, wgu_ref, wd_ref, fin_ref, o_ref,
                *, d_ff, chunk, final_norm):
    x = x_ref[0]
    h = ((_rms(x) * gain_ref[...]) * (1.0 + scale_ref[0]) + shift_ref[0]).astype(BF16)
    acc = jnp.zeros(x.shape, F32)
    for c0 in range(0, d_ff, chunk):
        c1 = min(c0 + chunk, d_ff)
        gt = _dot(h, wgu_ref[:, c0:c1])
        up = _dot(h, wgu_ref[:, d_ff + c0:d_ff + c1])
        act = (gt * jax.nn.sigmoid(gt) * up).astype(BF16)
        acc = acc + _dot(act, wd_ref[c0:c1, :])
    y = x + gate_ref[0] * acc
    if final_norm:
        y = _rms(y) * fin_ref[...]
    o_ref[0] = y


def _ffn(x, shift, scale, gate, gain, wgu, wd, fin_gain, *, tm, final_norm):
    b, length, d = x.shape
    d_ff = wd.shape[0]
    row = lambda bi, i: (bi, i, 0)
    per_b = lambda bi, i: (bi, 0, 0)
    kern = functools.partial(_ffn_kernel, d_ff=d_ff, chunk=256, final_norm=final_norm)
    return pl.pallas_call(
        kern,
        grid=(b, length // tm),
        in_specs=[pl.BlockSpec((1, tm, d), row),
                  pl.BlockSpec((1, 1, d), per_b),
                  pl.BlockSpec((1, 1, d), per_b),
                  pl.BlockSpec((1, 1, d), per_b),
                  _resident((1, d)),
                  _resident(wgu.shape), _resident(wd.shape),
                  _resident((1, d))],
        out_specs=pl.BlockSpec((1, tm, d), row),
        out_shape=jax.ShapeDtypeStruct(x.shape, F32),
        compiler_params=_params("parallel", "parallel"),
        name="ffn",
    )(x, shift, scale, gate, gain, wgu, wd, fin_gain)


def _rope_tables(length):
    rows = length // GRID_W
    axis_dim = HEAD_DIM // 2
    row = jnp.repeat(jnp.arange(rows, dtype=F32), GRID_W)
    col = jnp.tile(jnp.arange(GRID_W, dtype=F32), rows)
    inv_freq = ROPE_THETA ** (-jnp.arange(0, axis_dim, 2, dtype=F32) / axis_dim)
    ang = jnp.concatenate([row[:, None] * inv_freq, col[:, None] * inv_freq], axis=-1)
    cs, sn = jnp.cos(ang), jnp.sin(ang)
    return jnp.concatenate([cs, cs], axis=-1), jnp.concatenate([-sn, sn], axis=-1)


def _head_perm(n_heads):
    within = np.concatenate([np.arange(0, HEAD_DIM, 2), np.arange(1, HEAD_DIM, 2)])
    return np.concatenate([h * HEAD_DIM + within for h in range(n_heads)]), within


def kernel(x, c, ctx, c_ctx, w_mod, b_mod, norm_mix, w_in, q_norm, k_norm, conv_w, conv_b, filt_w1, filt_b1, filt_w2, filt_b2, filt_w3, filt_b3, filt_w4, filt_freq, hyena_bias, w_o_attn, w_o_hyena, w_out, norm_ffn, w_gate_up, w_down, norm_final):
    b, length, d = x.shape
    ctx_len = ctx.shape[1]
    depth = w_mod.shape[0]
    d_attn = N_HEADS * HEAD_DIM
    d_kv = N_KV_HEADS * HEAD_DIM
    d_hy = conv_w.shape[-1]
    c_hy = d_hy // 3
    tm = ROW_TILE
    tm_ctx = ctx_len
    key_chunk = ATTN_KEY_CHUNK if (ctx_len + length) % ATTN_KEY_CHUNK == 0 else MXU_TILE
    assert b % 2 == 0 and length % FFN_ROW_TILE == 0 and ctx_len % MXU_TILE == 0

    perm, within = _head_perm(N_HEADS + N_KV_HEADS)
    d_qk = d_attn + d_kv

    cosf, sinf = _rope_tables(length)
    cos_ctx = jnp.ones((ctx_len, HEAD_DIM), F32)
    sin_ctx = jnp.zeros((ctx_len, HEAD_DIM), F32)

    m_rows = 2 * SUBLANES
    assert b + 1 <= m_rows
    c_rows = jnp.zeros((m_rows, d), F32).at[:b].set(c).at[b].set(c_ctx)
    mod_layers = _modulation(c_rows, w_mod, b_mod)
    fin = norm_final.reshape(1, d)

    for layer in range(depth):
        last = layer == depth - 1
        mod_all = mod_layers[layer]
        mods = [m.reshape(b, 1, d) for m in jnp.split(mod_all[:b], N_MOD, axis=-1)]
        mods_c = [jnp.broadcast_to(m.reshape(1, 1, d), (b, 1, d))
                  for m in jnp.split(mod_all[b], N_MOD, axis=-1)]
        shift1, scale1, gate1, shift2, scale2, gate2 = mods
        c_shift1, c_scale1, c_gate1, c_shift2, c_scale2, c_gate2 = mods_c

        w_in_bf = jnp.concatenate([w_in[layer][:, :d_qk][:, perm], w_in[layer][:, d_qk:]],
                                  axis=1).astype(BF16)
        qg = (q_norm[layer][within] * (HEAD_DIM ** -0.5 * math.log2(math.e))).reshape(1, HEAD_DIM)
        kg = k_norm[layer][within].reshape(1, HEAD_DIM)
        gain_mix = norm_mix[layer].reshape(1, d)
        gain_ffn = norm_ffn[layer].reshape(1, d)
        woa = w_o_attn[layer].astype(BF16)
        woh = w_o_hyena[layer].astype(BF16)
        wout = w_out[layer].astype(BF16)
        wgu = w_gate_up[layer].astype(BF16)
        wd = w_down[layer].astype(BF16)
        filt = (filt_w1[layer], filt_b1[layer], filt_w2[layer], filt_b2[layer],
                filt_w3[layer], filt_b3[layer], filt_w4[layer], filt_freq[layer])
        proj = functools.partial(_in_projection, gain=gain_mix, w_bf=w_in_bf, qg=qg, kg=kg,
                                 conv_w=conv_w[layer], conv_b=conv_b[layer],
                                 d_attn=d_attn, d_kv=d_kv)

        qc, kc, vc, gc, vv_c, x0_c = proj(ctx, c_shift1, c_scale1, cosf=cos_ctx, sinf=sin_ctx,
                                          tm=tm_ctx)
        q, k, v, g, vv, x0 = proj(x, shift1, scale1, cosf=cosf, sinf=sinf, tm=tm)

        attn = _attention(q, k, v, kc, vc, tq=ATTN_Q_TILE, tk=key_chunk)
        hyo = _long_conv_gate(vv, x0, _hyena_filter(length, *filt, c_hy), hyena_bias[layer])
        x_mid = _merge(x, attn, hyo, g, gate1, woa, woh, wout, tm=tm)

        if not last:
            attn_c = _attention(qc, kc, vc, tq=ctx_len, tk=ctx_len)
            hyo_c = _long_conv_gate(vv_c, x0_c, _hyena_filter(ctx_len, *filt, c_hy),
                                    hyena_bias[layer])
            ctx = _merge(ctx, attn_c, hyo_c, gc, c_gate1, woa, woh, wout, tm=tm_ctx)
            ctx = _ffn(ctx, c_shift2, c_scale2, c_gate2, gain_ffn, wgu, wd, fin,
                       tm=tm_ctx, final_norm=False)

        x = _ffn(x_mid, shift2, scale2, gate2, gain_ffn, wgu, wd, fin, tm=FFN_ROW_TILE,
                 final_norm=last)
    return x
```

```python
import functools
import math

import numpy as np
import jax
import jax.numpy as jnp
from jax import lax
from jax.experimental import pallas as pl
from jax.experimental.pallas import tpu as pltpu

F32 = jnp.float32
BF16 = jnp.bfloat16

GRID_W = 64
N_HEADS = 8
N_KV_HEADS = 2
KV_GROUP = N_HEADS // N_KV_HEADS
HEAD_DIM = 128
ROPE_THETA = 10000.0
FILTER_EMB = 33
FILTER_BANDS = (FILTER_EMB - 1) // 2
DECAY_TARGET = 1e-2
FAST_DECAY_PCT = 0.3
SLOW_DECAY_PCT = 1.5
N_MOD = 6
EPS = 1e-6

LANES = 128
SUBLANES = 8
MXU_TILE = 256
VMEM_LIMIT = 48 * 1024 * 1024
DFT_N1 = 128

ROW_TILE = 512
FFN_ROW_TILE = 1024
ATTN_Q_TILE = 256
ATTN_KEY_CHUNK = 768
MOD_COL_TILE = 1536


def _params(*sem):
    return pltpu.CompilerParams(dimension_semantics=sem, vmem_limit_bytes=VMEM_LIMIT)


def _resident(shape):
    nd = len(shape)
    return pl.BlockSpec(shape, lambda *_: (0,) * nd, pipeline_mode=pl.Buffered(1))


def _layer_resident(stacked, layer):
    nd = stacked.ndim - 1
    return pl.BlockSpec((1,) + stacked.shape[1:], lambda *_: (layer,) + (0,) * nd,
                        pipeline_mode=pl.Buffered(1))


def _dot(a, b):
    return jnp.dot(a, b, preferred_element_type=F32)


def _split_bf16(a):
    hi = a.astype(BF16)
    lo = (a - hi.astype(F32)).astype(BF16)
    return hi, lo


def _dot3(a, b):
    ah, al = _split_bf16(a)
    bh, bl = _split_bf16(b)
    return _dot(ah, bh) + (_dot(ah, bl) + _dot(al, bh))


def _rms(x):
    return x * lax.rsqrt(jnp.mean(x * x, axis=-1, keepdims=True) + EPS)


def _mod_kernel(c_ref, w_ref, b_ref, o_ref):
    c = c_ref[...]
    s = c * jax.nn.sigmoid(c)
    o_ref[0] = _dot3(s, w_ref[0]) + b_ref[0]


def _modulation(c_rows, w, b):
    m, d = c_rows.shape
    depth, _, n = w.shape
    tn = MOD_COL_TILE
    return pl.pallas_call(
        _mod_kernel,
        grid=(depth, n // tn),
        in_specs=[pl.BlockSpec((m, d), lambda l, j: (0, 0)),
                  pl.BlockSpec((1, d, tn), lambda l, j: (l, 0, j)),
                  pl.BlockSpec((1, 1, tn), lambda l, j: (l, 0, j))],
        out_specs=pl.BlockSpec((1, m, tn), lambda l, j: (l, 0, j)),
        out_shape=jax.ShapeDtypeStruct((depth, m, n), F32),
        compiler_params=_params("parallel", "parallel"),
        name="modulation",
    )(c_rows, w, b.reshape(depth, 1, n))


def _inproj_kernel(x_ref, xp_ref, xn_ref, shift_ref, scale_ref, gain_ref, w_ref, qg_ref, kg_ref,
                   cos_ref, sin_ref, cw_ref, cb_ref,
                   q_ref, k_ref, v_ref, g_ref, vv_ref, x0_ref, us_ref, *, d_attn, d_kv, d_hy):
    i = pl.program_id(1)
    tm = x_ref.shape[1]
    halo = xp_ref.shape[1]
    x = jnp.concatenate([x_ref[0], xp_ref[0], xn_ref[0]], axis=0)
    h = (_rms(x) * gain_ref[...]) * (1.0 + scale_ref[0]) + shift_ref[0]
    hb_all = h.astype(BF16)
    hb = hb_all[:tm]
    cosf = cos_ref[...]
    sinf = sin_ref[...]

    def head_norm_rope(t, gain):
        t = _rms(t) * gain
        return t * cosf + pltpu.roll(t, HEAD_DIM // 2, axis=1) * sinf

    c0 = 0
    q = _dot(hb, w_ref[:, c0:c0 + d_attn])
    for hd in range(d_attn // HEAD_DIM):
        sl = slice(hd * HEAD_DIM, (hd + 1) * HEAD_DIM)
        q_ref[0, :, sl] = head_norm_rope(q[:, sl], qg_ref[...]).astype(q_ref.dtype)
    c0 += d_attn
    k = _dot(hb, w_ref[:, c0:c0 + d_kv])
    for hd in range(d_kv // HEAD_DIM):
        sl = slice(hd * HEAD_DIM, (hd + 1) * HEAD_DIM)
        k_ref[0, :, sl] = head_norm_rope(k[:, sl], kg_ref[...]).astype(k_ref.dtype)
    c0 += d_kv
    v_ref[0] = _dot(hb, w_ref[:, c0:c0 + d_kv]).astype(v_ref.dtype)
    c0 += d_kv
    u_all = _dot(hb_all, w_ref[:, c0:c0 + d_hy])
    c0 += d_hy
    g_ref[0] = _dot(hb, w_ref[:, c0:]).astype(g_ref.dtype)

    u = u_all[:tm]
    has_prev = i > 0
    has_next = i < pl.num_programs(1) - 1
    n_slabs = d_hy // LANES
    for t in range(n_slabs):
        lanes = slice(t * LANES, (t + 1) * LANES)
        us_ref[t, 0:halo, :] = jnp.where(has_prev, u_all[tm:tm + halo, lanes], 0.0)
        us_ref[t, halo:halo + tm, :] = u_all[:tm, lanes]
        us_ref[t, halo + tm:, :] = jnp.where(has_next, u_all[tm + halo:, lanes], 0.0)
    up = jnp.concatenate([us_ref[t, pl.ds(halo - 1, tm), :] for t in range(n_slabs)], axis=1)
    un = jnp.concatenate([us_ref[t, pl.ds(halo + 1, tm), :] for t in range(n_slabs)], axis=1)
    y = up * cw_ref[0:1, :] + u * cw_ref[1:2, :] + un * cw_ref[2:3, :] + cb_ref[...]
    c = d_hy // 3
    x0_ref[0, 0] = y[:, :c].astype(x0_ref.dtype)
    vv_ref[0, 0] = (y[:, 2 * c:] * y[:, c:2 * c]).astype(vv_ref.dtype)


def _in_projection(x, shift, scale, gain, w_bf, qg, kg, cosf, sinf, conv_w, conv_b,
                   *, tm, d_attn, d_kv):
    b, length, d = x.shape
    d_hy = conv_w.shape[-1]
    c = d_hy // 3
    p = b // 2
    d_gate = w_bf.shape[1] - d_attn - 2 * d_kv - d_hy
    halo = SUBLANES
    n_halo = length // halo
    per = tm // halo
    row = lambda bi, i: (bi, i, 0)
    per_b = lambda bi, i: (bi, 0, 0)
    pair_major = lambda bi, i: (bi % p, bi // p, i, 0)
    kern = functools.partial(_inproj_kernel, d_attn=d_attn, d_kv=d_kv, d_hy=d_hy)
    return pl.pallas_call(
        kern,
        grid=(b, length // tm),
        in_specs=[pl.BlockSpec((1, tm, d), row),
                  pl.BlockSpec((1, halo, d), lambda bi, i: (bi, jnp.maximum(i * per - 1, 0), 0)),
                  pl.BlockSpec((1, halo, d),
                               lambda bi, i: (bi, jnp.minimum((i + 1) * per, n_halo - 1), 0)),
                  pl.BlockSpec((1, 1, d), per_b),
                  pl.BlockSpec((1, 1, d), per_b),
                  _resident((1, d)),
                  _resident(w_bf.shape),
                  _resident((1, HEAD_DIM)),
                  _resident((1, HEAD_DIM)),
                  pl.BlockSpec((tm, HEAD_DIM), lambda bi, i: (i, 0)),
                  pl.BlockSpec((tm, HEAD_DIM), lambda bi, i: (i, 0)),
                  _resident((3, d_hy)),
                  _resident((1, d_hy))],
        out_specs=[pl.BlockSpec((1, tm, d_attn), row),
                   pl.BlockSpec((1, tm, d_kv), row),
                   pl.BlockSpec((1, tm, d_kv), row),
                   pl.BlockSpec((1, tm, d_gate), row),
                   pl.BlockSpec((1, 1, tm, c), pair_major),
                   pl.BlockSpec((1, 1, tm, c), pair_major)],
        out_shape=[jax.ShapeDtypeStruct((b, length, d_attn), BF16),
                   jax.ShapeDtypeStruct((b, length, d_kv), BF16),
                   jax.ShapeDtypeStruct((b, length, d_kv), BF16),
                   jax.ShapeDtypeStruct((b, length, d_gate), BF16),
                   jax.ShapeDtypeStruct((p, 2, length, c), BF16),
                   jax.ShapeDtypeStruct((p, 2, length, c), BF16)],
        scratch_shapes=[pltpu.VMEM((d_hy // LANES, tm + 2 * halo, LANES), F32)],
        compiler_params=_params("parallel", "parallel"),
        name="in_projection",
    )(x, x, x, shift, scale, gain, w_bf, qg, kg, cosf, sinf, conv_w, conv_b.reshape(1, d_hy))


def _flash_kernel(*refs, tq, tk, nk, extra):
    if extra:
        (q_ref, qn_ref, k_ref, v_ref, ke_ref, ve_ref, o_ref,
         kall_ref, vall_ref, qs_ref, s_ref, m_ref, acc_ref) = refs
    else:
        q_ref, qn_ref, k_ref, v_ref, o_ref, kall_ref, vall_ref, qs_ref, s_ref, m_ref, acc_ref = refs
    first_slot = 2
    slot_of = lambda j: first_slot if j == 0 else (j - 1) % 2

    def stack_heads(src_ref, which):
        for g in range(KV_GROUP):
            qs_ref[which, g * tq:(g + 1) * tq, :] = src_ref[0, :, g * HEAD_DIM:(g + 1) * HEAD_DIM]

    def chunk(ref, j):
        return ref[j * tk:(j + 1) * tk, :]

    def scores(j, slot, which=0):
        s_ref[slot] = lax.dot_general(qs_ref[which], chunk(kall_ref, j), (((1,), (1,)), ((), ())),
                                      preferred_element_type=F32)

    stack_heads(q_ref, 0)
    stack_heads(qn_ref, 1)

    @pl.when(pl.program_id(2) == 0)
    def _():
        n_main = k_ref.shape[1]
        kall_ref[0:n_main, :] = k_ref[0]
        vall_ref[0:n_main, 0:HEAD_DIM] = v_ref[0]
        if extra:
            kall_ref[n_main:, :] = ke_ref[0]
            vall_ref[n_main:, 0:HEAD_DIM] = ve_ref[0]
        vall_ref[:, HEAD_DIM:] = jnp.ones((vall_ref.shape[0], HEAD_DIM), vall_ref.dtype)
        scores(0, first_slot)

    m_ref[...] = jnp.full(m_ref.shape, -1e30, F32)
    acc_ref[...] = jnp.zeros(acc_ref.shape, F32)
    nt = tk // LANES

    def consume(j, slot):
        tiles = [s_ref[slot, :, t * LANES:(t + 1) * LANES] for t in range(nt)]
        mt = functools.reduce(jnp.maximum, tiles)
        m_prev = m_ref[...]
        m_new = jnp.maximum(m_prev, jnp.max(mt, axis=-1, keepdims=True))
        alpha = jnp.exp2(m_prev - m_new)
        p = jnp.concatenate([jnp.exp2(t - m_new).astype(BF16) for t in tiles], axis=1)
        pv = _dot(p, chunk(vall_ref, j))
        acc_ref[...] = jnp.concatenate([alpha, alpha], axis=1) * acc_ref[...] + pv
        m_ref[...] = m_new

    for j in range(nk - 1):
        scores(j + 1, slot_of(j + 1))
        consume(j, slot_of(j))
    if nk == 1:
        consume(0, first_slot)
        scores(0, first_slot, which=1)
    else:
        scores(0, first_slot, which=1)
        consume(nk - 1, slot_of(nk - 1))

    acc = acc_ref[...]
    out = acc[:, :HEAD_DIM] / acc[:, HEAD_DIM:]
    for g in range(KV_GROUP):
        o_ref[0, :, g * HEAD_DIM:(g + 1) * HEAD_DIM] = out[g * tq:(g + 1) * tq].astype(o_ref.dtype)


def _attention(q, k, v, k_extra=None, v_extra=None, *, tq, tk):
    b, lq, _ = q.shape
    extra = k_extra is not None
    lk = k.shape[1] + (k_extra.shape[1] if extra else 0)
    gw = KV_GROUP * HEAD_DIM
    rows = KV_GROUP * tq
    assert lk % tk == 0
    nq = lq // tq
    kv_spec = lambda a: pl.BlockSpec((1, a.shape[1], HEAD_DIM), lambda bi, h, i: (bi, 0, h))
    kv_args = [k, v] + ([k_extra, v_extra] if extra else [])
    kern = functools.partial(_flash_kernel, tq=tq, tk=tk, nk=lk // tk, extra=extra)
    return pl.pallas_call(
        kern,
        grid=(b, N_KV_HEADS, nq),
        in_specs=[pl.BlockSpec((1, tq, gw), lambda bi, h, i: (bi, i, h)),
                  pl.BlockSpec((1, tq, gw), lambda bi, h, i: (bi, jnp.minimum(i + 1, nq - 1), h))]
                 + [kv_spec(a) for a in kv_args],
        out_specs=pl.BlockSpec((1, tq, gw), lambda bi, h, i: (bi, i, h)),
        out_shape=jax.ShapeDtypeStruct(q.shape, BF16),
        scratch_shapes=[pltpu.VMEM((lk, HEAD_DIM), BF16),
                        pltpu.VMEM((lk, 2 * HEAD_DIM), BF16),
                        pltpu.VMEM((2, rows, HEAD_DIM), BF16),
                        pltpu.VMEM((3, rows, tk), F32),
                        pltpu.VMEM((rows, LANES), F32),
                        pltpu.VMEM((rows, 2 * HEAD_DIM), F32)],
        compiler_params=_params("arbitrary", "arbitrary", "arbitrary"),
        name="attention",
    )(q, q, *kv_args)


def _filter_kernel(z_ref, w1_ref, b1_ref, w2_ref, b2_ref, w3_ref, b3_ref, w4_ref, fr_ref, dec_ref,
                   o_ref):
    c = o_ref.shape[-1]
    h = jnp.sin(fr_ref[0:1, :] * (_dot3(z_ref[...], w1_ref[...]) + b1_ref[...]))
    h = jnp.sin(fr_ref[1:2, :] * (_dot3(h, w2_ref[...]) + b2_ref[...]))
    h = jnp.sin(fr_ref[2:3, :] * (_dot3(h, w3_ref[...]) + b3_ref[...]))
    out = _dot3(h, w4_ref[...])
    o_ref[0] = out[:, :c] * dec_ref[0]
    o_ref[1] = out[:, c:] * dec_ref[1]


def _pad_to(a, shape):
    return jnp.pad(a, [(0, s - d) for d, s in zip(a.shape, shape)])


def _block_diag2(a, b, shape):
    za = jnp.zeros(shape, a.dtype)
    return jnp.concatenate([jnp.concatenate([_pad_to(a, shape), za], axis=1),
                            jnp.concatenate([za, _pad_to(b, shape)], axis=1)], axis=0)


def _hyena_filter(length, fw1, fb1, fw2, fb2, fw3, fb3, fw4, freq, c):
    half = LANES // 2
    assert fw1.shape[0] <= half and fw1.shape[1] <= half
    rev = lambda a: jnp.concatenate([a[:1], a[:0:-1]])
    t = jnp.linspace(0.0, 1.0, length, dtype=F32)
    w = (2.0 * math.pi / length) * jnp.arange(length, dtype=F32)
    f = jnp.linspace(1e-4, FILTER_BANDS - 1, FILTER_BANDS, dtype=F32)[None, :]

    def features(tt, ww):
        tt, ww = tt[:, None], ww[:, None]
        return _pad_to(jnp.concatenate([tt, jnp.cos(f * ww), -jnp.sin(f * ww)], axis=-1),
                       (length, half))

    z = jnp.concatenate([features(t, w), features(rev(t), rev(w))], axis=1)
    max_decay = math.log(DECAY_TARGET) / FAST_DECAY_PCT
    min_decay = math.log(DECAY_TARGET) / SLOW_DECAY_PCT
    deltas = jnp.abs(jnp.linspace(min_decay, max_decay, c, dtype=F32))
    not_first = (jnp.arange(length) > 0).astype(F32)[:, None]
    decay = jnp.stack([jnp.exp(-t[:, None] * deltas),
                       jnp.exp(-rev(t)[:, None] * deltas) * not_first])

    sq = (half, half)
    w1 = _block_diag2(fw1, fw1, sq)
    w2 = _block_diag2(fw2, fw2, sq)
    w3 = _block_diag2(fw3, fw3, sq)
    w4 = _block_diag2(fw4[:, :c], fw4[:, c:], (half, c))
    twice = lambda v: jnp.tile(_pad_to(v.reshape(-1, v.shape[-1]), (v.size // v.shape[-1], half)),
                               (1, 2))
    b1, b2, b3, fr = twice(fb1), twice(fb2), twice(fb3), twice(freq)
    tl = min(length, 1024)
    out = pl.pallas_call(
        _filter_kernel,
        grid=(length // tl,),
        in_specs=[pl.BlockSpec((tl, LANES), lambda i: (i, 0)),
                  _resident(w1.shape), _resident(b1.shape),
                  _resident(w2.shape), _resident(b2.shape),
                  _resident(w3.shape), _resident(b3.shape),
                  _resident(w4.shape), _resident(fr.shape),
                  pl.BlockSpec((2, tl, c), lambda i: (0, i, 0))],
        out_specs=pl.BlockSpec((2, tl, c), lambda i: (0, i, 0)),
        out_shape=jax.ShapeDtypeStruct((2, length, c), F32),
        compiler_params=_params("parallel"),
        name="hyena_filter",
    )(z, w1, b1, w2, b2, w3, b3, w4, fr, decay)
    return out.reshape(2 * length, c)


def _dft_outer_kernel(*refs, nb, c, per_group, hyena):
    if hyena:
        t_ref, z_ref, vv_ref, x0_ref, bias_ref, o_ref = refs
    else:
        t_ref, z_ref, o_ref = refs
    for j in range(nb):
        cols = slice(j * c, (j + 1) * c)
        y = _dot(t_ref[j if per_group else 0], z_ref[0, :, cols].astype(BF16))
        if hyena:
            y = (y + vv_ref[0, :, cols] * bias_ref[...]) * x0_ref[0, :, cols]
        o_ref[0, :, cols] = y.astype(o_ref.dtype)


def _dft_outer(table, z, out_dtype, *, c, hyena_args=None):
    p, k, ncols = z.shape
    nt, m, _ = table.shape
    groups = ncols // c
    nb = min(groups, 8)
    per_group = nt > 1
    t_spec = (pl.BlockSpec((nb, m, k), lambda g, pi: (g, 0, 0)) if per_group
              else _resident((1, m, k)))
    col_spec = lambda rows: pl.BlockSpec((1, rows, nb * c), lambda g, pi: (pi, 0, g))
    in_specs = [t_spec, col_spec(k)]
    args = [table, z]
    if hyena_args is not None:
        vv, x0, bias = hyena_args
        in_specs += [col_spec(m), col_spec(m), _resident((1, c))]
        args += [vv, x0, bias]
    kern = functools.partial(_dft_outer_kernel, nb=nb, c=c, per_group=per_group,
                             hyena=hyena_args is not None)
    return pl.pallas_call(
        kern,
        grid=(groups // nb, p),
        in_specs=in_specs,
        out_specs=col_spec(m),
        out_shape=jax.ShapeDtypeStruct((p, m, ncols), out_dtype),
        compiler_params=_params("parallel", "parallel"),
        name="dft_outer",
    )(*args)


def _dft_strided_kernel(*refs, nb, hyena):
    if hyena:
        t_ref, z_ref, vv_ref, x0_ref, bias_ref, o_ref, zs_ref, os_ref, vs_ref, xs_ref = refs
    else:
        t_ref, z_ref, o_ref, zs_ref, os_ref = refs
    k, cb = z_ref.shape[1], z_ref.shape[3]
    m = o_ref.shape[1]
    nl = cb // LANES
    ng = nb // SUBLANES
    lane = lambda t: slice(t * LANES, (t + 1) * LANES)

    def stage(dst_ref, val):
        for t in range(nl):
            for g in range(ng):
                piece = val[:, g * SUBLANES:(g + 1) * SUBLANES, lane(t)]
                dst_ref[t, g] = piece.reshape(val.shape[0] * SUBLANES, LANES)

    def rows(src_ref, j, n):
        g, r = divmod(j, SUBLANES)
        return jnp.concatenate([src_ref[t, g, pl.ds(r, n, stride=SUBLANES), :]
                                for t in range(nl)], axis=1)

    stage(zs_ref, z_ref[0].astype(F32))
    if hyena:
        stage(vs_ref, vv_ref[0].astype(F32))
        stage(xs_ref, x0_ref[0].astype(F32))
    for j in range(nb):
        y = _dot(t_ref[j], rows(zs_ref, j, k).astype(BF16))
        if hyena:
            y = (y + rows(vs_ref, j, m) * bias_ref[...]) * rows(xs_ref, j, m)
        g, r = divmod(j, SUBLANES)
        for t in range(nl):
            os_ref[t, g, pl.ds(r, m, stride=SUBLANES), :] = y[:, lane(t)]
    o_ref[0] = jnp.concatenate(
        [jnp.concatenate([os_ref[t, g].reshape(m, SUBLANES, LANES) for g in range(ng)], axis=1)
         for t in range(nl)], axis=2).astype(o_ref.dtype)


def _dft_outer_strided(table, z, out_dtype, *, cb, hyena_args=None):
    p, k, n2, c = z.shape
    _, m, _ = table.shape
    nb = 16
    blk = lambda rows: pl.BlockSpec((1, rows, nb, cb), lambda g, ci, pi: (pi, 0, g, ci))
    in_specs = [pl.BlockSpec((nb, m, k), lambda g, ci, pi: (g, 0, 0)), blk(k)]
    args = [table, z]
    staging = lambda rows: pltpu.VMEM((cb // LANES, nb // SUBLANES, rows * SUBLANES, LANES), F32)
    scratch = [staging(k), staging(m)]
    if hyena_args is not None:
        vv, x0, bias = hyena_args
        in_specs += [blk(m), blk(m), pl.BlockSpec((1, cb), lambda g, ci, pi: (0, ci))]
        args += [vv, x0, bias]
        scratch += [staging(m), staging(m)]
    kern = functools.partial(_dft_strided_kernel, nb=nb, hyena=hyena_args is not None)
    return pl.pallas_call(
        kern,
        grid=(n2 // nb, c // cb, p),
        in_specs=in_specs,
        out_specs=blk(m),
        out_shape=jax.ShapeDtypeStruct((p, m, n2, c), out_dtype),
        scratch_shapes=scratch,
        compiler_params=_params("parallel", "parallel", "parallel"),
        name="dft_outer_strided",
    )(*args)


def _dft_inner_kernel(*refs, kc, n2, filtered):
    if filtered:
        mf_ref, mi_ref, x_ref, f_ref, o_ref = refs
    else:
        mf_ref, x_ref, o_ref = refs
    for kk in range(kc):
        zin = jnp.concatenate([x_ref[0, 0, kk], x_ref[0, 1, kk]], axis=0)
        y = _dot(mf_ref[...], zin)
        re, im = y[:n2], y[n2:]
        if filtered:
            fr, fi = f_ref[0, 0, kk], f_ref[0, 1, kk]
            prod = jnp.concatenate([re * fr - im * fi, re * fi + im * fr], axis=0)
            y = _dot(mi_ref[...], prod.astype(BF16))
            re, im = y[:n2], y[n2:]
        o_ref[0, 0, kk] = re.astype(o_ref.dtype)
        o_ref[0, 1, kk] = im.astype(o_ref.dtype)


def _dft_inner(m_fwd, x, out_dtype, m_inv=None, spectrum=None):
    p, _, n1, n2, c = x.shape
    kc = 16
    blk = lambda g, pi: (pi, 0, g, 0, 0)
    filtered = spectrum is not None
    in_specs = [_resident(m_fwd.shape)]
    args = [m_fwd]
    if filtered:
        in_specs.append(_resident(m_inv.shape))
        args.append(m_inv)
    in_specs.append(pl.BlockSpec((1, 2, kc, n2, c), blk))
    args.append(x)
    if filtered:
        in_specs.append(pl.BlockSpec((1, 2, kc, n2, c), lambda g, pi: (0, 0, g, 0, 0)))
        args.append(spectrum)
    kern = functools.partial(_dft_inner_kernel, kc=kc, n2=n2, filtered=filtered)
    return pl.pallas_call(
        kern,
        grid=(n1 // kc, p),
        in_specs=in_specs,
        out_specs=pl.BlockSpec((1, 2, kc, n2, c), blk),
        out_shape=jax.ShapeDtypeStruct(x.shape, out_dtype),
        compiler_params=_params("parallel", "parallel"),
        name="dft_inner",
    )(*args)


def _cmul_kernel(x_ref, f_ref, o_ref, *, n):
    xr, xi = x_ref[0, :n], x_ref[0, n:]
    fr, fi = f_ref[0, :n], f_ref[0, n:]
    o_ref[0, :n] = (xr * fr - xi * fi).astype(o_ref.dtype)
    o_ref[0, n:] = (xr * fi + xi * fr).astype(o_ref.dtype)


def _cmul(x, f, out_dtype):
    p, n2x, c = x.shape
    kern = functools.partial(_cmul_kernel, n=n2x // 2)
    return pl.pallas_call(
        kern,
        grid=(p,),
        in_specs=[pl.BlockSpec((1, n2x, c), lambda pi: (pi, 0, 0)),
                  pl.BlockSpec((1, n2x, c), lambda pi: (0, 0, 0))],
        out_specs=pl.BlockSpec((1, n2x, c), lambda pi: (pi, 0, 0)),
        out_shape=jax.ShapeDtypeStruct(x.shape, out_dtype),
        compiler_params=_params("parallel"),
        name="spectrum_product",
    )(x, f)


def _phase_tables(rows_n1, n1_total, n2_total):
    n = n1_total * n2_total
    n2 = jnp.arange(n2_total, dtype=jnp.int32)[:, None, None]
    k1 = jnp.arange(n1_total, dtype=jnp.int32)[None, :, None]
    n1 = jnp.arange(rows_n1, dtype=jnp.int32)[None, None, :]
    a = ((n1 * k1) % n1_total).astype(F32) * (2.0 * math.pi / n1_total)
    b = ((n2 * k1) % n).astype(F32) * (2.0 * math.pi / n)
    ca, sa, cb, sb = jnp.cos(a), jnp.sin(a), jnp.cos(b), jnp.sin(b)
    return ca * cb - sa * sb, sa * cb + ca * sb


def _forward_tables(length, n1_total, n2_total):
    rows = length // n2_total
    cs, sn = _phase_tables(rows, n1_total, n2_total)
    paired = jnp.concatenate([jnp.concatenate([cs, sn], axis=2),
                              jnp.concatenate([-sn, cs], axis=2)], axis=1)
    cs, sn = _phase_tables(n1_total, n1_total, n2_total)
    real = jnp.concatenate([cs, -sn], axis=1)
    return paired.astype(BF16), real.astype(BF16)


def _inverse_table(length, n1_total, n2_total):
    rows = length // n2_total
    cs, sn = _phase_tables(rows, n1_total, n2_total)
    cs = jnp.swapaxes(cs, 1, 2) / (n1_total * n2_total)
    sn = jnp.swapaxes(sn, 1, 2) / (n1_total * n2_total)
    tb = jnp.concatenate([jnp.concatenate([cs, -sn], axis=2),
                          jnp.concatenate([sn, cs], axis=2)], axis=1)
    return tb.astype(BF16)


def _inner_matrices(n2_total):
    idx = np.arange(n2_total)
    ang = 2.0 * np.pi * ((idx[:, None] * idx[None, :]) % n2_total) / n2_total
    cs, sn = np.cos(ang), np.sin(ang)
    fwd = np.block([[cs, sn], [-sn, cs]])
    inv = np.block([[cs, -sn], [sn, cs]])
    return jnp.asarray(fwd, dtype=BF16), jnp.asarray(inv, dtype=BF16)


def _long_conv_gate(vv, x0, filt, bias):
    p, _, length, c = vv.shape
    n = 2 * length
    n1_total = DFT_N1 if n % DFT_N1 == 0 and n // DFT_N1 >= 8 else n
    n2_total = n // n1_total
    rows = length // n2_total
    t_pair, t_real = _forward_tables(length, n1_total, n2_total)
    t_inv = _inverse_table(length, n1_total, n2_total)

    bias2 = bias.reshape(1, c)
    if n2_total > 1:
        m_fwd, m_inv = _inner_matrices(n2_total)
        shape5 = lambda a: a.reshape(a.shape[0], 2, n1_total, n2_total, c)
        vv4 = vv.reshape(p, 2 * rows, n2_total, c)
        x04 = x0.reshape(p, 2 * rows, n2_total, c)
        spec = _dft_outer_strided(t_real, filt.reshape(1, n1_total, n2_total, c), BF16, cb=c)
        spec = _dft_inner(m_fwd, shape5(spec), F32)
        y = _dft_outer_strided(t_pair, vv4, BF16, cb=c // 2)
        y = _dft_inner(m_fwd, shape5(y), BF16, m_inv=m_inv, spectrum=spec)
        y = y.reshape(p, 2 * n1_total, n2_total, c)
        out = _dft_outer_strided(t_inv, y, BF16, cb=c // 2, hyena_args=(vv4, x04, bias2))
    else:
        vv2 = vv.reshape(p, 2 * rows, c)
        x02 = x0.reshape(p, 2 * rows, c)
        spec = _dft_outer(t_real, filt.reshape(1, n1_total, c), F32, c=c)
        y = _dft_outer(t_pair, vv2, F32, c=c)
        y = _cmul(y, spec, BF16)
        out = _dft_outer(t_inv, y, BF16, c=c, hyena_args=(vv2, x02, bias2))
    return out.reshape(p, 2, length, c)


def _merge_kernel(x_ref, attn_ref, hy_ref, g_ref, gate_ref, woa_ref, woh_ref, wout_ref, o_ref, *, d):
    g = jax.nn.sigmoid(g_ref[0].astype(F32))
    merged = (g[:, :d] * _dot(attn_ref[0], woa_ref[0])
              + g[:, d:] * _dot(hy_ref[0, 0], woh_ref[0]))
    o_ref[0] = x_ref[0] + gate_ref[0] * _dot(merged.astype(BF16), wout_ref[0])


def _merge(x, attn, hyo, g, gate, woa, woh, wout, layer, *, tm):
    b, length, d = x.shape
    p = b // 2
    c = hyo.shape[-1]
    row = lambda bi, i: (bi, i, 0)
    kern = functools.partial(_merge_kernel, d=d)
    return pl.pallas_call(
        kern,
        grid=(b, length // tm),
        in_specs=[pl.BlockSpec((1, tm, d), row),
                  pl.BlockSpec((1, tm, attn.shape[-1]), row),
                  pl.BlockSpec((1, 1, tm, c), lambda bi, i: (bi % p, bi // p, i, 0)),
                  pl.BlockSpec((1, tm, g.shape[-1]), row),
                  pl.BlockSpec((1, 1, d), lambda bi, i: (bi, 0, 0)),
                  _layer_resident(woa, layer), _layer_resident(woh, layer),
                  _layer_resident(wout, layer)],
        out_specs=pl.BlockSpec((1, tm, d), row),
        out_shape=jax.ShapeDtypeStruct(x.shape, F32),
        compiler_params=_params("parallel", "parallel"),
        name="merge",
    )(x, attn, hyo, g, gate, woa, woh, wout)


def _ffn_kernel(x_ref, shift_ref, scale_ref, gate_ref, gain_ref, wgu_ref, wd_ref, fin_ref, o_ref,
                *, d_ff, chunk, final_norm):
    x = x_ref[0]
    h = ((_rms(x) * gain_ref[...]) * (1.0 + scale_ref[0]) + shift_ref[0]).astype(BF16)
    acc = jnp.zeros(x.shape, F32)
    for c0 in range(0, d_ff, chunk):
        c1 = min(c0 + chunk, d_ff)
        gt = _dot(h, wgu_ref[0, :, c0:c1])
        up = _dot(h, wgu_ref[0, :, d_ff + c0:d_ff + c1])
        act = (gt * jax.nn.sigmoid(gt) * up).astype(BF16)
        acc = acc + _dot(act, wd_ref[0, c0:c1, :])
    y = x + gate_ref[0] * acc
    if final_norm:
        y = _rms(y) * fin_ref[...]
    o_ref[0] = y


def _ffn(x, shift, scale, gate, gain, wgu, wd, layer, fin_gain, *, tm, final_norm):
    b, length, d = x.shape
    d_ff = wd.shape[1]
    row = lambda bi, i: (bi, i, 0)
    per_b = lambda bi, i: (bi, 0, 0)
    kern = functools.partial(_ffn_kernel, d_ff=d_ff, chunk=256, final_norm=final_norm)
    return pl.pallas_call(
        kern,
        grid=(b, length // tm),
        in_specs=[pl.BlockSpec((1, tm, d), row),
                  pl.BlockSpec((1, 1, d), per_b),
                  pl.BlockSpec((1, 1, d), per_b),
                  pl.BlockSpec((1, 1, d), per_b),
                  _resident((1, d)),
                  _layer_resident(wgu, layer), _layer_resident(wd, layer),
                  _resident((1, d))],
        out_specs=pl.BlockSpec((1, tm, d), row),
        out_shape=jax.ShapeDtypeStruct(x.shape, F32),
        compiler_params=_params("parallel", "parallel"),
        name="ffn",
    )(x, shift, scale, gate, gain, wgu, wd, fin_gain)


def _rope_tables(length):
    rows = length // GRID_W
    axis_dim = HEAD_DIM // 2
    row = jnp.repeat(jnp.arange(rows, dtype=F32), GRID_W)
    col = jnp.tile(jnp.arange(GRID_W, dtype=F32), rows)
    inv_freq = ROPE_THETA ** (-jnp.arange(0, axis_dim, 2, dtype=F32) / axis_dim)
    ang = jnp.concatenate([row[:, None] * inv_freq, col[:, None] * inv_freq], axis=-1)
    cs, sn = jnp.cos(ang), jnp.sin(ang)
    return jnp.concatenate([cs, cs], axis=-1), jnp.concatenate([-sn, sn], axis=-1)


def _head_perm(n_heads):
    within = np.concatenate([np.arange(0, HEAD_DIM, 2), np.arange(1, HEAD_DIM, 2)])
    return np.concatenate([h * HEAD_DIM + within for h in range(n_heads)]), within


def kernel(x, c, ctx, c_ctx, w_mod, b_mod, norm_mix, w_in, q_norm, k_norm, conv_w, conv_b, filt_w1, filt_b1, filt_w2, filt_b2, filt_w3, filt_b3, filt_w4, filt_freq, hyena_bias, w_o_attn, w_o_hyena, w_out, norm_ffn, w_gate_up, w_down, norm_final):
    b, length, d = x.shape
    ctx_len = ctx.shape[1]
    depth = w_mod.shape[0]
    d_attn = N_HEADS * HEAD_DIM
    d_kv = N_KV_HEADS * HEAD_DIM
    d_hy = conv_w.shape[-1]
    c_hy = d_hy // 3
    tm = ROW_TILE
    tm_ctx = ctx_len
    key_chunk = ATTN_KEY_CHUNK if (ctx_len + length) % ATTN_KEY_CHUNK == 0 else MXU_TILE
    assert b % 2 == 0 and length % FFN_ROW_TILE == 0 and ctx_len % MXU_TILE == 0

    perm, within = _head_perm(N_HEADS + N_KV_HEADS)
    d_qk = d_attn + d_kv

    cosf, sinf = _rope_tables(length)
    cos_ctx = jnp.ones((ctx_len, HEAD_DIM), F32)
    sin_ctx = jnp.zeros((ctx_len, HEAD_DIM), F32)

    m_rows = 2 * SUBLANES
    assert b + 1 <= m_rows
    c_rows = jnp.zeros((m_rows, d), F32).at[:b].set(c).at[b].set(c_ctx)
    mod_layers = _modulation(c_rows, w_mod, b_mod)
    fin = norm_final.reshape(1, d)
    woa, woh, wout = (w.astype(BF16) for w in (w_o_attn, w_o_hyena, w_out))
    wgu, wd = w_gate_up.astype(BF16), w_down.astype(BF16)

    for layer in range(depth):
        last = layer == depth - 1
        mod_all = mod_layers[layer]
        mods = [m.reshape(b, 1, d) for m in jnp.split(mod_all[:b], N_MOD, axis=-1)]
        mods_c = [jnp.broadcast_to(m.reshape(1, 1, d), (b, 1, d))
                  for m in jnp.split(mod_all[b], N_MOD, axis=-1)]
        shift1, scale1, gate1, shift2, scale2, gate2 = mods
        c_shift1, c_scale1, c_gate1, c_shift2, c_scale2, c_gate2 = mods_c

        w_in_bf = jnp.concatenate([w_in[layer][:, :d_qk][:, perm], w_in[layer][:, d_qk:]],
                                  axis=1).astype(BF16)
        qg = (q_norm[layer][within] * (HEAD_DIM ** -0.5 * math.log2(math.e))).reshape(1, HEAD_DIM)
        kg = k_norm[layer][within].reshape(1, HEAD_DIM)
        gain_mix = norm_mix[layer].reshape(1, d)
        gain_ffn = norm_ffn[layer].reshape(1, d)
        filt = (filt_w1[layer], filt_b1[layer], filt_w2[layer], filt_b2[layer],
                filt_w3[layer], filt_b3[layer], filt_w4[layer], filt_freq[layer])
        proj = functools.partial(_in_projection, gain=gain_mix, w_bf=w_in_bf, qg=qg, kg=kg,
                                 conv_w=conv_w[layer], conv_b=conv_b[layer],
                                 d_attn=d_attn, d_kv=d_kv)

        qc, kc, vc, gc, vv_c, x0_c = proj(ctx, c_shift1, c_scale1, cosf=cos_ctx, sinf=sin_ctx,
                                          tm=tm_ctx)
        q, k, v, g, vv, x0 = proj(x, shift1, scale1, cosf=cosf, sinf=sinf, tm=tm)

        attn = _attention(q, k, v, kc, vc, tq=ATTN_Q_TILE, tk=key_chunk)
        hyo = _long_conv_gate(vv, x0, _hyena_filter(length, *filt, c_hy), hyena_bias[layer])
        x_mid = _merge(x, attn, hyo, g, gate1, woa, woh, wout, layer, tm=tm)

        if not last:
            attn_c = _attention(qc, kc, vc, tq=ctx_len, tk=ctx_len)
            hyo_c = _long_conv_gate(vv_c, x0_c, _hyena_filter(ctx_len, *filt, c_hy),
                                    hyena_bias[layer])
            ctx = _merge(ctx, attn_c, hyo_c, gc, c_gate1, woa, woh, wout, layer, tm=tm_ctx)
            ctx = _ffn(ctx, c_shift2, c_scale2, c_gate2, gain_ffn, wgu, wd, layer, fin,
                       tm=tm_ctx, final_norm=False)

        x = _ffn(x_mid, shift2, scale2, gate2, gain_ffn, wgu, wd, layer, fin, tm=FFN_ROW_TILE,
                 final_norm=last)
    return x
```
